```python
import math
import jax
import jax.numpy as jnp
from jax import lax
import numpy as np

D_MODEL = 4096
BATCH = 4
SEQ = 2048
DEPTH = 2
DEC_BATCH = 8
DEC_SEQ = 4
PAST_LEN = 16384
PAGE_SIZE = 128

N_MIXERS = 2
N_NSA_LAYERS = (DEPTH + 1) // 2
N_GLA_LAYERS = DEPTH // 2
NSA_HEADS = 32
NSA_HEAD_DIM = D_MODEL // NSA_HEADS
NSA_KV_HEADS = 4
NSA_Q_PER_KV = NSA_HEADS // NSA_KV_HEADS
CMP_STRIDE = 16
CMP_LEN = 2 * CMP_STRIDE
SEL_BLOCK = 64
N_SEL = 16
WINDOW = 512
NSA_QBLOCK = 32
NSA_QD = NSA_HEADS * NSA_HEAD_DIM
NSA_KVD = NSA_KV_HEADS * NSA_HEAD_DIM
NSA_IN = NSA_QD + 6 * NSA_KVD + 3 * NSA_HEADS
GLA_HEADS = 8
GLA_DK = D_MODEL // (2 * GLA_HEADS)
GLA_DV = D_MODEL // GLA_HEADS
GLA_RANK = 16
GLA_TAU = 16.0
GLA_CHUNK = 32
GLA_IN = 2 * GLA_HEADS * GLA_DK + 2 * GLA_HEADS * GLA_DV + GLA_RANK
D_FF = 11008
PLE_DIM = 256
EPS = 1e-6
NEG_INF = -1e30
FORCE_SCORE = 1e30

kernel_name = 'nsa_gla_hybrid_step'


def rms_norm(x, g):
    xf = x.astype(jnp.float32)
    xf = xf * lax.rsqrt(jnp.mean(xf * xf, axis=-1, keepdims=True) + EPS)
    return (xf * g.astype(jnp.float32)).astype(x.dtype)


def swiglu_half(x, g, w_gu, w_down):
    a, u = jnp.split(rms_norm(x, g) @ w_gu, 2, axis=-1)
    return x + 0.5 * ((jax.nn.silu(a) * u) @ w_down)


def ple_add(x, p_i, g, w_gate, w_proj):
    gate = jax.nn.sigmoid(rms_norm(x, g) @ w_gate)
    return x + gate * (p_i.astype(x.dtype) @ w_proj)


def alibi_slopes():
    h = jnp.arange(NSA_HEADS, dtype=jnp.float32)
    return jnp.exp2(-8.0 * (h + 1.0) / NSA_HEADS).reshape(NSA_KV_HEADS, NSA_Q_PER_KV)


def pad_time(a, total):
    pad = [(0, 0)] * a.ndim
    pad[1] = (0, total - a.shape[1])
    return jnp.pad(a, pad)


def nsa_project(h, w_in, gate_bias):
    B, T, _ = h.shape
    z = h @ w_in
    q = z[..., :NSA_QD].reshape(B, T, NSA_KV_HEADS, NSA_Q_PER_KV, NSA_HEAD_DIM) * (NSA_HEAD_DIM ** -0.5)
    kv = z[..., NSA_QD:NSA_QD + 6 * NSA_KVD].reshape(B, T, 3, 2, NSA_KV_HEADS, NSA_HEAD_DIM)
    g = jax.nn.sigmoid((z[..., NSA_QD + 6 * NSA_KVD:] + gate_bias).astype(jnp.float32))
    g = g.reshape(B, T, 3, NSA_KV_HEADS, NSA_Q_PER_KV)
    return q, kv[:, :, 0], kv[:, :, 1], kv[:, :, 2], g


def compress_kv(kv_all, w_cmp, pe_cmp):
    B, Tp = kv_all.shape[:2]
    ch = kv_all.reshape(B, Tp // CMP_STRIDE, CMP_STRIDE, 2, NSA_KV_HEADS, NSA_HEAD_DIM)
    first = jnp.einsum('bnlcgd,clde->bncge', ch, w_cmp[:, :CMP_STRIDE])
    second = jnp.einsum('bnlcgd,clde->bncge', ch, w_cmp[:, CMP_STRIDE:])
    bias = jnp.einsum('cld,clde->ce', pe_cmp, w_cmp)
    return first[:, :-1] + second[:, 1:] + bias[:, None, :]


def cmp_branch(q, t_pos, ckv, slopes):
    n = ckv.shape[1]
    ck, cv = ckv[:, :, 0], ckv[:, :, 1]
    end = CMP_STRIDE * jnp.arange(n) + (CMP_LEN - 1)
    dist = (t_pos[:, None] - end[None, :]).astype(jnp.float32)
    vis = dist >= 0
    s = jnp.einsum('btgrd,bngd->bgrtn', q, ck).astype(jnp.float32)
    s = jnp.where(vis, s - slopes[:, :, None, None] * dist, NEG_INF)
    p = jnp.where(vis, jax.nn.softmax(s, axis=-1), 0.0)
    o = jnp.einsum('bgrtn,bngd->btgrd', p.astype(cv.dtype), cv)
    return o, p


def select_blocks(p, t_pos, n_slc):
    n_cmp = p.shape[-1]
    ci = jnp.arange(n_cmp)[:, None]
    sj = jnp.arange(n_slc)[None, :]
    ov = ((CMP_STRIDE * ci < SEL_BLOCK * (sj + 1)) & (CMP_STRIDE * ci + CMP_LEN > SEL_BLOCK * sj)).astype(jnp.float32)
    imp = jnp.einsum('bgrtn,nj->bgtj', p, ov)
    cur = (t_pos // SEL_BLOCK)[:, None]
    j = jnp.arange(n_slc)[None, :]
    vis = j <= cur
    forced = (j == 0) | (j == cur) | (j == cur - 1)
    score = jnp.where(forced, FORCE_SCORE, jnp.where(vis, imp, -FORCE_SCORE))
    _, idx = lax.top_k(score, min(N_SEL, n_slc))
    valid = jnp.take_along_axis(jnp.broadcast_to(vis, score.shape), idx, axis=-1)
    return idx, valid


def nsa_global(q, t_pos, kv_cmp_all, kv_sel_all, w_cmp, pe_cmp, slopes):
    B, T_all = kv_cmp_all.shape[:2]
    Tp = -(-T_all // SEL_BLOCK) * SEL_BLOCK
    ckv = compress_kv(pad_time(kv_cmp_all, Tp), w_cmp, pe_cmp)
    o_cmp, p = cmp_branch(q, t_pos, ckv, slopes)
    n_slc = Tp // SEL_BLOCK
    idx, valid = select_blocks(p, t_pos, n_slc)
    blk = pad_time(kv_sel_all, Tp).reshape(B, n_slc, SEL_BLOCK, 2, NSA_KV_HEADS, NSA_HEAD_DIM)
    blk = blk.transpose(3, 0, 4, 1, 2, 5)
    return o_cmp, idx, valid, blk[0], blk[1]


def sel_branch(q_c, t_c, idx_c, valid_c, kb, vb, slopes):
    B, QC = q_c.shape[:2]
    k = idx_c.shape[-1]
    bi = jnp.arange(B)[:, None, None, None]
    gi = jnp.arange(NSA_KV_HEADS)[None, :, None, None]
    kg = kb[bi, gi, idx_c].reshape(B, NSA_KV_HEADS, QC, k * SEL_BLOCK, NSA_HEAD_DIM)
    vg = vb[bi, gi, idx_c].reshape(B, NSA_KV_HEADS, QC, k * SEL_BLOCK, NSA_HEAD_DIM)
    s_pos = idx_c[..., None] * SEL_BLOCK + jnp.arange(SEL_BLOCK)
    dist = t_c[None, None, :, None, None] - s_pos
    mask = ((dist >= 0) & valid_c[..., None]).reshape(B, NSA_KV_HEADS, QC, k * SEL_BLOCK)
    dist = dist.astype(jnp.float32).reshape(B, NSA_KV_HEADS, QC, k * SEL_BLOCK)
    s = jnp.einsum('bqgrd,bgqsd->bgrqs', q_c, kg).astype(jnp.float32)
    s = jnp.where(mask[:, :, None], s - slopes[None, :, :, None, None] * dist[:, :, None], NEG_INF)
    p = jax.nn.softmax(s, axis=-1)
    return jnp.einsum('bgrqs,bgqsd->bqgrd', p.astype(vg.dtype), vg)


def win_branch(q_c, t_c, kw, vw, s_pos, slopes):
    dist = t_c[:, None] - s_pos[None, :]
    mask = (dist >= 0) & (dist <= WINDOW) & (s_pos >= 0)[None, :]
    s = jnp.einsum('bqgrd,bsgd->bgrqs', q_c, kw).astype(jnp.float32)
    s = jnp.where(mask, s - slopes[:, :, None, None] * dist.astype(jnp.float32), NEG_INF)
    p = jax.nn.softmax(s, axis=-1)
    return jnp.einsum('bgrqs,bsgd->bqgrd', p.astype(vw.dtype), vw)


def nsa_output(g, o_cmp, o_sel, o_win, w_out):
    B, T = g.shape[:2]
    o = g[:, :, 0, :, :, None] * o_cmp + g[:, :, 1, :, :, None] * o_sel + g[:, :, 2, :, :, None] * o_win
    return o.astype(w_out.dtype).reshape(B, T, NSA_QD) @ w_out


def nsa_prompt(h, w_in, gate_bias, w_cmp, pe_cmp, w_out):
    slopes = alibi_slopes()
    B, T, _ = h.shape
    q, kv_cmp, kv_sel, kv_win, g = nsa_project(h, w_in, gate_bias)
    t_pos = jnp.arange(T)
    o_cmp, idx, valid, kb, vb = nsa_global(q, t_pos, kv_cmp, kv_sel, w_cmp, pe_cmp, slopes)
    kw_pad = jnp.pad(kv_win, ((0, 0), (WINDOW, 0), (0, 0), (0, 0), (0, 0)))
    qc = NSA_QBLOCK if T % NSA_QBLOCK == 0 else T

    def block(c):
        q0 = c * qc
        q_c = lax.dynamic_slice_in_dim(q, q0, qc, axis=1)
        t_c = q0 + jnp.arange(qc)
        idx_c = lax.dynamic_slice_in_dim(idx, q0, qc, axis=2)
        valid_c = lax.dynamic_slice_in_dim(valid, q0, qc, axis=2)
        kw_c = lax.dynamic_slice_in_dim(kw_pad, q0, WINDOW + qc, axis=1)
        s_pos = q0 - WINDOW + jnp.arange(WINDOW + qc)
        o_s = sel_branch(q_c, t_c, idx_c, valid_c, kb, vb, slopes)
        o_w = win_branch(q_c, t_c, kw_c[:, :, 0], kw_c[:, :, 1], s_pos, slopes)
        return o_s, o_w

    o_sel, o_win = lax.map(block, jnp.arange(T // qc))
    o_sel = jnp.moveaxis(o_sel, 0, 1).reshape(q.shape)
    o_win = jnp.moveaxis(o_win, 0, 1).reshape(q.shape)
    out = nsa_output(g, o_cmp, o_sel, o_win, w_out)
    return out, kv_cmp, kv_sel, kv_win[:, T - min(WINDOW, T):]


def nsa_sample(h, past_cmp, past_sel, win_buf, w_in, gate_bias, w_cmp, pe_cmp, w_out):
    slopes = alibi_slopes()
    B, T, _ = h.shape
    q, kv_cmp, kv_sel, kv_win, g = nsa_project(h, w_in, gate_bias)
    t_pos = PAST_LEN + jnp.arange(T)
    kv_cmp_all = jnp.concatenate([past_cmp, kv_cmp.astype(past_cmp.dtype)], axis=1)
    kv_sel_all = jnp.concatenate([past_sel, kv_sel.astype(past_sel.dtype)], axis=1)
    o_cmp, idx, valid, kb, vb = nsa_global(q, t_pos, kv_cmp_all, kv_sel_all, w_cmp, pe_cmp, slopes)
    wb = win_buf.shape[1]
    kw_all = jnp.concatenate([win_buf, kv_win.astype(win_buf.dtype)], axis=1)
    s_pos = PAST_LEN - wb + jnp.arange(wb + T)
    o_sel = sel_branch(q, t_pos, idx, valid, kb, vb, slopes)
    o_win = win_branch(q, t_pos, kw_all[:, :, 0], kw_all[:, :, 1], s_pos, slopes)
    out = nsa_output(g, o_cmp, o_sel, o_win, w_out)
    return out, kv_cmp, kv_sel, kw_all[:, T:]


def gla_recurrence(q, k, v, log_a, s0):
    B, T, H, DK = q.shape
    DV = v.shape[-1]
    C = min(GLA_CHUNK, T)
    Tp = -(-T // C) * C
    N = Tp // C
    f32 = jnp.float32
    qf = pad_time(q.astype(f32) * (DK ** -0.5), Tp).reshape(B, N, C, H, DK)
    kf = pad_time(k.astype(f32), Tp).reshape(B, N, C, H, DK)
    vf = pad_time(v.astype(f32), Tp).reshape(B, N, C, H, DV)
    b = jnp.cumsum(pad_time(log_a, Tp).reshape(B, N, C, H, DK), axis=2)
    b_last = b[:, :, -1]
    qe = qf * jnp.exp(b)
    ke = kf * jnp.exp(-b)
    kd = kf * jnp.exp(b_last[:, :, None] - b)
    causal = jnp.tril(jnp.ones((C, C), dtype=bool))
    att = jnp.where(causal, jnp.einsum('bnthd,bnshd->bnhts', qe, ke), 0.0)
    o_intra = jnp.einsum('bnhts,bnshe->bnthe', att, vf)

    def step(s, xs_n):
        qe_n, kd_n, v_n, dec_n = xs_n
        o_n = jnp.einsum('bthd,bhde->bthe', qe_n, s)
        s = dec_n[..., None] * s + jnp.einsum('bthd,bthe->bhde', kd_n, v_n)
        return s, o_n

    xs = (jnp.moveaxis(qe, 1, 0), jnp.moveaxis(kd, 1, 0), jnp.moveaxis(vf, 1, 0),
          jnp.moveaxis(jnp.exp(b_last), 1, 0))
    s_fin, o_inter = lax.scan(step, s0.astype(f32), xs)
    o = (o_intra + jnp.moveaxis(o_inter, 0, 1)).reshape(B, Tp, H, DV)[:, :T]
    return o.astype(v.dtype), s_fin.astype(s0.dtype)


def gla_mixer(h, s0, w_in, w_alpha, b_alpha, norm_g, w_out):
    B, T, _ = h.shape
    qk = GLA_HEADS * GLA_DK
    vd = GLA_HEADS * GLA_DV
    z = h @ w_in
    q = z[..., :qk].reshape(B, T, GLA_HEADS, GLA_DK)
    k = z[..., qk:2 * qk].reshape(B, T, GLA_HEADS, GLA_DK)
    v = z[..., 2 * qk:2 * qk + vd].reshape(B, T, GLA_HEADS, GLA_DV)
    r = z[..., 2 * qk + vd:2 * qk + 2 * vd]
    a = z[..., 2 * qk + 2 * vd:]
    log_a = jax.nn.log_sigmoid((a @ w_alpha + b_alpha).astype(jnp.float32)) / GLA_TAU
    o, s_new = gla_recurrence(q, k, v, log_a.reshape(B, T, GLA_HEADS, GLA_DK), s0)
    o = rms_norm(o, norm_g).reshape(B, T, vd) * jax.nn.silu(r)
    return o @ w_out, s_new


def setup_inputs(seed: int = 0) -> dict:
    key = jax.random.key(seed)
    ks = iter(jax.random.split(key, 40))

    def nrm(shape, scale):
        return jax.random.normal(next(ks), shape, jnp.float32) * scale

    def gain(shape):
        return 1.0 + nrm(shape, 0.05)

    n_pages = PAST_LEN // PAGE_SIZE
    n_used = DEC_BATCH * n_pages
    n_pool = n_used + max(1, n_used // 4)
    win_buf = min(WINDOW, PAST_LEN)
    page_table = jax.random.permutation(next(ks), n_pool)[:n_used].reshape(DEC_BATCH, n_pages).astype(jnp.int32)
    return {
        'x_prompt': nrm((BATCH, SEQ, D_MODEL), 1.0),
        'x_sample': nrm((DEC_BATCH, DEC_SEQ, D_MODEL), 1.0),
        'p_prompt': nrm((DEPTH, BATCH, SEQ, PLE_DIM), 1.0),
        'p_sample': nrm((DEPTH, DEC_BATCH, DEC_SEQ, PLE_DIM), 1.0),
        'cache_cmp_kv': nrm((N_NSA_LAYERS, n_pool, PAGE_SIZE, 2, NSA_KV_HEADS, NSA_HEAD_DIM), 1.0),
        'cache_sel_kv': nrm((N_NSA_LAYERS, n_pool, PAGE_SIZE, 2, NSA_KV_HEADS, NSA_HEAD_DIM), 1.0),
        'cache_win_kv': nrm((N_NSA_LAYERS, DEC_BATCH, win_buf, 2, NSA_KV_HEADS, NSA_HEAD_DIM), 1.0),
        'state_gla': nrm((N_GLA_LAYERS, DEC_BATCH, GLA_HEADS, GLA_DK, GLA_DV), 1.0),
        'page_table': page_table,
        'ffn1_norm': gain((DEPTH, D_MODEL)),
        'ffn1_w_gu': nrm((DEPTH, D_MODEL, 2 * D_FF), D_MODEL ** -0.5),
        'ffn1_w_down': nrm((DEPTH, D_FF, D_MODEL), D_FF ** -0.5),
        'mix_norm': gain((DEPTH, D_MODEL)),
        'nsa_w_in': nrm((N_NSA_LAYERS, D_MODEL, NSA_IN), D_MODEL ** -0.5),
        'nsa_gate_bias': nrm((N_NSA_LAYERS, 3 * NSA_HEADS), 0.1),
        'nsa_w_cmp': nrm((N_NSA_LAYERS, 2, CMP_LEN, NSA_HEAD_DIM, NSA_HEAD_DIM), (CMP_LEN * NSA_HEAD_DIM) ** -0.5),
        'nsa_pe_cmp': nrm((N_NSA_LAYERS, 2, CMP_LEN, NSA_HEAD_DIM), 0.1),
        'nsa_w_out': nrm((N_NSA_LAYERS, NSA_QD, D_MODEL), NSA_QD ** -0.5),
        'gla_w_in': nrm((N_GLA_LAYERS, D_MODEL, GLA_IN), D_MODEL ** -0.5),
        'gla_w_alpha': nrm((N_GLA_LAYERS, GLA_RANK, GLA_HEADS * GLA_DK), GLA_RANK ** -0.5),
        'gla_b_alpha': nrm((N_GLA_LAYERS, GLA_HEADS * GLA_DK), 0.5),
        'gla_norm': gain((N_GLA_LAYERS, GLA_DV)),
        'gla_w_out': nrm((N_GLA_LAYERS, GLA_HEADS * GLA_DV, D_MODEL), (GLA_HEADS * GLA_DV) ** -0.5),
        'ffn2_norm': gain((DEPTH, D_MODEL)),
        'ffn2_w_gu': nrm((DEPTH, D_MODEL, 2 * D_FF), D_MODEL ** -0.5),
        'ffn2_w_down': nrm((DEPTH, D_FF, D_MODEL), D_FF ** -0.5),
        'ple_norm': gain((DEPTH, D_MODEL)),
        'ple_w_gate': nrm((DEPTH, D_MODEL, D_MODEL), D_MODEL ** -0.5),
        'ple_w_proj': nrm((DEPTH, PLE_DIM, D_MODEL), PLE_DIM ** -0.5),
        'final_norm': gain((D_MODEL,)),
    }


def reference(x_prompt, x_sample, p_prompt, p_sample, cache_cmp_kv, cache_sel_kv, cache_win_kv, state_gla,
              page_table, ffn1_norm, ffn1_w_gu, ffn1_w_down, mix_norm, nsa_w_in, nsa_gate_bias, nsa_w_cmp,
              nsa_pe_cmp, nsa_w_out, gla_w_in, gla_w_alpha, gla_b_alpha, gla_norm, gla_w_out, ffn2_norm,
              ffn2_w_gu, ffn2_w_down, ple_norm, ple_w_gate, ple_w_proj, final_norm):
    db, n_pages = page_table.shape
    bp = x_prompt.shape[0]
    xp, xs = x_prompt, x_sample
    cmp_p, sel_p, win_p, gla_p = [], [], [], []
    cmp_s, sel_s, win_s, gla_s = [], [], [], []
    for i in range(DEPTH):
        xp = swiglu_half(xp, ffn1_norm[i], ffn1_w_gu[i], ffn1_w_down[i])
        xs = swiglu_half(xs, ffn1_norm[i], ffn1_w_gu[i], ffn1_w_down[i])
        hp = rms_norm(xp, mix_norm[i])
        hs = rms_norm(xs, mix_norm[i])
        j = i // N_MIXERS
        if i % N_MIXERS == 0:
            w = (nsa_w_in[j], nsa_gate_bias[j], nsa_w_cmp[j], nsa_pe_cmp[j], nsa_w_out[j])
            op, kc, kl, kw = nsa_prompt(hp, *w)
            past_cmp = cache_cmp_kv[j][page_table].reshape(db, n_pages * PAGE_SIZE, 2, NSA_KV_HEADS, NSA_HEAD_DIM)
            past_sel = cache_sel_kv[j][page_table].reshape(db, n_pages * PAGE_SIZE, 2, NSA_KV_HEADS, NSA_HEAD_DIM)
            os_, kc2, kl2, kw2 = nsa_sample(hs, past_cmp, past_sel, cache_win_kv[j], *w)
            cmp_p.append(kc)
            sel_p.append(kl)
            win_p.append(kw)
            cmp_s.append(kc2)
            sel_s.append(kl2)
            win_s.append(kw2)
        else:
            w = (gla_w_in[j], gla_w_alpha[j], gla_b_alpha[j], gla_norm[j], gla_w_out[j])
            s0 = jnp.zeros((bp, GLA_HEADS, GLA_DK, GLA_DV), state_gla.dtype)
            op, sp = gla_mixer(hp, s0, *w)
            os_, ss = gla_mixer(hs, state_gla[j], *w)
            gla_p.append(sp)
            gla_s.append(ss)
        xp = xp + op
        xs = xs + os_
        xp = swiglu_half(xp, ffn2_norm[i], ffn2_w_gu[i], ffn2_w_down[i])
        xs = swiglu_half(xs, ffn2_norm[i], ffn2_w_gu[i], ffn2_w_down[i])
        xp = ple_add(xp, p_prompt[i], ple_norm[i], ple_w_gate[i], ple_w_proj[i])
        xs = ple_add(xs, p_sample[i], ple_norm[i], ple_w_gate[i], ple_w_proj[i])
    y_prompt = rms_norm(xp, final_norm)
    y_sample = rms_norm(xs, final_norm)
    return (y_prompt, y_sample, jnp.stack(cmp_p), jnp.stack(sel_p), jnp.stack(win_p), jnp.stack(gla_p),
            jnp.stack(cmp_s), jnp.stack(sel_s), jnp.stack(win_s), jnp.stack(gla_s))
```

```python
import functools
import math

import jax
import jax.numpy as jnp
from jax import lax
from jax.experimental import pallas as pl
from jax.experimental.pallas import tpu as pltpu

F32 = jnp.float32
BF16 = jnp.bfloat16

D_MODEL = 4096
DEPTH = 2
PAST_LEN = 16384
PAGE_SIZE = 128
NSA_HEADS = 32
NSA_HEAD_DIM = 128
NSA_KV_HEADS = 4
NSA_Q_PER_KV = 8
CMP_STRIDE = 16
CMP_LEN = 32
SEL_BLOCK = 64
N_SEL = 16
WINDOW = 512
NSA_QBLOCK = 32
NSA_QD = NSA_HEADS * NSA_HEAD_DIM
NSA_KVD = NSA_KV_HEADS * NSA_HEAD_DIM
GLA_HEADS = 8
GLA_DK = 256
GLA_DV = 512
GLA_RANK = 16
GLA_TAU = 16.0
GLA_CHUNK = 32
D_FF = 11008
EPS = 1e-6
NEG_INF = -1e30
FORCE_SCORE = 1e30

V7X_VMEM_LIMIT_BYTES = 56 * 1024 * 1024
LANE = 128


def _params(*sem):
    return pltpu.CompilerParams(dimension_semantics=sem, vmem_limit_bytes=V7X_VMEM_LIMIT_BYTES)


def _norm_rows(x, g):
    ms = jnp.mean(x * x, axis=-1, keepdims=True)
    return (x * lax.rsqrt(ms + EPS)) * g


def _row_tile(m, want):
    return want if m % want == 0 else m


def _ffn_kernel(x_ref, g_ref, wa_ref, wu_ref, wd_ref, o_ref, h_ref):
    @pl.when(pl.program_id(1) == 0)
    def _():
        x = x_ref[...]
        h_ref[...] = _norm_rows(x, g_ref[...]).astype(BF16)
        o_ref[...] = x

    h = h_ref[...]
    a = jnp.dot(h, wa_ref[...], preferred_element_type=F32)
    u = jnp.dot(h, wu_ref[...], preferred_element_type=F32)
    act = (0.5 * (a * jax.nn.sigmoid(a)) * u).astype(BF16)
    o_ref[...] += jnp.dot(act, wd_ref[...], preferred_element_type=F32)


def ffn(x, g, w_gu, w_down, tm, tf=256):
    m, d = x.shape
    nf = D_FF // tf
    return pl.pallas_call(
        _ffn_kernel,
        grid=(m // tm, nf),
        in_specs=[
            pl.BlockSpec((tm, d), lambda i, j: (i, 0), pipeline_mode=pl.Buffered(1)),
            pl.BlockSpec((1, d), lambda i, j: (0, 0)),
            pl.BlockSpec((d, tf), lambda i, j: (0, j)),
            pl.BlockSpec((d, tf), lambda i, j: (0, j + nf)),
            pl.BlockSpec((tf, d), lambda i, j: (j, 0)),
        ],
        out_specs=pl.BlockSpec((tm, d), lambda i, j: (i, 0)),
        out_shape=jax.ShapeDtypeStruct((m, d), F32),
        scratch_shapes=[pltpu.VMEM((tm, d), BF16)],
        compiler_params=_params("parallel", "arbitrary"),
    )(x, g.reshape(1, d), w_gu, w_gu, w_down)


def _norm_matmul_kernel(x_ref, g_ref, w_ref, o_ref, h_ref):
    @pl.when(pl.program_id(1) == 0)
    def _():
        h_ref[...] = _norm_rows(x_ref[...], g_ref[...]).astype(BF16)

    o_ref[...] = jnp.dot(h_ref[...], w_ref[...], preferred_element_type=F32)


def norm_matmul(x, g, w, tm, tn=512):
    m, d = x.shape
    n = w.shape[1]
    return pl.pallas_call(
        _norm_matmul_kernel,
        grid=(m // tm, n // tn),
        in_specs=[
            pl.BlockSpec((tm, d), lambda i, j: (i, 0)),
            pl.BlockSpec((1, d), lambda i, j: (0, 0)),
            pl.BlockSpec((d, tn), lambda i, j: (0, j)),
        ],
        out_specs=pl.BlockSpec((tm, tn), lambda i, j: (i, j)),
        out_shape=jax.ShapeDtypeStruct((m, n), F32),
        scratch_shapes=[pltpu.VMEM((tm, d), BF16)],
        compiler_params=_params("parallel", "arbitrary"),
    )(x, g.reshape(1, d), w)


def _matmul_residual_kernel(a_ref, w_ref, x_ref, o_ref):
    o_ref[...] = x_ref[...] + jnp.dot(a_ref[...], w_ref[...], preferred_element_type=F32)


def matmul_residual(a, w, x, tm, tn=512):
    m, k = a.shape
    n = w.shape[1]
    return pl.pallas_call(
        _matmul_residual_kernel,
        grid=(m // tm, n // tn),
        in_specs=[
            pl.BlockSpec((tm, k), lambda i, j: (i, 0)),
            pl.BlockSpec((k, tn), lambda i, j: (0, j)),
            pl.BlockSpec((tm, tn), lambda i, j: (i, j)),
        ],
        out_specs=pl.BlockSpec((tm, tn), lambda i, j: (i, j)),
        out_shape=jax.ShapeDtypeStruct((m, n), F32),
        compiler_params=_params("parallel", "arbitrary"),
    )(a, w, x)


def _ple_kernel(tn, x_ref, g_ref, wg_ref, p_ref, wp_ref, o_ref, h_ref):
    j = pl.program_id(1)

    @pl.when(j == 0)
    def _():
        h_ref[...] = _norm_rows(x_ref[...], g_ref[...]).astype(BF16)

    gate = jax.nn.sigmoid(jnp.dot(h_ref[...], wg_ref[...], preferred_element_type=F32))
    proj = jnp.dot(p_ref[...].astype(BF16), wp_ref[...], preferred_element_type=F32)
    col = pl.multiple_of(j * tn, tn)
    o_ref[...] = x_ref[:, pl.ds(col, tn)] + gate * proj


def ple(x, p, g, w_gate, w_proj, tm, tn=512):
    m, d = x.shape
    pd = p.shape[1]
    return pl.pallas_call(
        functools.partial(_ple_kernel, tn),
        grid=(m // tm, d // tn),
        in_specs=[
            pl.BlockSpec((tm, d), lambda i, j: (i, 0)),
            pl.BlockSpec((1, d), lambda i, j: (0, 0)),
            pl.BlockSpec((d, tn), lambda i, j: (0, j)),
            pl.BlockSpec((tm, pd), lambda i, j: (i, 0)),
            pl.BlockSpec((pd, tn), lambda i, j: (0, j)),
        ],
        out_specs=pl.BlockSpec((tm, tn), lambda i, j: (i, j)),
        out_shape=jax.ShapeDtypeStruct((m, d), F32),
        scratch_shapes=[pltpu.VMEM((tm, d), BF16)],
        compiler_params=_params("parallel", "arbitrary"),
    )(x, g.reshape(1, d), w_gate, p, w_proj)


def _rmsnorm_kernel(x_ref, g_ref, o_ref):
    o_ref[...] = _norm_rows(x_ref[...], g_ref[...])


def rmsnorm(x, g, tm):
    m, d = x.shape
    return pl.pallas_call(
        _rmsnorm_kernel,
        grid=(m // tm,),
        in_specs=[pl.BlockSpec((tm, d), lambda i: (i, 0)), pl.BlockSpec((1, d), lambda i: (0, 0))],
        out_specs=pl.BlockSpec((tm, d), lambda i: (i, 0)),
        out_shape=jax.ShapeDtypeStruct((m, d), F32),
        compiler_params=_params("parallel"),
    )(x, g.reshape(1, d))


def alibi_slopes():
    h = jnp.arange(NSA_HEADS, dtype=F32)
    return jnp.exp2(-8.0 * (h + 1.0) / NSA_HEADS).reshape(NSA_KV_HEADS, NSA_Q_PER_KV)


def pad_time(a, total):
    pad = [(0, 0)] * a.ndim
    pad[1] = (0, total - a.shape[1])
    return jnp.pad(a, pad)


def nsa_split(z, gate_bias, b, t):
    q = z[:, :NSA_QD].reshape(b, t, NSA_KV_HEADS, NSA_Q_PER_KV, NSA_HEAD_DIM) * (NSA_HEAD_DIM ** -0.5)
    kv = z[:, NSA_QD:NSA_QD + 6 * NSA_KVD].reshape(b, t, 3, 2, NSA_KV_HEADS, NSA_HEAD_DIM)
    g = jax.nn.sigmoid(z[:, NSA_QD + 6 * NSA_KVD:NSA_QD + 6 * NSA_KVD + 3 * NSA_HEADS] + gate_bias)
    g = g.reshape(b, t, 3, NSA_KV_HEADS, NSA_Q_PER_KV)
    return q, kv[:, :, 0], kv[:, :, 1], kv[:, :, 2], g


def compress_kv(kv_all, w_cmp, pe_cmp):
    B, Tp = kv_all.shape[:2]
    ch = kv_all.reshape(B, Tp // CMP_STRIDE, CMP_STRIDE, 2, NSA_KV_HEADS, NSA_HEAD_DIM)
    first = jnp.einsum('bnlcgd,clde->bncge', ch, w_cmp[:, :CMP_STRIDE])
    second = jnp.einsum('bnlcgd,clde->bncge', ch, w_cmp[:, CMP_STRIDE:])
    bias = jnp.einsum('cld,clde->ce', pe_cmp, w_cmp)
    return first[:, :-1] + second[:, 1:] + bias[:, None, :]


def cmp_branch(q, t_pos, ckv, slopes):
    n = ckv.shape[1]
    ck, cv = ckv[:, :, 0], ckv[:, :, 1]
    end = CMP_STRIDE * jnp.arange(n) + (CMP_LEN - 1)
    dist = (t_pos[:, None] - end[None, :]).astype(F32)
    vis = dist >= 0
    s = jnp.einsum('btgrd,bngd->bgrtn', q, ck).astype(F32)
    s = jnp.where(vis, s - slopes[:, :, None, None] * dist, NEG_INF)
    p = jnp.where(vis, jax.nn.softmax(s, axis=-1), 0.0)
    o = jnp.einsum('bgrtn,bngd->btgrd', p.astype(cv.dtype), cv)
    return o, p


def select_blocks(p, t_pos, n_slc):
    n_cmp = p.shape[-1]
    ci = jnp.arange(n_cmp)[:, None]
    sj = jnp.arange(n_slc)[None, :]
    ov = ((CMP_STRIDE * ci < SEL_BLOCK * (sj + 1)) & (CMP_STRIDE * ci + CMP_LEN > SEL_BLOCK * sj)).astype(F32)
    imp = jnp.einsum('bgrtn,nj->bgtj', p, ov)
    cur = (t_pos // SEL_BLOCK)[:, None]
    j = jnp.arange(n_slc)[None, :]
    vis = j <= cur
    forced = (j == 0) | (j == cur) | (j == cur - 1)
    score = jnp.where(forced, FORCE_SCORE, jnp.where(vis, imp, -FORCE_SCORE))
    _, idx = lax.top_k(score, min(N_SEL, n_slc))
    valid = jnp.take_along_axis(jnp.broadcast_to(vis, score.shape), idx, axis=-1)
    return idx, valid


def nsa_global(q, t_pos, kv_cmp_all, kv_sel_all, w_cmp, pe_cmp, slopes):
    B, T_all = kv_cmp_all.shape[:2]
    Tp = -(-T_all // SEL_BLOCK) * SEL_BLOCK
    ckv = compress_kv(pad_time(kv_cmp_all, Tp), w_cmp, pe_cmp)
    o_cmp, p = cmp_branch(q, t_pos, ckv, slopes)
    n_slc = Tp // SEL_BLOCK
    idx, valid = select_blocks(p, t_pos, n_slc)
    blk = pad_time(kv_sel_all, Tp).reshape(B, n_slc, SEL_BLOCK, 2, NSA_KV_HEADS, NSA_HEAD_DIM)
    blk = blk.transpose(3, 0, 4, 1, 2, 5)
    return o_cmp, idx, valid, blk[0], blk[1]


def sel_branch(q_c, t_c, idx_c, valid_c, kb, vb, slopes):
    B, QC = q_c.shape[:2]
    k = idx_c.shape[-1]
    bi = jnp.arange(B)[:, None, None, None]
    gi = jnp.arange(NSA_KV_HEADS)[None, :, None, None]
    kg = kb[bi, gi, idx_c].reshape(B, NSA_KV_HEADS, QC, k * SEL_BLOCK, NSA_HEAD_DIM)
    vg = vb[bi, gi, idx_c].reshape(B, NSA_KV_HEADS, QC, k * SEL_BLOCK, NSA_HEAD_DIM)
    s_pos = idx_c[..., None] * SEL_BLOCK + jnp.arange(SEL_BLOCK)
    dist = t_c[None, None, :, None, None] - s_pos
    mask = ((dist >= 0) & valid_c[..., None]).reshape(B, NSA_KV_HEADS, QC, k * SEL_BLOCK)
    dist = dist.astype(F32).reshape(B, NSA_KV_HEADS, QC, k * SEL_BLOCK)
    s = jnp.einsum('bqgrd,bgqsd->bgrqs', q_c, kg).astype(F32)
    s = jnp.where(mask[:, :, None], s - slopes[None, :, :, None, None] * dist[:, :, None], NEG_INF)
    p = jax.nn.softmax(s, axis=-1)
    return jnp.einsum('bgrqs,bgqsd->bqgrd', p.astype(vg.dtype), vg)


def win_branch(q_c, t_c, kw, vw, s_pos, slopes):
    dist = t_c[:, None] - s_pos[None, :]
    mask = (dist >= 0) & (dist <= WINDOW) & (s_pos >= 0)[None, :]
    s = jnp.einsum('bqgrd,bsgd->bgrqs', q_c, kw).astype(F32)
    s = jnp.where(mask, s - slopes[:, :, None, None] * dist.astype(F32), NEG_INF)
    p = jax.nn.softmax(s, axis=-1)
    return jnp.einsum('bgrqs,bsgd->bqgrd', p.astype(vw.dtype), vw)


def nsa_combine(g, o_cmp, o_sel, o_win):
    B, T = g.shape[:2]
    o = g[:, :, 0, :, :, None] * o_cmp + g[:, :, 1, :, :, None] * o_sel + g[:, :, 2, :, :, None] * o_win
    return o.reshape(B * T, NSA_QD).astype(BF16)


def nsa_prompt_mix(z, b, t, gate_bias, w_cmp, pe_cmp):
    slopes = alibi_slopes()
    q, kv_cmp, kv_sel, kv_win, g = nsa_split(z, gate_bias, b, t)
    t_pos = jnp.arange(t)
    o_cmp, idx, valid, kb, vb = nsa_global(q, t_pos, kv_cmp, kv_sel, w_cmp, pe_cmp, slopes)
    kw_pad = jnp.pad(kv_win, ((0, 0), (WINDOW, 0), (0, 0), (0, 0), (0, 0)))
    qc = NSA_QBLOCK

    def block(c):
        q0 = c * qc
        q_c = lax.dynamic_slice_in_dim(q, q0, qc, axis=1)
        t_c = q0 + jnp.arange(qc)
        idx_c = lax.dynamic_slice_in_dim(idx, q0, qc, axis=2)
        valid_c = lax.dynamic_slice_in_dim(valid, q0, qc, axis=2)
        kw_c = lax.dynamic_slice_in_dim(kw_pad, q0, WINDOW + qc, axis=1)
        s_pos = q0 - WINDOW + jnp.arange(WINDOW + qc)
        o_s = sel_branch(q_c, t_c, idx_c, valid_c, kb, vb, slopes)
        o_w = win_branch(q_c, t_c, kw_c[:, :, 0], kw_c[:, :, 1], s_pos, slopes)
        return o_s, o_w

    o_sel, o_win = lax.map(block, jnp.arange(t // qc))
    o_sel = jnp.moveaxis(o_sel, 0, 1).reshape(q.shape)
    o_win = jnp.moveaxis(o_win, 0, 1).reshape(q.shape)
    return nsa_combine(g, o_cmp, o_sel, o_win), kv_cmp, kv_sel, kv_win[:, t - min(WINDOW, t):]


def nsa_sample_mix(z, b, t, past_cmp, past_sel, win_buf, gate_bias, w_cmp, pe_cmp):
    slopes = alibi_slopes()
    q, kv_cmp, kv_sel, kv_win, g = nsa_split(z, gate_bias, b, t)
    t_pos = PAST_LEN + jnp.arange(t)
    kv_cmp_all = jnp.concatenate([past_cmp, kv_cmp], axis=1)
    kv_sel_all = jnp.concatenate([past_sel, kv_sel], axis=1)
    o_cmp, idx, valid, kb, vb = nsa_global(q, t_pos, kv_cmp_all, kv_sel_all, w_cmp, pe_cmp, slopes)
    wb = win_buf.shape[1]
    kw_all = jnp.concatenate([win_buf, kv_win], axis=1)
    s_pos = PAST_LEN - wb + jnp.arange(wb + t)
    o_sel = sel_branch(q, t_pos, idx, valid, kb, vb, slopes)
    o_win = win_branch(q, t_pos, kw_all[:, :, 0], kw_all[:, :, 1], s_pos, slopes)
    return nsa_combine(g, o_cmp, o_sel, o_win), kv_cmp, kv_sel, kw_all[:, t:]


def gla_recurrence(q, k, v, log_a, s0):
    B, T, H, DK = q.shape
    DV = v.shape[-1]
    C = min(GLA_CHUNK, T)
    N = T // C
    qf = (q * (DK ** -0.5)).reshape(B, N, C, H, DK)
    kf = k.reshape(B, N, C, H, DK)
    vf = v.reshape(B, N, C, H, DV)
    b = jnp.cumsum(log_a.reshape(B, N, C, H, DK), axis=2)
    b_last = b[:, :, -1]
    qe = qf * jnp.exp(b)
    ke = kf * jnp.exp(-b)
    kd = kf * jnp.exp(b_last[:, :, None] - b)
    causal = jnp.tril(jnp.ones((C, C), dtype=bool))
    att = jnp.where(causal, jnp.einsum('bnthd,bnshd->bnhts', qe, ke), 0.0)
    o_intra = jnp.einsum('bnhts,bnshe->bnthe', att, vf)

    def step(s, xs_n):
        qe_n, kd_n, v_n, dec_n = xs_n
        o_n = jnp.einsum('bthd,bhde->bthe', qe_n, s)
        s = dec_n[..., None] * s + jnp.einsum('bthd,bthe->bhde', kd_n, v_n)
        return s, o_n

    xs = (jnp.moveaxis(qe, 1, 0), jnp.moveaxis(kd, 1, 0), jnp.moveaxis(vf, 1, 0),
          jnp.moveaxis(jnp.exp(b_last), 1, 0))
    s_fin, o_inter = lax.scan(step, s0, xs)
    o = (o_intra + jnp.moveaxis(o_inter, 0, 1)).reshape(B, T, H, DV)
    return o, s_fin


def gla_mix(z, b, t, s0, w_alpha, b_alpha, norm_g):
    qk = GLA_HEADS * GLA_DK
    vd = GLA_HEADS * GLA_DV
    z = z.reshape(b, t, -1)
    q = z[..., :qk].reshape(b, t, GLA_HEADS, GLA_DK)
    k = z[..., qk:2 * qk].reshape(b, t, GLA_HEADS, GLA_DK)
    v = z[..., 2 * qk:2 * qk + vd].reshape(b, t, GLA_HEADS, GLA_DV)
    r = z[..., 2 * qk + vd:2 * qk + 2 * vd]
    a = z[..., 2 * qk + 2 * vd:2 * qk + 2 * vd + GLA_RANK]
    log_a = jax.nn.log_sigmoid(a @ w_alpha + b_alpha) / GLA_TAU
    o, s_new = gla_recurrence(q, k, v, log_a.reshape(b, t, GLA_HEADS, GLA_DK), s0)
    o = _norm_rows(o, norm_g).reshape(b, t, vd) * jax.nn.silu(r)
    return o.reshape(b * t, vd).astype(BF16), s_new


def _pad_cols(w, mult):
    n = w.shape[-1]
    return jnp.pad(w, ((0, 0), (0, -(-n // mult) * mult - n)))


def kernel(x_prompt, x_sample, p_prompt, p_sample, cache_cmp_kv, cache_sel_kv, cache_win_kv, state_gla, page_table, ffn1_norm, ffn1_w_gu, ffn1_w_down, mix_norm, nsa_w_in, nsa_gate_bias, nsa_w_cmp, nsa_pe_cmp, nsa_w_out, gla_w_in, gla_w_alpha, gla_b_alpha, gla_norm, gla_w_out, ffn2_norm, ffn2_w_gu, ffn2_w_down, ple_norm, ple_w_gate, ple_w_proj, final_norm):
    bp, tp, d = x_prompt.shape
    bs, ts, _ = x_sample.shape
    db, n_pages = page_table.shape
    mp, ms = bp * tp, bs * ts
    xp = x_prompt.reshape(mp, d)
    xs = x_sample.reshape(ms, d)
    tm_p = _row_tile(mp, 512)
    tm_p2 = _row_tile(mp, 512)

    cmp_p, sel_p, win_p, gla_p = [], [], [], []
    cmp_s, sel_s, win_s, gla_s = [], [], [], []
    for i in range(DEPTH):
        w_gu1 = ffn1_w_gu[i].astype(BF16)
        w_d1 = ffn1_w_down[i].astype(BF16)
        xp = ffn(xp, ffn1_norm[i], w_gu1, w_d1, tm_p)
        xs = ffn(xs, ffn1_norm[i], w_gu1, w_d1, ms)
        j = i // 2
        if i % 2 == 0:
            w_in = _pad_cols(nsa_w_in[j], 512).astype(BF16)
            w_out = nsa_w_out[j].astype(BF16)
            zp = norm_matmul(xp, mix_norm[i], w_in, tm_p2)
            zs = norm_matmul(xs, mix_norm[i], w_in, ms)
            op, kc, kl, kw = nsa_prompt_mix(zp, bp, tp, nsa_gate_bias[j], nsa_w_cmp[j], nsa_pe_cmp[j])
            past_cmp = cache_cmp_kv[j][page_table].reshape(db, n_pages * PAGE_SIZE, 2, NSA_KV_HEADS, NSA_HEAD_DIM)
            past_sel = cache_sel_kv[j][page_table].reshape(db, n_pages * PAGE_SIZE, 2, NSA_KV_HEADS, NSA_HEAD_DIM)
            os_, kc2, kl2, kw2 = nsa_sample_mix(zs, bs, ts, past_cmp, past_sel, cache_win_kv[j],
                                                nsa_gate_bias[j], nsa_w_cmp[j], nsa_pe_cmp[j])
            cmp_p.append(kc)
            sel_p.append(kl)
            win_p.append(kw)
            cmp_s.append(kc2)
            sel_s.append(kl2)
            win_s.append(kw2)
        else:
            w_in = _pad_cols(gla_w_in[j], 512).astype(BF16)
            w_out = gla_w_out[j].astype(BF16)
            zp = norm_matmul(xp, mix_norm[i], w_in, tm_p2)
            zs = norm_matmul(xs, mix_norm[i], w_in, ms)
            s0 = jnp.zeros((bp, GLA_HEADS, GLA_DK, GLA_DV), F32)
            op, sp = gla_mix(zp, bp, tp, s0, gla_w_alpha[j], gla_b_alpha[j], gla_norm[j])
            os_, ss = gla_mix(zs, bs, ts, state_gla[j], gla_w_alpha[j], gla_b_alpha[j], gla_norm[j])
            gla_p.append(sp)
            gla_s.append(ss)
        xp = matmul_residual(op, w_out, xp, tm_p)
        xs = matmul_residual(os_, w_out, xs, ms)
        w_gu2 = ffn2_w_gu[i].astype(BF16)
        w_d2 = ffn2_w_down[i].astype(BF16)
        xp = ffn(xp, ffn2_norm[i], w_gu2, w_d2, tm_p)
        xs = ffn(xs, ffn2_norm[i], w_gu2, w_d2, ms)
        w_pg = ple_w_gate[i].astype(BF16)
        w_pp = ple_w_proj[i].astype(BF16)
        xp = ple(xp, p_prompt[i].reshape(mp, -1), ple_norm[i], w_pg, w_pp, tm_p2)
        xs = ple(xs, p_sample[i].reshape(ms, -1), ple_norm[i], w_pg, w_pp, ms)
    y_prompt = rmsnorm(xp, final_norm, tm_p2).reshape(bp, tp, d)
    y_sample = rmsnorm(xs, final_norm, ms).reshape(bs, ts, d)
    return (y_prompt, y_sample, jnp.stack(cmp_p), jnp.stack(sel_p), jnp.stack(win_p), jnp.stack(gla_p),
            jnp.stack(cmp_s), jnp.stack(sel_s), jnp.stack(win_s), jnp.stack(gla_s))
```

```python
import functools
import math

import jax
import jax.numpy as jnp
from jax import lax
from jax.experimental import pallas as pl
from jax.experimental.pallas import tpu as pltpu

F32 = jnp.float32
BF16 = jnp.bfloat16

D_MODEL = 4096
DEPTH = 2
PAST_LEN = 16384
PAGE_SIZE = 128
NSA_HEADS = 32
NSA_HEAD_DIM = 128
NSA_KV_HEADS = 4
NSA_Q_PER_KV = 8
CMP_STRIDE = 16
CMP_LEN = 32
SEL_BLOCK = 64
N_SEL = 16
WINDOW = 512
NSA_QBLOCK = 32
NSA_QD = NSA_HEADS * NSA_HEAD_DIM
NSA_KVD = NSA_KV_HEADS * NSA_HEAD_DIM
GLA_HEADS = 8
GLA_DK = 256
GLA_DV = 512
GLA_RANK = 16
GLA_TAU = 16.0
GLA_CHUNK = 32
D_FF = 11008
EPS = 1e-6
NEG_INF = -1e30
FORCE_SCORE = 1e30

V7X_VMEM_LIMIT_BYTES = 56 * 1024 * 1024
LANE = 128


def _params(*sem):
    return pltpu.CompilerParams(dimension_semantics=sem, vmem_limit_bytes=V7X_VMEM_LIMIT_BYTES)


def _norm_rows(x, g):
    ms = jnp.mean(x * x, axis=-1, keepdims=True)
    return (x * lax.rsqrt(ms + EPS)) * g


def _row_tile(m, want):
    return want if m % want == 0 else m


def _ffn_kernel(x_ref, g_ref, wa_ref, wu_ref, wd_ref, o_ref, h_ref):
    @pl.when(pl.program_id(1) == 0)
    def _():
        x = x_ref[...]
        h_ref[...] = _norm_rows(x, g_ref[...]).astype(BF16)
        o_ref[...] = x

    h = h_ref[...]
    a = jnp.dot(h, wa_ref[...], preferred_element_type=F32)
    u = jnp.dot(h, wu_ref[...], preferred_element_type=F32)
    act = (0.5 * (a * jax.nn.sigmoid(a)) * u).astype(BF16)
    o_ref[...] += jnp.dot(act, wd_ref[...], preferred_element_type=F32)


def ffn(x, g, w_gu, w_down, tm, tf=256):
    m, d = x.shape
    nf = D_FF // tf
    return pl.pallas_call(
        _ffn_kernel,
        grid=(m // tm, nf),
        in_specs=[
            pl.BlockSpec((tm, d), lambda i, j: (i, 0), pipeline_mode=pl.Buffered(1)),
            pl.BlockSpec((1, d), lambda i, j: (0, 0)),
            pl.BlockSpec((d, tf), lambda i, j: (0, j)),
            pl.BlockSpec((d, tf), lambda i, j: (0, j + nf)),
            pl.BlockSpec((tf, d), lambda i, j: (j, 0)),
        ],
        out_specs=pl.BlockSpec((tm, d), lambda i, j: (i, 0)),
        out_shape=jax.ShapeDtypeStruct((m, d), F32),
        scratch_shapes=[pltpu.VMEM((tm, d), BF16)],
        compiler_params=_params("parallel", "arbitrary"),
    )(x, g.reshape(1, d), w_gu, w_gu, w_down)


def _norm_matmul_kernel(x_ref, g_ref, w_ref, o_ref, h_ref):
    @pl.when(pl.program_id(1) == 0)
    def _():
        h_ref[...] = _norm_rows(x_ref[...], g_ref[...]).astype(BF16)

    o_ref[...] = jnp.dot(h_ref[...], w_ref[...], preferred_element_type=F32)


def norm_matmul(x, g, w, tm, tn=512):
    m, d = x.shape
    n = w.shape[1]
    return pl.pallas_call(
        _norm_matmul_kernel,
        grid=(m // tm, n // tn),
        in_specs=[
            pl.BlockSpec((tm, d), lambda i, j: (i, 0)),
            pl.BlockSpec((1, d), lambda i, j: (0, 0)),
            pl.BlockSpec((d, tn), lambda i, j: (0, j)),
        ],
        out_specs=pl.BlockSpec((tm, tn), lambda i, j: (i, j)),
        out_shape=jax.ShapeDtypeStruct((m, n), F32),
        scratch_shapes=[pltpu.VMEM((tm, d), BF16)],
        compiler_params=_params("parallel", "arbitrary"),
    )(x, g.reshape(1, d), w)


def _matmul_residual_kernel(a_ref, w_ref, x_ref, o_ref):
    o_ref[...] = x_ref[...] + jnp.dot(a_ref[...], w_ref[...], preferred_element_type=F32)


def matmul_residual(a, w, x, tm, tn=512):
    m, k = a.shape
    n = w.shape[1]
    return pl.pallas_call(
        _matmul_residual_kernel,
        grid=(m // tm, n // tn),
        in_specs=[
            pl.BlockSpec((tm, k), lambda i, j: (i, 0)),
            pl.BlockSpec((k, tn), lambda i, j: (0, j)),
            pl.BlockSpec((tm, tn), lambda i, j: (i, j)),
        ],
        out_specs=pl.BlockSpec((tm, tn), lambda i, j: (i, j)),
        out_shape=jax.ShapeDtypeStruct((m, n), F32),
        compiler_params=_params("parallel", "arbitrary"),
    )(a, w, x)


def _ple_kernel(tn, x_ref, g_ref, wg_ref, p_ref, wp_ref, o_ref, h_ref):
    j = pl.program_id(1)

    @pl.when(j == 0)
    def _():
        h_ref[...] = _norm_rows(x_ref[...], g_ref[...]).astype(BF16)

    gate = jax.nn.sigmoid(jnp.dot(h_ref[...], wg_ref[...], preferred_element_type=F32))
    proj = jnp.dot(p_ref[...].astype(BF16), wp_ref[...], preferred_element_type=F32)
    col = pl.multiple_of(j * tn, tn)
    o_ref[...] = x_ref[:, pl.ds(col, tn)] + gate * proj


def ple(x, p, g, w_gate, w_proj, tm, tn=512):
    m, d = x.shape
    pd = p.shape[1]
    return pl.pallas_call(
        functools.partial(_ple_kernel, tn),
        grid=(m // tm, d // tn),
        in_specs=[
            pl.BlockSpec((tm, d), lambda i, j: (i, 0)),
            pl.BlockSpec((1, d), lambda i, j: (0, 0)),
            pl.BlockSpec((d, tn), lambda i, j: (0, j)),
            pl.BlockSpec((tm, pd), lambda i, j: (i, 0)),
            pl.BlockSpec((pd, tn), lambda i, j: (0, j)),
        ],
        out_specs=pl.BlockSpec((tm, tn), lambda i, j: (i, j)),
        out_shape=jax.ShapeDtypeStruct((m, d), F32),
        scratch_shapes=[pltpu.VMEM((tm, d), BF16)],
        compiler_params=_params("parallel", "arbitrary"),
    )(x, g.reshape(1, d), w_gate, p, w_proj)


def _rmsnorm_kernel(x_ref, g_ref, o_ref):
    o_ref[...] = _norm_rows(x_ref[...], g_ref[...])


def rmsnorm(x, g, tm):
    m, d = x.shape
    return pl.pallas_call(
        _rmsnorm_kernel,
        grid=(m // tm,),
        in_specs=[pl.BlockSpec((tm, d), lambda i: (i, 0)), pl.BlockSpec((1, d), lambda i: (0, 0))],
        out_specs=pl.BlockSpec((tm, d), lambda i: (i, 0)),
        out_shape=jax.ShapeDtypeStruct((m, d), F32),
        compiler_params=_params("parallel"),
    )(x, g.reshape(1, d))


Z_GATE_COL = (NSA_QD + 6 * NSA_KVD) // LANE


def _compress_kernel(nc, *refs):
    x_refs, (w2_ref, wf_ref, pe_ref, o_ref) = refs[:8], refs[8:]
    n_idx = lax.broadcasted_iota(jnp.int32, (NSA_KV_HEADS * nc, NSA_HEAD_DIM), 0) % nc
    for c in range(2):
        acc = jnp.zeros((NSA_KV_HEADS * nc, 2 * NSA_HEAD_DIM), F32)
        for l in range(CMP_STRIDE):
            a = jnp.concatenate([x_refs[c * NSA_KV_HEADS + g][pl.ds(l, nc, stride=CMP_STRIDE), :]
                                 for g in range(NSA_KV_HEADS)], axis=0)
            acc += jnp.dot(a.astype(BF16), w2_ref[c, l], preferred_element_type=F32)
        first = acc[:, :NSA_HEAD_DIM]
        second = pltpu.roll(acc[:, NSA_HEAD_DIM:], NSA_KV_HEADS * nc - 1, 0)
        second = jnp.where(n_idx == nc - 1, 0.0, second)
        bias = jnp.dot(pe_ref[c], wf_ref[c], preferred_element_type=F32)[0:1]
        o_ref[c] = (first + second + bias).astype(BF16)


def compress_prompt(z, b, t, w_cmp, pe_cmp):
    nc = t // CMP_STRIDE
    hd = NSA_HEAD_DIM
    w2 = w_cmp.reshape(2, 2, CMP_STRIDE, hd, hd).transpose(0, 2, 3, 1, 4).reshape(2, CMP_STRIDE, hd, 2 * hd)
    wf = w_cmp.reshape(2, CMP_LEN * hd, hd)
    pe = jnp.broadcast_to(pe_cmp.reshape(2, 1, CMP_LEN * hd), (2, 8, CMP_LEN * hd))
    x_spec = lambda cg: pl.BlockSpec((t, hd), lambda i: (i, NSA_QD // hd + cg))
    return pl.pallas_call(
        functools.partial(_compress_kernel, nc),
        grid=(b,),
        in_specs=[x_spec(cg) for cg in range(2 * NSA_KV_HEADS)] + [
            pl.BlockSpec((2, CMP_STRIDE, hd, 2 * hd), lambda i: (0, 0, 0, 0)),
            pl.BlockSpec((2, CMP_LEN * hd, hd), lambda i: (0, 0, 0)),
            pl.BlockSpec((2, 8, CMP_LEN * hd), lambda i: (0, 0, 0)),
        ],
        out_specs=pl.BlockSpec((None, 2, NSA_KV_HEADS * nc, hd), lambda i: (i, 0, 0, 0)),
        out_shape=jax.ShapeDtypeStruct((b, 2, NSA_KV_HEADS * nc, hd), BF16),
        compiler_params=_params("parallel"),
    )(*([z] * (2 * NSA_KV_HEADS)), w2.astype(BF16), wf.astype(BF16), pe.astype(BF16))


def _nsa_prompt_kernel(t, qb, kb, nc, slope_ref, zq_ref, ks_ref, vs_ref, kw_ref, vw_ref, gz_ref, gb_ref,
                       ck_ref, cv_ref, ov_ref, o_ref, m_ref, l_ref, acc_ref):
    g = pl.program_id(1)
    qi = pl.program_id(2)
    t0 = qi * qb
    r8 = NSA_Q_PER_KV
    hd = NSA_HEAD_DIM
    n_slc = t // SEL_BLOCK
    n_cmp = nc - 1

    zq = zq_ref[...] * (hd ** -0.5)
    q2 = jnp.concatenate([zq[:, r * hd:(r + 1) * hd] for r in range(r8)], axis=0).astype(BF16)

    r_idx = lax.broadcasted_iota(jnp.int32, (r8, 1, 1), 0)
    slope3 = jnp.zeros((r8, 1, 1), F32)
    for r in range(r8):
        slope3 = jnp.where(r_idx == r, slope_ref[g * r8 + r], slope3)

    tq = lax.broadcasted_iota(jnp.int32, (qb, 1), 0) + t0

    n_i = lax.broadcasted_iota(jnp.int32, (qb, nc), 1)
    dist_c = tq - (CMP_STRIDE * n_i + (CMP_LEN - 1))
    vis_c = (dist_c >= 0) & (n_i < n_cmp)
    s = lax.dot_general(q2, ck_ref[...], (((1,), (1,)), ((), ())), preferred_element_type=F32)
    s = s.reshape(r8, qb, nc)
    s = jnp.where(vis_c[None], s - slope3 * dist_c.astype(F32)[None], NEG_INF)
    e = jnp.exp(s - jnp.max(s, axis=-1, keepdims=True))
    p = jnp.where(vis_c[None], e / jnp.sum(e, axis=-1, keepdims=True), 0.0)
    o_cmp = jnp.dot(p.reshape(r8 * qb, nc).astype(BF16), cv_ref[...], preferred_element_type=F32)
    o_cmp = o_cmp.reshape(r8, qb, hd)

    psum = jnp.sum(p, axis=0)
    p_hi = psum.astype(BF16)
    p_mid = (psum - p_hi.astype(F32)).astype(BF16)
    p_lo = (psum - p_hi.astype(F32) - p_mid.astype(F32)).astype(BF16)
    ov = ov_ref[...]
    imp = (jnp.dot(p_hi, ov, preferred_element_type=F32) + jnp.dot(p_mid, ov, preferred_element_type=F32)
           + jnp.dot(p_lo, ov, preferred_element_type=F32))
    j_i = lax.broadcasted_iota(jnp.int32, (qb, LANE), 1)
    cur = tq // SEL_BLOCK
    vis_j = j_i <= cur
    forced = (j_i == 0) | (j_i == cur) | (j_i == cur - 1)
    score = jnp.where(forced, FORCE_SCORE, jnp.where(vis_j, imp, -FORCE_SCORE))
    score = jnp.where(j_i < n_slc, score, -jnp.inf)
    rank = jnp.zeros((qb, LANE), jnp.int32)
    for jp in range(n_slc):
        col = score[:, jp:jp + 1]
        ahead = (col > score) | ((col == score) & (jp < j_i))
        rank = rank + ahead.astype(jnp.int32)
    selected = ((rank < min(N_SEL, n_slc)) & vis_j).astype(BF16)

    d0 = (lax.broadcasted_iota(jnp.int32, (qb, kb), 0) - lax.broadcasted_iota(jnp.int32, (qb, kb), 1))

    def flash_init():
        m_ref[...] = jnp.full(m_ref.shape, NEG_INF, F32)
        l_ref[...] = jnp.zeros(l_ref.shape, F32)
        acc_ref[...] = jnp.zeros(acc_ref.shape, F32)

    def flash_step(k_ref, v_ref, k0, mask, dist):
        k = k_ref[pl.ds(k0, kb), :].astype(BF16)
        v = v_ref[pl.ds(k0, kb), :].astype(BF16)
        s = lax.dot_general(q2, k, (((1,), (1,)), ((), ())), preferred_element_type=F32).reshape(r8, qb, kb)
        s = jnp.where(mask[None], s - slope3 * dist[None], NEG_INF)
        m_old = m_ref[...]
        m_new = jnp.maximum(m_old, jnp.max(s, axis=-1, keepdims=True))
        alpha = jnp.exp(m_old - m_new)
        e = jnp.exp(s - m_new)
        l_ref[...] = alpha * l_ref[...] + jnp.sum(e, axis=-1, keepdims=True)
        pv = jnp.dot(e.reshape(r8 * qb, kb).astype(BF16), v, preferred_element_type=F32)
        acc_ref[...] = alpha * acc_ref[...] + pv.reshape(r8, qb, hd)
        m_ref[...] = m_new

    def flash_result():
        return acc_ref[...] / l_ref[...]

    flash_init()

    def sel_body(i, carry):
        k0 = pl.multiple_of(i * kb, kb)
        dist = d0 + (t0 - k0)
        blk_of_key = (lax.broadcasted_iota(jnp.int32, (LANE, kb), 1) + k0) // SEL_BLOCK
        expand = (blk_of_key == lax.broadcasted_iota(jnp.int32, (LANE, kb), 0)).astype(BF16)
        chosen = jnp.dot(selected, expand, preferred_element_type=F32)
        flash_step(ks_ref, vs_ref, k0, (chosen > 0.5) & (dist >= 0), dist.astype(F32))
        return carry

    lax.fori_loop(0, (t0 + qb - 1) // kb + 1, sel_body, 0)
    o_sel = flash_result()

    flash_init()

    def win_body(i, carry):
        k0 = pl.multiple_of(i * kb, kb)
        dist = d0 + (t0 - k0)
        flash_step(kw_ref, vw_ref, k0, (dist >= 0) & (dist <= WINDOW), dist.astype(F32))
        return carry

    lax.fori_loop(jnp.maximum(t0 - WINDOW, 0) // kb, (t0 + qb - 1) // kb + 1, win_body, 0)
    o_win = flash_result()

    gates = jax.nn.sigmoid(gz_ref[...] + gb_ref[...])
    lane = lax.broadcasted_iota(jnp.int32, (qb, LANE), 1)

    def gate3(branch):
        cols = [jnp.sum(jnp.where(lane == branch * NSA_HEADS + g * r8 + r, gates, 0.0), axis=-1, keepdims=True)
                for r in range(r8)]
        return jnp.stack(cols, axis=0)

    o = gate3(0) * o_cmp + gate3(1) * o_sel + gate3(2) * o_win
    for r in range(r8):
        o_ref[:, r * hd:(r + 1) * hd] = o[r].astype(BF16)


def nsa_prompt_attention(z, ckv, gate_bias, b, t, qb=128, kb=256):
    nc = t // CMP_STRIDE
    nq = t // qb
    hd = NSA_HEAD_DIM
    n_slc = t // SEL_BLOCK
    slopes = jnp.exp2(-8.0 * (jnp.arange(NSA_HEADS, dtype=F32) + 1.0) / NSA_HEADS)
    ci = jnp.arange(nc)[:, None]
    sj = jnp.arange(LANE)[None, :]
    ov = ((CMP_STRIDE * ci < SEL_BLOCK * (sj + 1)) & (CMP_STRIDE * ci + CMP_LEN > SEL_BLOCK * sj)
          & (sj < n_slc)).astype(BF16)
    gb = jnp.pad(gate_bias, (0, LANE - gate_bias.shape[0])).reshape(1, LANE)
    kv_spec = lambda off: pl.BlockSpec((t, hd), lambda bi, g, qi: (bi, off + g))
    kv_col = (NSA_QD + 2 * NSA_KVD) // hd
    return pl.pallas_call(
        functools.partial(_nsa_prompt_kernel, t, qb, kb, nc),
        grid=(b, NSA_KV_HEADS, nq),
        in_specs=[
            pl.BlockSpec(memory_space=pltpu.SMEM),
            pl.BlockSpec((qb, NSA_Q_PER_KV * hd), lambda bi, g, qi: (bi * nq + qi, g)),
            kv_spec(kv_col), kv_spec(kv_col + 4), kv_spec(kv_col + 8), kv_spec(kv_col + 12),
            pl.BlockSpec((qb, LANE), lambda bi, g, qi: (bi * nq + qi, Z_GATE_COL)),
            pl.BlockSpec((1, LANE), lambda bi, g, qi: (0, 0)),
            pl.BlockSpec((None, None, nc, hd), lambda bi, g, qi: (bi, 0, g, 0)),
            pl.BlockSpec((None, None, nc, hd), lambda bi, g, qi: (bi, 1, g, 0)),
            pl.BlockSpec((nc, LANE), lambda bi, g, qi: (0, 0)),
        ],
        out_specs=pl.BlockSpec((qb, NSA_Q_PER_KV * hd), lambda bi, g, qi: (bi * nq + qi, g)),
        out_shape=jax.ShapeDtypeStruct((b * t, NSA_QD), BF16),
        scratch_shapes=[pltpu.VMEM((NSA_Q_PER_KV, qb, 1), F32), pltpu.VMEM((NSA_Q_PER_KV, qb, 1), F32),
                        pltpu.VMEM((NSA_Q_PER_KV, qb, hd), F32)],
        compiler_params=_params("parallel", "parallel", "arbitrary"),
    )(slopes, z, z, z, z, z, z, gb, ckv, ckv, ov)


def alibi_slopes():
    h = jnp.arange(NSA_HEADS, dtype=F32)
    return jnp.exp2(-8.0 * (h + 1.0) / NSA_HEADS).reshape(NSA_KV_HEADS, NSA_Q_PER_KV)


def pad_time(a, total):
    pad = [(0, 0)] * a.ndim
    pad[1] = (0, total - a.shape[1])
    return jnp.pad(a, pad)


def nsa_split(z, gate_bias, b, t):
    q = z[:, :NSA_QD].reshape(b, t, NSA_KV_HEADS, NSA_Q_PER_KV, NSA_HEAD_DIM) * (NSA_HEAD_DIM ** -0.5)
    kv = z[:, NSA_QD:NSA_QD + 6 * NSA_KVD].reshape(b, t, 3, 2, NSA_KV_HEADS, NSA_HEAD_DIM)
    g = jax.nn.sigmoid(z[:, NSA_QD + 6 * NSA_KVD:NSA_QD + 6 * NSA_KVD + 3 * NSA_HEADS] + gate_bias)
    g = g.reshape(b, t, 3, NSA_KV_HEADS, NSA_Q_PER_KV)
    return q, kv[:, :, 0], kv[:, :, 1], kv[:, :, 2], g


def compress_kv(kv_all, w_cmp, pe_cmp):
    B, Tp = kv_all.shape[:2]
    ch = kv_all.reshape(B, Tp // CMP_STRIDE, CMP_STRIDE, 2, NSA_KV_HEADS, NSA_HEAD_DIM)
    first = jnp.einsum('bnlcgd,clde->bncge', ch, w_cmp[:, :CMP_STRIDE])
    second = jnp.einsum('bnlcgd,clde->bncge', ch, w_cmp[:, CMP_STRIDE:])
    bias = jnp.einsum('cld,clde->ce', pe_cmp, w_cmp)
    return first[:, :-1] + second[:, 1:] + bias[:, None, :]


def cmp_branch(q, t_pos, ckv, slopes):
    n = ckv.shape[1]
    ck, cv = ckv[:, :, 0], ckv[:, :, 1]
    end = CMP_STRIDE * jnp.arange(n) + (CMP_LEN - 1)
    dist = (t_pos[:, None] - end[None, :]).astype(F32)
    vis = dist >= 0
    s = jnp.einsum('btgrd,bngd->bgrtn', q, ck).astype(F32)
    s = jnp.where(vis, s - slopes[:, :, None, None] * dist, NEG_INF)
    p = jnp.where(vis, jax.nn.softmax(s, axis=-1), 0.0)
    o = jnp.einsum('bgrtn,bngd->btgrd', p.astype(cv.dtype), cv)
    return o, p


def select_blocks(p, t_pos, n_slc):
    n_cmp = p.shape[-1]
    ci = jnp.arange(n_cmp)[:, None]
    sj = jnp.arange(n_slc)[None, :]
    ov = ((CMP_STRIDE * ci < SEL_BLOCK * (sj + 1)) & (CMP_STRIDE * ci + CMP_LEN > SEL_BLOCK * sj)).astype(F32)
    imp = jnp.einsum('bgrtn,nj->bgtj', p, ov)
    cur = (t_pos // SEL_BLOCK)[:, None]
    j = jnp.arange(n_slc)[None, :]
    vis = j <= cur
    forced = (j == 0) | (j == cur) | (j == cur - 1)
    score = jnp.where(forced, FORCE_SCORE, jnp.where(vis, imp, -FORCE_SCORE))
    _, idx = lax.top_k(score, min(N_SEL, n_slc))
    valid = jnp.take_along_axis(jnp.broadcast_to(vis, score.shape), idx, axis=-1)
    return idx, valid


def nsa_global(q, t_pos, kv_cmp_all, kv_sel_all, w_cmp, pe_cmp, slopes):
    B, T_all = kv_cmp_all.shape[:2]
    Tp = -(-T_all // SEL_BLOCK) * SEL_BLOCK
    ckv = compress_kv(pad_time(kv_cmp_all, Tp), w_cmp, pe_cmp)
    o_cmp, p = cmp_branch(q, t_pos, ckv, slopes)
    n_slc = Tp // SEL_BLOCK
    idx, valid = select_blocks(p, t_pos, n_slc)
    blk = pad_time(kv_sel_all, Tp).reshape(B, n_slc, SEL_BLOCK, 2, NSA_KV_HEADS, NSA_HEAD_DIM)
    blk = blk.transpose(3, 0, 4, 1, 2, 5)
    return o_cmp, idx, valid, blk[0], blk[1]


def sel_branch(q_c, t_c, idx_c, valid_c, kb, vb, slopes):
    B, QC = q_c.shape[:2]
    k = idx_c.shape[-1]
    bi = jnp.arange(B)[:, None, None, None]
    gi = jnp.arange(NSA_KV_HEADS)[None, :, None, None]
    kg = kb[bi, gi, idx_c].reshape(B, NSA_KV_HEADS, QC, k * SEL_BLOCK, NSA_HEAD_DIM)
    vg = vb[bi, gi, idx_c].reshape(B, NSA_KV_HEADS, QC, k * SEL_BLOCK, NSA_HEAD_DIM)
    s_pos = idx_c[..., None] * SEL_BLOCK + jnp.arange(SEL_BLOCK)
    dist = t_c[None, None, :, None, None] - s_pos
    mask = ((dist >= 0) & valid_c[..., None]).reshape(B, NSA_KV_HEADS, QC, k * SEL_BLOCK)
    dist = dist.astype(F32).reshape(B, NSA_KV_HEADS, QC, k * SEL_BLOCK)
    s = jnp.einsum('bqgrd,bgqsd->bgrqs', q_c, kg).astype(F32)
    s = jnp.where(mask[:, :, None], s - slopes[None, :, :, None, None] * dist[:, :, None], NEG_INF)
    p = jax.nn.softmax(s, axis=-1)
    return jnp.einsum('bgrqs,bgqsd->bqgrd', p.astype(vg.dtype), vg)


def win_branch(q_c, t_c, kw, vw, s_pos, slopes):
    dist = t_c[:, None] - s_pos[None, :]
    mask = (dist >= 0) & (dist <= WINDOW) & (s_pos >= 0)[None, :]
    s = jnp.einsum('bqgrd,bsgd->bgrqs', q_c, kw).astype(F32)
    s = jnp.where(mask, s - slopes[:, :, None, None] * dist.astype(F32), NEG_INF)
    p = jax.nn.softmax(s, axis=-1)
    return jnp.einsum('bgrqs,bsgd->bqgrd', p.astype(vw.dtype), vw)


def nsa_combine(g, o_cmp, o_sel, o_win):
    B, T = g.shape[:2]
    o = g[:, :, 0, :, :, None] * o_cmp + g[:, :, 1, :, :, None] * o_sel + g[:, :, 2, :, :, None] * o_win
    return o.reshape(B * T, NSA_QD).astype(BF16)


def nsa_prompt_mix(z, b, t, gate_bias, w_cmp, pe_cmp):
    ckv = compress_prompt(z, b, t, w_cmp, pe_cmp)
    o = nsa_prompt_attention(z, ckv, gate_bias, b, t)
    kv = z[:, NSA_QD:NSA_QD + 6 * NSA_KVD].reshape(b, t, 3, 2, NSA_KV_HEADS, NSA_HEAD_DIM)
    return o, kv[:, :, 0], kv[:, :, 1], kv[:, t - min(WINDOW, t):, 2]


def nsa_sample_mix(z, b, t, past_cmp, past_sel, win_buf, gate_bias, w_cmp, pe_cmp):
    slopes = alibi_slopes()
    q, kv_cmp, kv_sel, kv_win, g = nsa_split(z, gate_bias, b, t)
    t_pos = PAST_LEN + jnp.arange(t)
    kv_cmp_all = jnp.concatenate([past_cmp, kv_cmp], axis=1)
    kv_sel_all = jnp.concatenate([past_sel, kv_sel], axis=1)
    o_cmp, idx, valid, kb, vb = nsa_global(q, t_pos, kv_cmp_all, kv_sel_all, w_cmp, pe_cmp, slopes)
    wb = win_buf.shape[1]
    kw_all = jnp.concatenate([win_buf, kv_win], axis=1)
    s_pos = PAST_LEN - wb + jnp.arange(wb + t)
    o_sel = sel_branch(q, t_pos, idx, valid, kb, vb, slopes)
    o_win = win_branch(q, t_pos, kw_all[:, :, 0], kw_all[:, :, 1], s_pos, slopes)
    return nsa_combine(g, o_cmp, o_sel, o_win), kv_cmp, kv_sel, kw_all[:, t:]


def gla_recurrence(q, k, v, log_a, s0):
    B, T, H, DK = q.shape
    DV = v.shape[-1]
    C = min(GLA_CHUNK, T)
    N = T // C
    qf = (q * (DK ** -0.5)).reshape(B, N, C, H, DK)
    kf = k.reshape(B, N, C, H, DK)
    vf = v.reshape(B, N, C, H, DV)
    b = jnp.cumsum(log_a.reshape(B, N, C, H, DK), axis=2)
    b_last = b[:, :, -1]
    qe = qf * jnp.exp(b)
    ke = kf * jnp.exp(-b)
    kd = kf * jnp.exp(b_last[:, :, None] - b)
    causal = jnp.tril(jnp.ones((C, C), dtype=bool))
    att = jnp.where(causal, jnp.einsum('bnthd,bnshd->bnhts', qe, ke), 0.0)
    o_intra = jnp.einsum('bnhts,bnshe->bnthe', att, vf)

    def step(s, xs_n):
        qe_n, kd_n, v_n, dec_n = xs_n
        o_n = jnp.einsum('bthd,bhde->bthe', qe_n, s)
        s = dec_n[..., None] * s + jnp.einsum('bthd,bthe->bhde', kd_n, v_n)
        return s, o_n

    xs = (jnp.moveaxis(qe, 1, 0), jnp.moveaxis(kd, 1, 0), jnp.moveaxis(vf, 1, 0),
          jnp.moveaxis(jnp.exp(b_last), 1, 0))
    s_fin, o_inter = lax.scan(step, s0, xs)
    o = (o_intra + jnp.moveaxis(o_inter, 0, 1)).reshape(B, T, H, DV)
    return o, s_fin


def gla_mix(z, b, t, s0, w_alpha, b_alpha, norm_g):
    qk = GLA_HEADS * GLA_DK
    vd = GLA_HEADS * GLA_DV
    z = z.reshape(b, t, -1)
    q = z[..., :qk].reshape(b, t, GLA_HEADS, GLA_DK)
    k = z[..., qk:2 * qk].reshape(b, t, GLA_HEADS, GLA_DK)
    v = z[..., 2 * qk:2 * qk + vd].reshape(b, t, GLA_HEADS, GLA_DV)
    r = z[..., 2 * qk + vd:2 * qk + 2 * vd]
    a = z[..., 2 * qk + 2 * vd:2 * qk + 2 * vd + GLA_RANK]
    log_a = jax.nn.log_sigmoid(a @ w_alpha + b_alpha) / GLA_TAU
    o, s_new = gla_recurrence(q, k, v, log_a.reshape(b, t, GLA_HEADS, GLA_DK), s0)
    o = _norm_rows(o, norm_g).reshape(b, t, vd) * jax.nn.silu(r)
    return o.reshape(b * t, vd).astype(BF16), s_new


def _pad_cols(w, mult):
    n = w.shape[-1]
    return jnp.pad(w, ((0, 0), (0, -(-n // mult) * mult - n)))


def kernel(x_prompt, x_sample, p_prompt, p_sample, cache_cmp_kv, cache_sel_kv, cache_win_kv, state_gla, page_table, ffn1_norm, ffn1_w_gu, ffn1_w_down, mix_norm, nsa_w_in, nsa_gate_bias, nsa_w_cmp, nsa_pe_cmp, nsa_w_out, gla_w_in, gla_w_alpha, gla_b_alpha, gla_norm, gla_w_out, ffn2_norm, ffn2_w_gu, ffn2_w_down, ple_norm, ple_w_gate, ple_w_proj, final_norm):
    bp, tp, d = x_prompt.shape
    bs, ts, _ = x_sample.shape
    db, n_pages = page_table.shape
    mp, ms = bp * tp, bs * ts
    xp = x_prompt.reshape(mp, d)
    xs = x_sample.reshape(ms, d)
    tm_p = _row_tile(mp, 512)
    tm_p2 = _row_tile(mp, 512)

    cmp_p, sel_p, win_p, gla_p = [], [], [], []
    cmp_s, sel_s, win_s, gla_s = [], [], [], []
    for i in range(DEPTH):
        w_gu1 = ffn1_w_gu[i].astype(BF16)
        w_d1 = ffn1_w_down[i].astype(BF16)
        xp = ffn(xp, ffn1_norm[i], w_gu1, w_d1, tm_p)
        xs = ffn(xs, ffn1_norm[i], w_gu1, w_d1, ms)
        j = i // 2
        if i % 2 == 0:
            w_in = _pad_cols(nsa_w_in[j], 512).astype(BF16)
            w_out = nsa_w_out[j].astype(BF16)
            zp = norm_matmul(xp, mix_norm[i], w_in, tm_p2)
            zs = norm_matmul(xs, mix_norm[i], w_in, ms)
            op, kc, kl, kw = nsa_prompt_mix(zp, bp, tp, nsa_gate_bias[j], nsa_w_cmp[j], nsa_pe_cmp[j])
            past_cmp = cache_cmp_kv[j][page_table].reshape(db, n_pages * PAGE_SIZE, 2, NSA_KV_HEADS, NSA_HEAD_DIM)
            past_sel = cache_sel_kv[j][page_table].reshape(db, n_pages * PAGE_SIZE, 2, NSA_KV_HEADS, NSA_HEAD_DIM)
            os_, kc2, kl2, kw2 = nsa_sample_mix(zs, bs, ts, past_cmp, past_sel, cache_win_kv[j],
                                                nsa_gate_bias[j], nsa_w_cmp[j], nsa_pe_cmp[j])
            cmp_p.append(kc)
            sel_p.append(kl)
            win_p.append(kw)
            cmp_s.append(kc2)
            sel_s.append(kl2)
            win_s.append(kw2)
        else:
            w_in = _pad_cols(gla_w_in[j], 512).astype(BF16)
            w_out = gla_w_out[j].astype(BF16)
            zp = norm_matmul(xp, mix_norm[i], w_in, tm_p2)
            zs = norm_matmul(xs, mix_norm[i], w_in, ms)
            s0 = jnp.zeros((bp, GLA_HEADS, GLA_DK, GLA_DV), F32)
            op, sp = gla_mix(zp, bp, tp, s0, gla_w_alpha[j], gla_b_alpha[j], gla_norm[j])
            os_, ss = gla_mix(zs, bs, ts, state_gla[j], gla_w_alpha[j], gla_b_alpha[j], gla_norm[j])
            gla_p.append(sp)
            gla_s.append(ss)
        xp = matmul_residual(op, w_out, xp, tm_p)
        xs = matmul_residual(os_, w_out, xs, ms)
        w_gu2 = ffn2_w_gu[i].astype(BF16)
        w_d2 = ffn2_w_down[i].astype(BF16)
        xp = ffn(xp, ffn2_norm[i], w_gu2, w_d2, tm_p)
        xs = ffn(xs, ffn2_norm[i], w_gu2, w_d2, ms)
        w_pg = ple_w_gate[i].astype(BF16)
        w_pp = ple_w_proj[i].astype(BF16)
        xp = ple(xp, p_prompt[i].reshape(mp, -1), ple_norm[i], w_pg, w_pp, tm_p2)
        xs = ple(xs, p_sample[i].reshape(ms, -1), ple_norm[i], w_pg, w_pp, ms)
    y_prompt = rmsnorm(xp, final_norm, tm_p2).reshape(bp, tp, d)
    y_sample = rmsnorm(xs, final_norm, ms).reshape(bs, ts, d)
    return (y_prompt, y_sample, jnp.stack(cmp_p), jnp.stack(sel_p), jnp.stack(win_p), jnp.stack(gla_p),
            jnp.stack(cmp_s), jnp.stack(sel_s), jnp.stack(win_s), jnp.stack(gla_s))
```

```python
import functools
import math

import jax
import jax.numpy as jnp
from jax import lax
from jax.experimental import pallas as pl
from jax.experimental.pallas import tpu as pltpu

F32 = jnp.float32
BF16 = jnp.bfloat16

D_MODEL = 4096
DEPTH = 2
PAST_LEN = 16384
PAGE_SIZE = 128
NSA_HEADS = 32
NSA_HEAD_DIM = 128
NSA_KV_HEADS = 4
NSA_Q_PER_KV = 8
CMP_STRIDE = 16
CMP_LEN = 32
SEL_BLOCK = 64
N_SEL = 16
WINDOW = 512
NSA_QBLOCK = 32
NSA_QD = NSA_HEADS * NSA_HEAD_DIM
NSA_KVD = NSA_KV_HEADS * NSA_HEAD_DIM
GLA_HEADS = 8
GLA_DK = 256
GLA_DV = 512
GLA_RANK = 16
GLA_TAU = 16.0
GLA_CHUNK = 32
D_FF = 11008
EPS = 1e-6
NEG_INF = -1e30
FORCE_SCORE = 1e30

V7X_VMEM_LIMIT_BYTES = 56 * 1024 * 1024
LANE = 128


def _params(*sem):
    return pltpu.CompilerParams(dimension_semantics=sem, vmem_limit_bytes=V7X_VMEM_LIMIT_BYTES)


def _norm_rows(x, g):
    ms = jnp.mean(x * x, axis=-1, keepdims=True)
    return (x * lax.rsqrt(ms + EPS)) * g


def _row_tile(m, want):
    return want if m % want == 0 else m


def _ffn_kernel(x_ref, g_ref, wa_ref, wu_ref, wd_ref, o_ref, h_ref):
    @pl.when(pl.program_id(1) == 0)
    def _():
        x = x_ref[...]
        h_ref[...] = _norm_rows(x, g_ref[...]).astype(BF16)
        o_ref[...] = x

    h = h_ref[...]
    a = jnp.dot(h, wa_ref[...], preferred_element_type=F32)
    u = jnp.dot(h, wu_ref[...], preferred_element_type=F32)
    act = (0.5 * (a * jax.nn.sigmoid(a)) * u).astype(BF16)
    o_ref[...] += jnp.dot(act, wd_ref[...], preferred_element_type=F32)


def ffn(x, g, w_gu, w_down, tm, tf=256):
    m, d = x.shape
    nf = D_FF // tf
    return pl.pallas_call(
        _ffn_kernel,
        grid=(m // tm, nf),
        in_specs=[
            pl.BlockSpec((tm, d), lambda i, j: (i, 0), pipeline_mode=pl.Buffered(1)),
            pl.BlockSpec((1, d), lambda i, j: (0, 0)),
            pl.BlockSpec((d, tf), lambda i, j: (0, j)),
            pl.BlockSpec((d, tf), lambda i, j: (0, j + nf)),
            pl.BlockSpec((tf, d), lambda i, j: (j, 0)),
        ],
        out_specs=pl.BlockSpec((tm, d), lambda i, j: (i, 0)),
        out_shape=jax.ShapeDtypeStruct((m, d), F32),
        scratch_shapes=[pltpu.VMEM((tm, d), BF16)],
        compiler_params=_params("parallel", "arbitrary"),
    )(x, g.reshape(1, d), w_gu, w_gu, w_down)


def _norm_matmul_kernel(x_ref, g_ref, w_ref, o_ref, h_ref):
    @pl.when(pl.program_id(1) == 0)
    def _():
        h_ref[...] = _norm_rows(x_ref[...], g_ref[...]).astype(BF16)

    o_ref[...] = jnp.dot(h_ref[...], w_ref[...], preferred_element_type=F32)


def norm_matmul(x, g, w, tm, tn=512):
    m, d = x.shape
    n = w.shape[1]
    return pl.pallas_call(
        _norm_matmul_kernel,
        grid=(m // tm, n // tn),
        in_specs=[
            pl.BlockSpec((tm, d), lambda i, j: (i, 0)),
            pl.BlockSpec((1, d), lambda i, j: (0, 0)),
            pl.BlockSpec((d, tn), lambda i, j: (0, j)),
        ],
        out_specs=pl.BlockSpec((tm, tn), lambda i, j: (i, j)),
        out_shape=jax.ShapeDtypeStruct((m, n), F32),
        scratch_shapes=[pltpu.VMEM((tm, d), BF16)],
        compiler_params=_params("parallel", "arbitrary"),
    )(x, g.reshape(1, d), w)


def _matmul_residual_kernel(a_ref, w_ref, x_ref, o_ref):
    o_ref[...] = x_ref[...] + jnp.dot(a_ref[...], w_ref[...], preferred_element_type=F32)


def matmul_residual(a, w, x, tm, tn=512):
    m, k = a.shape
    n = w.shape[1]
    return pl.pallas_call(
        _matmul_residual_kernel,
        grid=(m // tm, n // tn),
        in_specs=[
            pl.BlockSpec((tm, k), lambda i, j: (i, 0)),
            pl.BlockSpec((k, tn), lambda i, j: (0, j)),
            pl.BlockSpec((tm, tn), lambda i, j: (i, j)),
        ],
        out_specs=pl.BlockSpec((tm, tn), lambda i, j: (i, j)),
        out_shape=jax.ShapeDtypeStruct((m, n), F32),
        compiler_params=_params("parallel", "arbitrary"),
    )(a, w, x)


def _ple_kernel(tn, x_ref, g_ref, wg_ref, p_ref, wp_ref, o_ref, h_ref):
    j = pl.program_id(1)

    @pl.when(j == 0)
    def _():
        h_ref[...] = _norm_rows(x_ref[...], g_ref[...]).astype(BF16)

    gate = jax.nn.sigmoid(jnp.dot(h_ref[...], wg_ref[...], preferred_element_type=F32))
    proj = jnp.dot(p_ref[...].astype(BF16), wp_ref[...], preferred_element_type=F32)
    col = pl.multiple_of(j * tn, tn)
    o_ref[...] = x_ref[:, pl.ds(col, tn)] + gate * proj


def ple(x, p, g, w_gate, w_proj, tm, tn=512):
    m, d = x.shape
    pd = p.shape[1]
    return pl.pallas_call(
        functools.partial(_ple_kernel, tn),
        grid=(m // tm, d // tn),
        in_specs=[
            pl.BlockSpec((tm, d), lambda i, j: (i, 0)),
            pl.BlockSpec((1, d), lambda i, j: (0, 0)),
            pl.BlockSpec((d, tn), lambda i, j: (0, j)),
            pl.BlockSpec((tm, pd), lambda i, j: (i, 0)),
            pl.BlockSpec((pd, tn), lambda i, j: (0, j)),
        ],
        out_specs=pl.BlockSpec((tm, tn), lambda i, j: (i, j)),
        out_shape=jax.ShapeDtypeStruct((m, d), F32),
        scratch_shapes=[pltpu.VMEM((tm, d), BF16)],
        compiler_params=_params("parallel", "arbitrary"),
    )(x, g.reshape(1, d), w_gate, p, w_proj)


def _rmsnorm_kernel(x_ref, g_ref, o_ref):
    o_ref[...] = _norm_rows(x_ref[...], g_ref[...])


def rmsnorm(x, g, tm):
    m, d = x.shape
    return pl.pallas_call(
        _rmsnorm_kernel,
        grid=(m // tm,),
        in_specs=[pl.BlockSpec((tm, d), lambda i: (i, 0)), pl.BlockSpec((1, d), lambda i: (0, 0))],
        out_specs=pl.BlockSpec((tm, d), lambda i: (i, 0)),
        out_shape=jax.ShapeDtypeStruct((m, d), F32),
        compiler_params=_params("parallel"),
    )(x, g.reshape(1, d))


Z_GATE_COL = (NSA_QD + 6 * NSA_KVD) // LANE


def _compress_kernel(nc, *refs):
    x_refs, (w2_ref, wf_ref, pe_ref, o_ref) = refs[:8], refs[8:]
    n_idx = lax.broadcasted_iota(jnp.int32, (NSA_KV_HEADS * nc, NSA_HEAD_DIM), 0) % nc
    for c in range(2):
        acc = jnp.zeros((NSA_KV_HEADS * nc, 2 * NSA_HEAD_DIM), F32)
        for l in range(CMP_STRIDE):
            a = jnp.concatenate([x_refs[c * NSA_KV_HEADS + g][pl.ds(l, nc, stride=CMP_STRIDE), :]
                                 for g in range(NSA_KV_HEADS)], axis=0)
            acc += jnp.dot(a.astype(BF16), w2_ref[c, l], preferred_element_type=F32)
        first = acc[:, :NSA_HEAD_DIM]
        second = pltpu.roll(acc[:, NSA_HEAD_DIM:], NSA_KV_HEADS * nc - 1, 0)
        second = jnp.where(n_idx == nc - 1, 0.0, second)
        bias = jnp.dot(pe_ref[c], wf_ref[c], preferred_element_type=F32)[0:1]
        o_ref[c] = (first + second + bias).astype(BF16)


def compress_prompt(z, b, t, w_cmp, pe_cmp):
    nc = t // CMP_STRIDE
    hd = NSA_HEAD_DIM
    w2 = w_cmp.reshape(2, 2, CMP_STRIDE, hd, hd).transpose(0, 2, 3, 1, 4).reshape(2, CMP_STRIDE, hd, 2 * hd)
    wf = w_cmp.reshape(2, CMP_LEN * hd, hd)
    pe = jnp.broadcast_to(pe_cmp.reshape(2, 1, CMP_LEN * hd), (2, 8, CMP_LEN * hd))
    x_spec = lambda cg: pl.BlockSpec((t, hd), lambda i: (i, NSA_QD // hd + cg))
    return pl.pallas_call(
        functools.partial(_compress_kernel, nc),
        grid=(b,),
        in_specs=[x_spec(cg) for cg in range(2 * NSA_KV_HEADS)] + [
            pl.BlockSpec((2, CMP_STRIDE, hd, 2 * hd), lambda i: (0, 0, 0, 0)),
            pl.BlockSpec((2, CMP_LEN * hd, hd), lambda i: (0, 0, 0)),
            pl.BlockSpec((2, 8, CMP_LEN * hd), lambda i: (0, 0, 0)),
        ],
        out_specs=pl.BlockSpec((None, 2, NSA_KV_HEADS * nc, hd), lambda i: (i, 0, 0, 0)),
        out_shape=jax.ShapeDtypeStruct((b, 2, NSA_KV_HEADS * nc, hd), BF16),
        compiler_params=_params("parallel"),
    )(*([z] * (2 * NSA_KV_HEADS)), w2.astype(BF16), wf.astype(BF16), pe.astype(BF16))


def _nsa_prompt_kernel(t, qb, kb, nc, slope_ref, zq_ref, ks_ref, vs_ref, kw_ref, vw_ref, gz_ref, gb_ref,
                       ck_ref, cv_ref, ov_ref, o_ref, m_ref, l_ref, acc_ref, p_ref):
    g = pl.program_id(1)
    qi = pl.program_id(2)
    t0 = qi * qb
    r8 = NSA_Q_PER_KV
    hd = NSA_HEAD_DIM
    n_slc = t // SEL_BLOCK
    n_cmp = nc - 1

    zq = zq_ref[...] * (hd ** -0.5)
    q2 = jnp.concatenate([zq[:, r * hd:(r + 1) * hd] for r in range(r8)], axis=0).astype(BF16)

    slope = [slope_ref[g * r8 + r] for r in range(r8)]
    tq = lax.broadcasted_iota(jnp.int32, (qb, 1), 0) + t0
    rows = lambda r: slice(r * qb, (r + 1) * qb)

    n_i = lax.broadcasted_iota(jnp.int32, (qb, nc), 1)
    dist_ci = tq - (CMP_STRIDE * n_i + (CMP_LEN - 1))
    vis_c = (dist_ci >= 0) & (n_i < n_cmp)
    dist_c = dist_ci.astype(F32)
    s_all = lax.dot_general(q2, ck_ref[...], (((1,), (1,)), ((), ())), preferred_element_type=F32)
    psum = jnp.zeros((qb, nc), F32)
    for r in range(r8):
        s = jnp.where(vis_c, s_all[rows(r)] - slope[r] * dist_c, NEG_INF)
        e = jnp.exp(s - jnp.max(s, axis=-1, keepdims=True))
        p = jnp.where(vis_c, e / jnp.sum(e, axis=-1, keepdims=True), 0.0)
        psum = psum + p
        p_ref[rows(r), :nc] = p.astype(BF16)
    o_cmp = jnp.dot(p_ref[:, :nc], cv_ref[...], preferred_element_type=F32)

    p_hi = psum.astype(BF16)
    p_mid = (psum - p_hi.astype(F32)).astype(BF16)
    p_lo = (psum - p_hi.astype(F32) - p_mid.astype(F32)).astype(BF16)
    ov = ov_ref[...]
    imp = (jnp.dot(p_hi, ov, preferred_element_type=F32) + jnp.dot(p_mid, ov, preferred_element_type=F32)
           + jnp.dot(p_lo, ov, preferred_element_type=F32))
    j_i = lax.broadcasted_iota(jnp.int32, (qb, LANE), 1)
    cur = tq // SEL_BLOCK
    vis_j = j_i <= cur
    forced = (j_i == 0) | (j_i == cur) | (j_i == cur - 1)
    score = jnp.where(forced, FORCE_SCORE, jnp.where(vis_j, imp, -FORCE_SCORE))
    score = jnp.where(j_i < n_slc, score, -jnp.inf)
    rank = jnp.zeros((qb, LANE), jnp.int32)
    for jp in range(n_slc):
        col = score[:, jp:jp + 1]
        ahead = (col > score) | ((col == score) & (jp < j_i))
        rank = rank + ahead.astype(jnp.int32)
    selected = ((rank < min(N_SEL, n_slc)) & vis_j).astype(BF16)

    d0 = (lax.broadcasted_iota(jnp.int32, (qb, kb), 0) - lax.broadcasted_iota(jnp.int32, (qb, kb), 1))

    def flash_init():
        m_ref[...] = jnp.full(m_ref.shape, NEG_INF, F32)
        l_ref[...] = jnp.zeros(l_ref.shape, F32)
        acc_ref[...] = jnp.zeros(acc_ref.shape, F32)

    def flash_step(k_ref, v_ref, k0, mask, dist):
        k = k_ref[pl.ds(k0, kb), :].astype(BF16)
        v = v_ref[pl.ds(k0, kb), :].astype(BF16)
        s_all = lax.dot_general(q2, k, (((1,), (1,)), ((), ())), preferred_element_type=F32)
        alphas = []
        for r in range(r8):
            s = jnp.where(mask, s_all[rows(r)] - slope[r] * dist, NEG_INF)
            m_prev = m_ref[r]
            m_next = jnp.maximum(m_prev, jnp.max(s, axis=-1, keepdims=True))
            alpha = jnp.exp(m_prev - m_next)
            p = jnp.exp(s - jnp.concatenate([m_next] * (kb // LANE), axis=1))
            l_ref[r] = alpha * l_ref[r] + jnp.sum(p, axis=-1, keepdims=True)
            m_ref[r] = m_next
            p_ref[rows(r), :kb] = p.astype(BF16)
            alphas.append(alpha)
        pv = jnp.dot(p_ref[:, :kb], v, preferred_element_type=F32)
        for r in range(r8):
            acc_ref[r] = alphas[r] * acc_ref[r] + pv[rows(r)]

    flash_init()

    def sel_body(i, carry):
        k0 = pl.multiple_of(i * kb, kb)
        dist = d0 + (t0 - k0)
        blk_of_key = (lax.broadcasted_iota(jnp.int32, (LANE, kb), 1) + k0) // SEL_BLOCK
        expand = (blk_of_key == lax.broadcasted_iota(jnp.int32, (LANE, kb), 0)).astype(BF16)
        chosen = jnp.dot(selected, expand, preferred_element_type=F32)
        flash_step(ks_ref, vs_ref, k0, (chosen > 0.5) & (dist >= 0), dist.astype(F32))
        return carry

    lax.fori_loop(0, (t0 + qb - 1) // kb + 1, sel_body, 0)
    o_sel = [acc_ref[r] / l_ref[r] for r in range(r8)]

    flash_init()

    def win_body(i, carry):
        k0 = pl.multiple_of(i * kb, kb)
        dist = d0 + (t0 - k0)
        flash_step(kw_ref, vw_ref, k0, (dist >= 0) & (dist <= WINDOW), dist.astype(F32))
        return carry

    lax.fori_loop(jnp.maximum(t0 - WINDOW, 0) // kb, (t0 + qb - 1) // kb + 1, win_body, 0)
    o_win = [acc_ref[r] / l_ref[r] for r in range(r8)]

    gates = jax.nn.sigmoid(gz_ref[...] + gb_ref[...])
    lane = lax.broadcasted_iota(jnp.int32, (qb, LANE), 1)

    def gate(branch, r):
        pick = lane == branch * NSA_HEADS + g * r8 + r
        return jnp.sum(jnp.where(pick, gates, 0.0), axis=-1, keepdims=True)

    for r in range(r8):
        o = gate(0, r) * o_cmp[rows(r)] + gate(1, r) * o_sel[r] + gate(2, r) * o_win[r]
        o_ref[:, r * hd:(r + 1) * hd] = o.astype(BF16)


def nsa_prompt_attention(z, ckv, gate_bias, b, t, qb=128, kb=256):
    nc = t // CMP_STRIDE
    nq = t // qb
    hd = NSA_HEAD_DIM
    n_slc = t // SEL_BLOCK
    slopes = jnp.exp2(-8.0 * (jnp.arange(NSA_HEADS, dtype=F32) + 1.0) / NSA_HEADS)
    ci = jnp.arange(nc)[:, None]
    sj = jnp.arange(LANE)[None, :]
    ov = ((CMP_STRIDE * ci < SEL_BLOCK * (sj + 1)) & (CMP_STRIDE * ci + CMP_LEN > SEL_BLOCK * sj)
          & (sj < n_slc)).astype(BF16)
    gb = jnp.pad(gate_bias, (0, LANE - gate_bias.shape[0])).reshape(1, LANE)
    kv_spec = lambda off: pl.BlockSpec((t, hd), lambda bi, g, qi: (bi, off + g))
    kv_col = (NSA_QD + 2 * NSA_KVD) // hd
    return pl.pallas_call(
        functools.partial(_nsa_prompt_kernel, t, qb, kb, nc),
        grid=(b, NSA_KV_HEADS, nq),
        in_specs=[
            pl.BlockSpec(memory_space=pltpu.SMEM),
            pl.BlockSpec((qb, NSA_Q_PER_KV * hd), lambda bi, g, qi: (bi * nq + qi, g)),
            kv_spec(kv_col), kv_spec(kv_col + 4), kv_spec(kv_col + 8), kv_spec(kv_col + 12),
            pl.BlockSpec((qb, LANE), lambda bi, g, qi: (bi * nq + qi, Z_GATE_COL)),
            pl.BlockSpec((1, LANE), lambda bi, g, qi: (0, 0)),
            pl.BlockSpec((None, None, nc, hd), lambda bi, g, qi: (bi, 0, g, 0)),
            pl.BlockSpec((None, None, nc, hd), lambda bi, g, qi: (bi, 1, g, 0)),
            pl.BlockSpec((nc, LANE), lambda bi, g, qi: (0, 0)),
        ],
        out_specs=pl.BlockSpec((qb, NSA_Q_PER_KV * hd), lambda bi, g, qi: (bi * nq + qi, g)),
        out_shape=jax.ShapeDtypeStruct((b * t, NSA_QD), BF16),
        scratch_shapes=[pltpu.VMEM((NSA_Q_PER_KV, qb, LANE), F32), pltpu.VMEM((NSA_Q_PER_KV, qb, LANE), F32),
                        pltpu.VMEM((NSA_Q_PER_KV, qb, hd), F32),
                        pltpu.VMEM((NSA_Q_PER_KV * qb, max(kb, nc)), BF16)],
        compiler_params=_params("parallel", "parallel", "arbitrary"),
        name="nsa_prompt_attn",
    )(slopes, z, z, z, z, z, z, gb, ckv, ckv, ov)


GLA_QK = GLA_HEADS * GLA_DK
GLA_VD = GLA_HEADS * GLA_DV
GLA_SUPER = 256
_NT = (((1,), (1,)), ((), ()))
_TN = (((0,), (0,)), ((), ()))


def _chunk_scan(la, c):
    n = la.shape[0]
    ri = lax.broadcasted_iota(jnp.int32, la.shape, 0) % c
    b = la
    s = 1
    while s < c:
        b = b + jnp.where(ri >= s, pltpu.roll(b, s, 0), 0.0)
        s *= 2
    tot = jnp.where(ri == c - 1, b, 0.0)
    s = 1
    while s < c:
        tot = tot + jnp.where(ri + s < c, pltpu.roll(tot, n - s, 0), 0.0)
        s *= 2
    return b, tot


def _gla_gates(a_rows, wa_ref, ba_ref):
    x = jnp.dot(a_rows.astype(BF16), wa_ref[...], preferred_element_type=F32) + ba_ref[...]
    return jax.nn.log_sigmoid(x) / GLA_TAU


def _gla_finish(o, r, g_ref):
    return (_norm_rows(o, g_ref[...]) * (r * jax.nn.sigmoid(r))).astype(BF16)


def _gla_prompt_kernel(t, q_ref, k_ref, v_ref, r_ref, a_ref, wa_ref, ba_ref, g_ref, o_ref, s_ref, st_ref, oc_ref):
    c = GLA_CHUNK
    sc = GLA_SUPER
    st_ref[...] = jnp.zeros(st_ref.shape, F32)
    ti = lax.broadcasted_iota(jnp.int32, (sc, sc), 0)
    si = lax.broadcasted_iota(jnp.int32, (sc, sc), 1)
    causal = (ti // c == si // c) & (si <= ti)

    def body(i, carry):
        r0 = pl.multiple_of(i * sc, sc)
        la = _gla_gates(a_ref[pl.ds(r0, sc), :], wa_ref, ba_ref)
        b, b_last = _chunk_scan(la, c)
        q = q_ref[pl.ds(r0, sc), :] * (GLA_DK ** -0.5)
        k = k_ref[pl.ds(r0, sc), :]
        qe = (q * jnp.exp(b)).astype(BF16)
        ke = (k * jnp.exp(-b)).astype(BF16)
        kd = (k * jnp.exp(b_last - b)).astype(BF16)
        vb = v_ref[pl.ds(r0, sc), :].astype(BF16)
        att = jnp.where(causal, lax.dot_general(qe, ke, _NT, preferred_element_type=F32), 0.0)
        o_intra = jnp.dot(att.astype(BF16), vb, preferred_element_type=F32)
        for j in range(sc // c):
            rows = slice(j * c, (j + 1) * c)
            st = st_ref[...]
            o_inter = lax.dot_general(qe[rows], st.astype(BF16), _NT, preferred_element_type=F32)
            oc_ref[rows, :] = o_intra[rows] + o_inter
            dec = jnp.exp(b_last[j * c:j * c + 1, :])
            st_ref[...] = dec * st + lax.dot_general(vb[rows], kd[rows], _TN, preferred_element_type=F32)
        o_ref[pl.ds(r0, sc), :] = _gla_finish(oc_ref[...], r_ref[pl.ds(r0, sc), :], g_ref)
        return carry

    lax.fori_loop(0, t // sc, body, 0)
    s_ref[...] = st_ref[...].T


def gla_prompt(z, b, t, w_alpha, b_alpha, norm_g):
    dk, dv, h = GLA_DK, GLA_DV, GLA_HEADS
    wa = jnp.pad(w_alpha, ((0, LANE - GLA_RANK), (0, 0))).astype(BF16)
    return pl.pallas_call(
        functools.partial(_gla_prompt_kernel, t),
        grid=(b, h),
        in_specs=[
            pl.BlockSpec((t, dk), lambda bi, hi: (bi, hi)),
            pl.BlockSpec((t, dk), lambda bi, hi: (bi, h + hi)),
            pl.BlockSpec((t, dv), lambda bi, hi: (bi, 2 * GLA_QK // dv + hi)),
            pl.BlockSpec((t, dv), lambda bi, hi: (bi, (2 * GLA_QK + GLA_VD) // dv + hi)),
            pl.BlockSpec((t, LANE), lambda bi, hi: (bi, (2 * GLA_QK + 2 * GLA_VD) // LANE)),
            pl.BlockSpec((LANE, dk), lambda bi, hi: (0, hi)),
            pl.BlockSpec((1, dk), lambda bi, hi: (0, hi)),
            pl.BlockSpec((1, dv), lambda bi, hi: (0, 0)),
        ],
        out_specs=[
            pl.BlockSpec((t, dv), lambda bi, hi: (bi, hi)),
            pl.BlockSpec((None, None, dk, dv), lambda bi, hi: (bi, hi, 0, 0)),
        ],
        out_shape=[jax.ShapeDtypeStruct((b * t, GLA_VD), BF16), jax.ShapeDtypeStruct((b, h, dk, dv), F32)],
        scratch_shapes=[pltpu.VMEM((dv, dk), F32), pltpu.VMEM((GLA_SUPER, dv), F32)],
        compiler_params=_params("parallel", "arbitrary"),
        name="gla_prompt",
    )(z, z, z, z, z, wa, b_alpha.reshape(1, -1), norm_g.reshape(1, -1))


def _gla_sample_kernel(nb, t, q_ref, k_ref, v_ref, r_ref, a_ref, wa_ref, ba_ref, g_ref, s0_ref, o_ref, s_ref):
    n = nb * t
    la = _gla_gates(a_ref[...], wa_ref, ba_ref)
    b, b_last = _chunk_scan(la, t)
    q = q_ref[...] * (GLA_DK ** -0.5)
    k = k_ref[...]
    qe = q * jnp.exp(b)
    ke = (k * jnp.exp(-b)).astype(BF16)
    kd = k * jnp.exp(b_last - b)
    vb = v_ref[...].astype(BF16)
    ti = lax.broadcasted_iota(jnp.int32, (n, n), 0)
    si = lax.broadcasted_iota(jnp.int32, (n, n), 1)
    causal = (ti // t == si // t) & (si <= ti)
    att = jnp.where(causal, lax.dot_general(qe.astype(BF16), ke, _NT, preferred_element_type=F32), 0.0)
    o = jnp.dot(att.astype(BF16), vb, preferred_element_type=F32)
    row = lax.broadcasted_iota(jnp.int32, (n, 1), 0) // t
    for i in range(nb):
        mine = row == i
        st = s0_ref[i].T
        qe_i = jnp.where(mine, qe, 0.0).astype(BF16)
        kd_i = jnp.where(mine, kd, 0.0).astype(BF16)
        o = o + lax.dot_general(qe_i, st.astype(BF16), _NT, preferred_element_type=F32)
        dec = jnp.exp(b_last[i * t:i * t + 1, :])
        s_ref[i] = (dec * st + lax.dot_general(vb, kd_i, _TN, preferred_element_type=F32)).T
    o_ref[...] = _gla_finish(o, r_ref[...], g_ref)


def gla_sample(z, nb, t, s0, w_alpha, b_alpha, norm_g):
    dk, dv, h = GLA_DK, GLA_DV, GLA_HEADS
    n = nb * t
    wa = jnp.pad(w_alpha, ((0, LANE - GLA_RANK), (0, 0))).astype(BF16)
    return pl.pallas_call(
        functools.partial(_gla_sample_kernel, nb, t),
        grid=(h,),
        in_specs=[
            pl.BlockSpec((n, dk), lambda hi: (0, hi)),
            pl.BlockSpec((n, dk), lambda hi: (0, h + hi)),
            pl.BlockSpec((n, dv), lambda hi: (0, 2 * GLA_QK // dv + hi)),
            pl.BlockSpec((n, dv), lambda hi: (0, (2 * GLA_QK + GLA_VD) // dv + hi)),
            pl.BlockSpec((n, LANE), lambda hi: (0, (2 * GLA_QK + 2 * GLA_VD) // LANE)),
            pl.BlockSpec((LANE, dk), lambda hi: (0, hi)),
            pl.BlockSpec((1, dk), lambda hi: (0, hi)),
            pl.BlockSpec((1, dv), lambda hi: (0, 0)),
            pl.BlockSpec((nb, None, dk, dv), lambda hi: (0, hi, 0, 0)),
        ],
        out_specs=[
            pl.BlockSpec((n, dv), lambda hi: (0, hi)),
            pl.BlockSpec((nb, None, dk, dv), lambda hi: (0, hi, 0, 0)),
        ],
        out_shape=[jax.ShapeDtypeStruct((n, GLA_VD), BF16), jax.ShapeDtypeStruct((nb, h, dk, dv), F32)],
        compiler_params=_params("parallel"),
        name="gla_sample",
    )(z, z, z, z, z, wa, b_alpha.reshape(1, -1), norm_g.reshape(1, -1), s0)


def alibi_slopes():
    h = jnp.arange(NSA_HEADS, dtype=F32)
    return jnp.exp2(-8.0 * (h + 1.0) / NSA_HEADS).reshape(NSA_KV_HEADS, NSA_Q_PER_KV)


def pad_time(a, total):
    pad = [(0, 0)] * a.ndim
    pad[1] = (0, total - a.shape[1])
    return jnp.pad(a, pad)


def nsa_split(z, gate_bias, b, t):
    q = z[:, :NSA_QD].reshape(b, t, NSA_KV_HEADS, NSA_Q_PER_KV, NSA_HEAD_DIM) * (NSA_HEAD_DIM ** -0.5)
    kv = z[:, NSA_QD:NSA_QD + 6 * NSA_KVD].reshape(b, t, 3, 2, NSA_KV_HEADS, NSA_HEAD_DIM)
    g = jax.nn.sigmoid(z[:, NSA_QD + 6 * NSA_KVD:NSA_QD + 6 * NSA_KVD + 3 * NSA_HEADS] + gate_bias)
    g = g.reshape(b, t, 3, NSA_KV_HEADS, NSA_Q_PER_KV)
    return q, kv[:, :, 0], kv[:, :, 1], kv[:, :, 2], g


def compress_kv(kv_all, w_cmp, pe_cmp):
    B, Tp = kv_all.shape[:2]
    ch = kv_all.reshape(B, Tp // CMP_STRIDE, CMP_STRIDE, 2, NSA_KV_HEADS, NSA_HEAD_DIM)
    first = jnp.einsum('bnlcgd,clde->bncge', ch, w_cmp[:, :CMP_STRIDE])
    second = jnp.einsum('bnlcgd,clde->bncge', ch, w_cmp[:, CMP_STRIDE:])
    bias = jnp.einsum('cld,clde->ce', pe_cmp, w_cmp)
    return first[:, :-1] + second[:, 1:] + bias[:, None, :]


def cmp_branch(q, t_pos, ckv, slopes):
    n = ckv.shape[1]
    ck, cv = ckv[:, :, 0], ckv[:, :, 1]
    end = CMP_STRIDE * jnp.arange(n) + (CMP_LEN - 1)
    dist = (t_pos[:, None] - end[None, :]).astype(F32)
    vis = dist >= 0
    s = jnp.einsum('btgrd,bngd->bgrtn', q, ck).astype(F32)
    s = jnp.where(vis, s - slopes[:, :, None, None] * dist, NEG_INF)
    p = jnp.where(vis, jax.nn.softmax(s, axis=-1), 0.0)
    o = jnp.einsum('bgrtn,bngd->btgrd', p.astype(cv.dtype), cv)
    return o, p


def select_blocks(p, t_pos, n_slc):
    n_cmp = p.shape[-1]
    ci = jnp.arange(n_cmp)[:, None]
    sj = jnp.arange(n_slc)[None, :]
    ov = ((CMP_STRIDE * ci < SEL_BLOCK * (sj + 1)) & (CMP_STRIDE * ci + CMP_LEN > SEL_BLOCK * sj)).astype(F32)
    imp = jnp.einsum('bgrtn,nj->bgtj', p, ov)
    cur = (t_pos // SEL_BLOCK)[:, None]
    j = jnp.arange(n_slc)[None, :]
    vis = j <= cur
    forced = (j == 0) | (j == cur) | (j == cur - 1)
    score = jnp.where(forced, FORCE_SCORE, jnp.where(vis, imp, -FORCE_SCORE))
    _, idx = lax.top_k(score, min(N_SEL, n_slc))
    valid = jnp.take_along_axis(jnp.broadcast_to(vis, score.shape), idx, axis=-1)
    return idx, valid


def nsa_global(q, t_pos, kv_cmp_all, kv_sel_all, w_cmp, pe_cmp, slopes):
    B, T_all = kv_cmp_all.shape[:2]
    Tp = -(-T_all // SEL_BLOCK) * SEL_BLOCK
    ckv = compress_kv(pad_time(kv_cmp_all, Tp), w_cmp, pe_cmp)
    o_cmp, p = cmp_branch(q, t_pos, ckv, slopes)
    n_slc = Tp // SEL_BLOCK
    idx, valid = select_blocks(p, t_pos, n_slc)
    blk = pad_time(kv_sel_all, Tp).reshape(B, n_slc, SEL_BLOCK, 2, NSA_KV_HEADS, NSA_HEAD_DIM)
    blk = blk.transpose(3, 0, 4, 1, 2, 5)
    return o_cmp, idx, valid, blk[0], blk[1]


def sel_branch(q_c, t_c, idx_c, valid_c, kb, vb, slopes):
    B, QC = q_c.shape[:2]
    k = idx_c.shape[-1]
    bi = jnp.arange(B)[:, None, None, None]
    gi = jnp.arange(NSA_KV_HEADS)[None, :, None, None]
    kg = kb[bi, gi, idx_c].reshape(B, NSA_KV_HEADS, QC, k * SEL_BLOCK, NSA_HEAD_DIM)
    vg = vb[bi, gi, idx_c].reshape(B, NSA_KV_HEADS, QC, k * SEL_BLOCK, NSA_HEAD_DIM)
    s_pos = idx_c[..., None] * SEL_BLOCK + jnp.arange(SEL_BLOCK)
    dist = t_c[None, None, :, None, None] - s_pos
    mask = ((dist >= 0) & valid_c[..., None]).reshape(B, NSA_KV_HEADS, QC, k * SEL_BLOCK)
    dist = dist.astype(F32).reshape(B, NSA_KV_HEADS, QC, k * SEL_BLOCK)
    s = jnp.einsum('bqgrd,bgqsd->bgrqs', q_c, kg).astype(F32)
    s = jnp.where(mask[:, :, None], s - slopes[None, :, :, None, None] * dist[:, :, None], NEG_INF)
    p = jax.nn.softmax(s, axis=-1)
    return jnp.einsum('bgrqs,bgqsd->bqgrd', p.astype(vg.dtype), vg)


def win_branch(q_c, t_c, kw, vw, s_pos, slopes):
    dist = t_c[:, None] - s_pos[None, :]
    mask = (dist >= 0) & (dist <= WINDOW) & (s_pos >= 0)[None, :]
    s = jnp.einsum('bqgrd,bsgd->bgrqs', q_c, kw).astype(F32)
    s = jnp.where(mask, s - slopes[:, :, None, None] * dist.astype(F32), NEG_INF)
    p = jax.nn.softmax(s, axis=-1)
    return jnp.einsum('bgrqs,bsgd->bqgrd', p.astype(vw.dtype), vw)


def nsa_combine(g, o_cmp, o_sel, o_win):
    B, T = g.shape[:2]
    o = g[:, :, 0, :, :, None] * o_cmp + g[:, :, 1, :, :, None] * o_sel + g[:, :, 2, :, :, None] * o_win
    return o.reshape(B * T, NSA_QD).astype(BF16)


def nsa_prompt_mix(z, b, t, gate_bias, w_cmp, pe_cmp):
    ckv = compress_prompt(z, b, t, w_cmp, pe_cmp)
    o = nsa_prompt_attention(z, ckv, gate_bias, b, t)
    kv = z[:, NSA_QD:NSA_QD + 6 * NSA_KVD].reshape(b, t, 3, 2, NSA_KV_HEADS, NSA_HEAD_DIM)
    return o, kv[:, :, 0], kv[:, :, 1], kv[:, t - min(WINDOW, t):, 2]


def nsa_sample_mix(z, b, t, past_cmp, past_sel, win_buf, gate_bias, w_cmp, pe_cmp):
    slopes = alibi_slopes()
    q, kv_cmp, kv_sel, kv_win, g = nsa_split(z, gate_bias, b, t)
    t_pos = PAST_LEN + jnp.arange(t)
    kv_cmp_all = jnp.concatenate([past_cmp, kv_cmp], axis=1)
    kv_sel_all = jnp.concatenate([past_sel, kv_sel], axis=1)
    o_cmp, idx, valid, kb, vb = nsa_global(q, t_pos, kv_cmp_all, kv_sel_all, w_cmp, pe_cmp, slopes)
    wb = win_buf.shape[1]
    kw_all = jnp.concatenate([win_buf, kv_win], axis=1)
    s_pos = PAST_LEN - wb + jnp.arange(wb + t)
    o_sel = sel_branch(q, t_pos, idx, valid, kb, vb, slopes)
    o_win = win_branch(q, t_pos, kw_all[:, :, 0], kw_all[:, :, 1], s_pos, slopes)
    return nsa_combine(g, o_cmp, o_sel, o_win), kv_cmp, kv_sel, kw_all[:, t:]


def gla_recurrence(q, k, v, log_a, s0):
    B, T, H, DK = q.shape
    DV = v.shape[-1]
    C = min(GLA_CHUNK, T)
    N = T // C
    qf = (q * (DK ** -0.5)).reshape(B, N, C, H, DK)
    kf = k.reshape(B, N, C, H, DK)
    vf = v.reshape(B, N, C, H, DV)
    b = jnp.cumsum(log_a.reshape(B, N, C, H, DK), axis=2)
    b_last = b[:, :, -1]
    qe = qf * jnp.exp(b)
    ke = kf * jnp.exp(-b)
    kd = kf * jnp.exp(b_last[:, :, None] - b)
    causal = jnp.tril(jnp.ones((C, C), dtype=bool))
    att = jnp.where(causal, jnp.einsum('bnthd,bnshd->bnhts', qe, ke), 0.0)
    o_intra = jnp.einsum('bnhts,bnshe->bnthe', att, vf)

    def step(s, xs_n):
        qe_n, kd_n, v_n, dec_n = xs_n
        o_n = jnp.einsum('bthd,bhde->bthe', qe_n, s)
        s = dec_n[..., None] * s + jnp.einsum('bthd,bthe->bhde', kd_n, v_n)
        return s, o_n

    xs = (jnp.moveaxis(qe, 1, 0), jnp.moveaxis(kd, 1, 0), jnp.moveaxis(vf, 1, 0),
          jnp.moveaxis(jnp.exp(b_last), 1, 0))
    s_fin, o_inter = lax.scan(step, s0, xs)
    o = (o_intra + jnp.moveaxis(o_inter, 0, 1)).reshape(B, T, H, DV)
    return o, s_fin


def gla_mix(z, b, t, s0, w_alpha, b_alpha, norm_g):
    qk = GLA_HEADS * GLA_DK
    vd = GLA_HEADS * GLA_DV
    z = z.reshape(b, t, -1)
    q = z[..., :qk].reshape(b, t, GLA_HEADS, GLA_DK)
    k = z[..., qk:2 * qk].reshape(b, t, GLA_HEADS, GLA_DK)
    v = z[..., 2 * qk:2 * qk + vd].reshape(b, t, GLA_HEADS, GLA_DV)
    r = z[..., 2 * qk + vd:2 * qk + 2 * vd]
    a = z[..., 2 * qk + 2 * vd:2 * qk + 2 * vd + GLA_RANK]
    log_a = jax.nn.log_sigmoid(a @ w_alpha + b_alpha) / GLA_TAU
    o, s_new = gla_recurrence(q, k, v, log_a.reshape(b, t, GLA_HEADS, GLA_DK), s0)
    o = _norm_rows(o, norm_g).reshape(b, t, vd) * jax.nn.silu(r)
    return o.reshape(b * t, vd).astype(BF16), s_new


def _pad_cols(w, mult):
    n = w.shape[-1]
    return jnp.pad(w, ((0, 0), (0, -(-n // mult) * mult - n)))


def kernel(x_prompt, x_sample, p_prompt, p_sample, cache_cmp_kv, cache_sel_kv, cache_win_kv, state_gla, page_table, ffn1_norm, ffn1_w_gu, ffn1_w_down, mix_norm, nsa_w_in, nsa_gate_bias, nsa_w_cmp, nsa_pe_cmp, nsa_w_out, gla_w_in, gla_w_alpha, gla_b_alpha, gla_norm, gla_w_out, ffn2_norm, ffn2_w_gu, ffn2_w_down, ple_norm, ple_w_gate, ple_w_proj, final_norm):
    bp, tp, d = x_prompt.shape
    bs, ts, _ = x_sample.shape
    db, n_pages = page_table.shape
    mp, ms = bp * tp, bs * ts
    xp = x_prompt.reshape(mp, d)
    xs = x_sample.reshape(ms, d)
    tm_p = _row_tile(mp, 512)
    tm_p2 = _row_tile(mp, 512)

    cmp_p, sel_p, win_p, gla_p = [], [], [], []
    cmp_s, sel_s, win_s, gla_s = [], [], [], []
    for i in range(DEPTH):
        w_gu1 = ffn1_w_gu[i].astype(BF16)
        w_d1 = ffn1_w_down[i].astype(BF16)
        xp = ffn(xp, ffn1_norm[i], w_gu1, w_d1, tm_p)
        xs = ffn(xs, ffn1_norm[i], w_gu1, w_d1, ms)
        j = i // 2
        if i % 2 == 0:
            w_in = _pad_cols(nsa_w_in[j], 512).astype(BF16)
            w_out = nsa_w_out[j].astype(BF16)
            zp = norm_matmul(xp, mix_norm[i], w_in, tm_p2)
            zs = norm_matmul(xs, mix_norm[i], w_in, ms)
            op, kc, kl, kw = nsa_prompt_mix(zp, bp, tp, nsa_gate_bias[j], nsa_w_cmp[j], nsa_pe_cmp[j])
            past_cmp = cache_cmp_kv[j][page_table].reshape(db, n_pages * PAGE_SIZE, 2, NSA_KV_HEADS, NSA_HEAD_DIM)
            past_sel = cache_sel_kv[j][page_table].reshape(db, n_pages * PAGE_SIZE, 2, NSA_KV_HEADS, NSA_HEAD_DIM)
            os_, kc2, kl2, kw2 = nsa_sample_mix(zs, bs, ts, past_cmp, past_sel, cache_win_kv[j],
                                                nsa_gate_bias[j], nsa_w_cmp[j], nsa_pe_cmp[j])
            cmp_p.append(kc)
            sel_p.append(kl)
            win_p.append(kw)
            cmp_s.append(kc2)
            sel_s.append(kl2)
            win_s.append(kw2)
        else:
            w_in = _pad_cols(gla_w_in[j], 512).astype(BF16)
            w_out = gla_w_out[j].astype(BF16)
            zp = norm_matmul(xp, mix_norm[i], w_in, tm_p2)
            zs = norm_matmul(xs, mix_norm[i], w_in, ms)
            op, sp = gla_prompt(zp, bp, tp, gla_w_alpha[j], gla_b_alpha[j], gla_norm[j])
            os_, ss = gla_sample(zs, bs, ts, state_gla[j], gla_w_alpha[j], gla_b_alpha[j], gla_norm[j])
            gla_p.append(sp)
            gla_s.append(ss)
        xp = matmul_residual(op, w_out, xp, tm_p)
        xs = matmul_residual(os_, w_out, xs, ms)
        w_gu2 = ffn2_w_gu[i].astype(BF16)
        w_d2 = ffn2_w_down[i].astype(BF16)
        xp = ffn(xp, ffn2_norm[i], w_gu2, w_d2, tm_p)
        xs = ffn(xs, ffn2_norm[i], w_gu2, w_d2, ms)
        w_pg = ple_w_gate[i].astype(BF16)
        w_pp = ple_w_proj[i].astype(BF16)
        xp = ple(xp, p_prompt[i].reshape(mp, -1), ple_norm[i], w_pg, w_pp, tm_p2)
        xs = ple(xs, p_sample[i].reshape(ms, -1), ple_norm[i], w_pg, w_pp, ms)
    y_prompt = rmsnorm(xp, final_norm, tm_p2).reshape(bp, tp, d)
    y_sample = rmsnorm(xs, final_norm, ms).reshape(bs, ts, d)
    return (y_prompt, y_sample, jnp.stack(cmp_p), jnp.stack(sel_p), jnp.stack(win_p), jnp.stack(gla_p),
            jnp.stack(cmp_s), jnp.stack(sel_s), jnp.stack(win_s), jnp.stack(gla_s))
```

```python
import functools
import math

import jax
import jax.numpy as jnp
from jax import lax
from jax.experimental import pallas as pl
from jax.experimental.pallas import tpu as pltpu

F32 = jnp.float32
BF16 = jnp.bfloat16

D_MODEL = 4096
DEPTH = 2
PAST_LEN = 16384
PAGE_SIZE = 128
NSA_HEADS = 32
NSA_HEAD_DIM = 128
NSA_KV_HEADS = 4
NSA_Q_PER_KV = 8
CMP_STRIDE = 16
CMP_LEN = 32
SEL_BLOCK = 64
N_SEL = 16
WINDOW = 512
NSA_QBLOCK = 32
NSA_QD = NSA_HEADS * NSA_HEAD_DIM
NSA_KVD = NSA_KV_HEADS * NSA_HEAD_DIM
GLA_HEADS = 8
GLA_DK = 256
GLA_DV = 512
GLA_RANK = 16
GLA_TAU = 16.0
GLA_CHUNK = 32
D_FF = 11008
EPS = 1e-6
NEG_INF = -1e30
FORCE_SCORE = 1e30

V7X_VMEM_LIMIT_BYTES = 56 * 1024 * 1024
LANE = 128


def _params(*sem):
    return pltpu.CompilerParams(dimension_semantics=sem, vmem_limit_bytes=V7X_VMEM_LIMIT_BYTES)


def _norm_rows(x, g):
    ms = jnp.mean(x * x, axis=-1, keepdims=True)
    return (x * lax.rsqrt(ms + EPS)) * g


def _row_tile(m, want):
    return want if m % want == 0 else m


def _ffn_kernel(x_ref, g_ref, wa_ref, wu_ref, wd_ref, o_ref, h_ref):
    @pl.when(pl.program_id(1) == 0)
    def _():
        x = x_ref[...]
        h_ref[...] = _norm_rows(x, g_ref[...]).astype(BF16)
        o_ref[...] = x

    h = h_ref[...]
    a = jnp.dot(h, wa_ref[...], preferred_element_type=F32)
    u = jnp.dot(h, wu_ref[...], preferred_element_type=F32)
    act = (0.5 * (a * jax.nn.sigmoid(a)) * u).astype(BF16)
    o_ref[...] += jnp.dot(act, wd_ref[...], preferred_element_type=F32)


def ffn(x, g, w_gu, w_down, tm, tf=256):
    m, d = x.shape
    nf = D_FF // tf
    return pl.pallas_call(
        _ffn_kernel,
        grid=(m // tm, nf),
        in_specs=[
            pl.BlockSpec((tm, d), lambda i, j: (i, 0), pipeline_mode=pl.Buffered(1)),
            pl.BlockSpec((1, d), lambda i, j: (0, 0)),
            pl.BlockSpec((d, tf), lambda i, j: (0, j)),
            pl.BlockSpec((d, tf), lambda i, j: (0, j + nf)),
            pl.BlockSpec((tf, d), lambda i, j: (j, 0)),
        ],
        out_specs=pl.BlockSpec((tm, d), lambda i, j: (i, 0)),
        out_shape=jax.ShapeDtypeStruct((m, d), F32),
        scratch_shapes=[pltpu.VMEM((tm, d), BF16)],
        compiler_params=_params("parallel", "arbitrary"),
    )(x, g.reshape(1, d), w_gu, w_gu, w_down)


def _norm_matmul_kernel(x_ref, g_ref, w_ref, o_ref, h_ref):
    @pl.when(pl.program_id(1) == 0)
    def _():
        h_ref[...] = _norm_rows(x_ref[...], g_ref[...]).astype(BF16)

    o_ref[...] = jnp.dot(h_ref[...], w_ref[...], preferred_element_type=F32)


def norm_matmul(x, g, w, tm, tn=512):
    m, d = x.shape
    n = w.shape[1]
    return pl.pallas_call(
        _norm_matmul_kernel,
        grid=(m // tm, n // tn),
        in_specs=[
            pl.BlockSpec((tm, d), lambda i, j: (i, 0)),
            pl.BlockSpec((1, d), lambda i, j: (0, 0)),
            pl.BlockSpec((d, tn), lambda i, j: (0, j)),
        ],
        out_specs=pl.BlockSpec((tm, tn), lambda i, j: (i, j)),
        out_shape=jax.ShapeDtypeStruct((m, n), F32),
        scratch_shapes=[pltpu.VMEM((tm, d), BF16)],
        compiler_params=_params("parallel", "arbitrary"),
    )(x, g.reshape(1, d), w)


def _matmul_residual_kernel(a_ref, w_ref, x_ref, o_ref):
    o_ref[...] = x_ref[...] + jnp.dot(a_ref[...], w_ref[...], preferred_element_type=F32)


def matmul_residual(a, w, x, tm, tn=512):
    m, k = a.shape
    n = w.shape[1]
    return pl.pallas_call(
        _matmul_residual_kernel,
        grid=(m // tm, n // tn),
        in_specs=[
            pl.BlockSpec((tm, k), lambda i, j: (i, 0)),
            pl.BlockSpec((k, tn), lambda i, j: (0, j)),
            pl.BlockSpec((tm, tn), lambda i, j: (i, j)),
        ],
        out_specs=pl.BlockSpec((tm, tn), lambda i, j: (i, j)),
        out_shape=jax.ShapeDtypeStruct((m, n), F32),
        compiler_params=_params("parallel", "arbitrary"),
    )(a, w, x)


def _ple_kernel(tn, x_ref, g_ref, wg_ref, p_ref, wp_ref, o_ref, h_ref):
    j = pl.program_id(1)

    @pl.when(j == 0)
    def _():
        h_ref[...] = _norm_rows(x_ref[...], g_ref[...]).astype(BF16)

    gate = jax.nn.sigmoid(jnp.dot(h_ref[...], wg_ref[...], preferred_element_type=F32))
    proj = jnp.dot(p_ref[...].astype(BF16), wp_ref[...], preferred_element_type=F32)
    col = pl.multiple_of(j * tn, tn)
    o_ref[...] = x_ref[:, pl.ds(col, tn)] + gate * proj


def ple(x, p, g, w_gate, w_proj, tm, tn=512):
    m, d = x.shape
    pd = p.shape[1]
    return pl.pallas_call(
        functools.partial(_ple_kernel, tn),
        grid=(m // tm, d // tn),
        in_specs=[
            pl.BlockSpec((tm, d), lambda i, j: (i, 0)),
            pl.BlockSpec((1, d), lambda i, j: (0, 0)),
            pl.BlockSpec((d, tn), lambda i, j: (0, j)),
            pl.BlockSpec((tm, pd), lambda i, j: (i, 0)),
            pl.BlockSpec((pd, tn), lambda i, j: (0, j)),
        ],
        out_specs=pl.BlockSpec((tm, tn), lambda i, j: (i, j)),
        out_shape=jax.ShapeDtypeStruct((m, d), F32),
        scratch_shapes=[pltpu.VMEM((tm, d), BF16)],
        compiler_params=_params("parallel", "arbitrary"),
    )(x, g.reshape(1, d), w_gate, p, w_proj)


def _rmsnorm_kernel(x_ref, g_ref, o_ref):
    o_ref[...] = _norm_rows(x_ref[...], g_ref[...])


def rmsnorm(x, g, tm):
    m, d = x.shape
    return pl.pallas_call(
        _rmsnorm_kernel,
        grid=(m // tm,),
        in_specs=[pl.BlockSpec((tm, d), lambda i: (i, 0)), pl.BlockSpec((1, d), lambda i: (0, 0))],
        out_specs=pl.BlockSpec((tm, d), lambda i: (i, 0)),
        out_shape=jax.ShapeDtypeStruct((m, d), F32),
        compiler_params=_params("parallel"),
    )(x, g.reshape(1, d))


Z_GATE_COL = (NSA_QD + 6 * NSA_KVD) // LANE


def _compress_kernel(nc, x_ref, w2_ref, wf_ref, pe_ref, o_ref):
    acc = jnp.zeros((nc, 2 * NSA_HEAD_DIM), F32)
    for l in range(CMP_STRIDE):
        a = x_ref[pl.ds(l, nc, stride=CMP_STRIDE), :]
        acc += jnp.dot(a.astype(BF16), w2_ref[l], preferred_element_type=F32)
    n_idx = lax.broadcasted_iota(jnp.int32, (nc, NSA_HEAD_DIM), 0)
    first = acc[:, :NSA_HEAD_DIM]
    second = jnp.where(n_idx == nc - 1, 0.0, pltpu.roll(acc[:, NSA_HEAD_DIM:], nc - 1, 0))
    bias = jnp.dot(pe_ref[...], wf_ref[...], preferred_element_type=F32)[0:1]
    o_ref[...] = (first + second + bias).astype(BF16)


def compress(x, nb, nc, col0, w_cmp, pe_cmp):
    hd = NSA_HEAD_DIM
    w2 = w_cmp.reshape(2, 2, CMP_STRIDE, hd, hd).transpose(0, 2, 3, 1, 4).reshape(2, CMP_STRIDE, hd, 2 * hd)
    wf = w_cmp.reshape(2, CMP_LEN * hd, hd)
    pe = jnp.broadcast_to(pe_cmp.reshape(2, 1, CMP_LEN * hd), (2, 8, CMP_LEN * hd))
    return pl.pallas_call(
        functools.partial(_compress_kernel, nc),
        grid=(nb, 2, NSA_KV_HEADS),
        in_specs=[
            pl.BlockSpec((CMP_STRIDE * nc, hd), lambda i, c, g: (i, col0 + c * NSA_KV_HEADS + g)),
            pl.BlockSpec((None, CMP_STRIDE, hd, 2 * hd), lambda i, c, g: (c, 0, 0, 0)),
            pl.BlockSpec((None, CMP_LEN * hd, hd), lambda i, c, g: (c, 0, 0)),
            pl.BlockSpec((None, 8, CMP_LEN * hd), lambda i, c, g: (c, 0, 0)),
        ],
        out_specs=pl.BlockSpec((None, None, None, nc, hd), lambda i, c, g: (i, c, g, 0, 0)),
        out_shape=jax.ShapeDtypeStruct((nb, 2, NSA_KV_HEADS, nc, hd), BF16),
        compiler_params=_params("parallel", "parallel", "parallel"),
        name="nsa_compress",
    )(x, w2.astype(BF16), wf.astype(BF16), pe.astype(BF16))


def _nsa_prompt_kernel(t, qb, kb, nc, slope_ref, zq_ref, ks_ref, vs_ref, kw_ref, vw_ref, gz_ref, gb_ref,
                       ck_ref, cv_ref, ov_ref, o_ref, m_ref, l_ref, acc_ref, p_ref):
    g = pl.program_id(1)
    qi = pl.program_id(2)
    t0 = qi * qb
    r8 = NSA_Q_PER_KV
    hd = NSA_HEAD_DIM
    n_slc = t // SEL_BLOCK
    n_cmp = nc - 1

    zq = zq_ref[...] * (hd ** -0.5)
    q2 = jnp.concatenate([zq[:, r * hd:(r + 1) * hd] for r in range(r8)], axis=0).astype(BF16)

    slope = [slope_ref[g * r8 + r] for r in range(r8)]
    tq = lax.broadcasted_iota(jnp.int32, (qb, 1), 0) + t0
    rows = lambda r: slice(r * qb, (r + 1) * qb)

    n_i = lax.broadcasted_iota(jnp.int32, (qb, nc), 1)
    dist_ci = tq - (CMP_STRIDE * n_i + (CMP_LEN - 1))
    vis_c = (dist_ci >= 0) & (n_i < n_cmp)
    dist_c = dist_ci.astype(F32)
    s_all = lax.dot_general(q2, ck_ref[...], (((1,), (1,)), ((), ())), preferred_element_type=F32)
    psum = jnp.zeros((qb, nc), F32)
    for r in range(r8):
        s = jnp.where(vis_c, s_all[rows(r)] - slope[r] * dist_c, NEG_INF)
        e = jnp.exp(s - jnp.max(s, axis=-1, keepdims=True))
        p = jnp.where(vis_c, e / jnp.sum(e, axis=-1, keepdims=True), 0.0)
        psum = psum + p
        p_ref[rows(r), :nc] = p.astype(BF16)
    o_cmp = jnp.dot(p_ref[:, :nc], cv_ref[...], preferred_element_type=F32)

    p_hi = psum.astype(BF16)
    p_mid = (psum - p_hi.astype(F32)).astype(BF16)
    p_lo = (psum - p_hi.astype(F32) - p_mid.astype(F32)).astype(BF16)
    ov = ov_ref[...]
    imp = (jnp.dot(p_hi, ov, preferred_element_type=F32) + jnp.dot(p_mid, ov, preferred_element_type=F32)
           + jnp.dot(p_lo, ov, preferred_element_type=F32))
    j_i = lax.broadcasted_iota(jnp.int32, (qb, LANE), 1)
    cur = tq // SEL_BLOCK
    vis_j = j_i <= cur
    forced = (j_i == 0) | (j_i == cur) | (j_i == cur - 1)
    score = jnp.where(forced, FORCE_SCORE, jnp.where(vis_j, imp, -FORCE_SCORE))
    score = jnp.where(j_i < n_slc, score, -jnp.inf)
    rank = jnp.zeros((qb, LANE), jnp.int32)
    for jp in range(n_slc):
        col = score[:, jp:jp + 1]
        ahead = (col > score) | ((col == score) & (jp < j_i))
        rank = rank + ahead.astype(jnp.int32)
    selected = ((rank < min(N_SEL, n_slc)) & vis_j).astype(BF16)

    d0 = (lax.broadcasted_iota(jnp.int32, (qb, kb), 0) - lax.broadcasted_iota(jnp.int32, (qb, kb), 1))

    def flash_init():
        m_ref[...] = jnp.full(m_ref.shape, NEG_INF, F32)
        l_ref[...] = jnp.zeros(l_ref.shape, F32)
        acc_ref[...] = jnp.zeros(acc_ref.shape, F32)

    def flash_step(k_ref, v_ref, k0, mask, dist):
        k = k_ref[pl.ds(k0, kb), :].astype(BF16)
        v = v_ref[pl.ds(k0, kb), :].astype(BF16)
        s_all = lax.dot_general(q2, k, (((1,), (1,)), ((), ())), preferred_element_type=F32)
        alphas = []
        for r in range(r8):
            s = jnp.where(mask, s_all[rows(r)] - slope[r] * dist, NEG_INF)
            m_prev = m_ref[r]
            m_next = jnp.maximum(m_prev, jnp.max(s, axis=-1, keepdims=True))
            alpha = jnp.exp(m_prev - m_next)
            p = jnp.exp(s - jnp.concatenate([m_next] * (kb // LANE), axis=1))
            l_ref[r] = alpha * l_ref[r] + jnp.sum(p, axis=-1, keepdims=True)
            m_ref[r] = m_next
            p_ref[rows(r), :kb] = p.astype(BF16)
            alphas.append(alpha)
        pv = jnp.dot(p_ref[:, :kb], v, preferred_element_type=F32)
        for r in range(r8):
            acc_ref[r] = alphas[r] * acc_ref[r] + pv[rows(r)]

    flash_init()

    def sel_body(i, carry):
        k0 = pl.multiple_of(i * kb, kb)
        dist = d0 + (t0 - k0)
        blk_of_key = (lax.broadcasted_iota(jnp.int32, (LANE, kb), 1) + k0) // SEL_BLOCK
        expand = (blk_of_key == lax.broadcasted_iota(jnp.int32, (LANE, kb), 0)).astype(BF16)
        chosen = jnp.dot(selected, expand, preferred_element_type=F32)
        flash_step(ks_ref, vs_ref, k0, (chosen > 0.5) & (dist >= 0), dist.astype(F32))
        return carry

    lax.fori_loop(0, (t0 + qb - 1) // kb + 1, sel_body, 0)
    o_sel = [acc_ref[r] / l_ref[r] for r in range(r8)]

    flash_init()

    def win_body(i, carry):
        k0 = pl.multiple_of(i * kb, kb)
        dist = d0 + (t0 - k0)
        flash_step(kw_ref, vw_ref, k0, (dist >= 0) & (dist <= WINDOW), dist.astype(F32))
        return carry

    lax.fori_loop(jnp.maximum(t0 - WINDOW, 0) // kb, (t0 + qb - 1) // kb + 1, win_body, 0)
    o_win = [acc_ref[r] / l_ref[r] for r in range(r8)]

    gates = jax.nn.sigmoid(gz_ref[...] + gb_ref[...])
    lane = lax.broadcasted_iota(jnp.int32, (qb, LANE), 1)

    def gate(branch, r):
        pick = lane == branch * NSA_HEADS + g * r8 + r
        return jnp.sum(jnp.where(pick, gates, 0.0), axis=-1, keepdims=True)

    for r in range(r8):
        o = gate(0, r) * o_cmp[rows(r)] + gate(1, r) * o_sel[r] + gate(2, r) * o_win[r]
        o_ref[:, r * hd:(r + 1) * hd] = o.astype(BF16)


def nsa_prompt_attention(z, ckv, gate_bias, b, t, qb=128, kb=256):
    nc = t // CMP_STRIDE
    nq = t // qb
    hd = NSA_HEAD_DIM
    n_slc = t // SEL_BLOCK
    slopes = jnp.exp2(-8.0 * (jnp.arange(NSA_HEADS, dtype=F32) + 1.0) / NSA_HEADS)
    ci = jnp.arange(nc)[:, None]
    sj = jnp.arange(LANE)[None, :]
    ov = ((CMP_STRIDE * ci < SEL_BLOCK * (sj + 1)) & (CMP_STRIDE * ci + CMP_LEN > SEL_BLOCK * sj)
          & (sj < n_slc)).astype(BF16)
    gb = jnp.pad(gate_bias, (0, LANE - gate_bias.shape[0])).reshape(1, LANE)
    kv_spec = lambda off: pl.BlockSpec((t, hd), lambda bi, g, qi: (bi, off + g))
    kv_col = (NSA_QD + 2 * NSA_KVD) // hd
    return pl.pallas_call(
        functools.partial(_nsa_prompt_kernel, t, qb, kb, nc),
        grid=(b, NSA_KV_HEADS, nq),
        in_specs=[
            pl.BlockSpec(memory_space=pltpu.SMEM),
            pl.BlockSpec((qb, NSA_Q_PER_KV * hd), lambda bi, g, qi: (bi * nq + qi, g)),
            kv_spec(kv_col), kv_spec(kv_col + 4), kv_spec(kv_col + 8), kv_spec(kv_col + 12),
            pl.BlockSpec((qb, LANE), lambda bi, g, qi: (bi * nq + qi, Z_GATE_COL)),
            pl.BlockSpec((1, LANE), lambda bi, g, qi: (0, 0)),
            pl.BlockSpec((None, None, None, nc, hd), lambda bi, g, qi: (bi, 0, g, 0, 0)),
            pl.BlockSpec((None, None, None, nc, hd), lambda bi, g, qi: (bi, 1, g, 0, 0)),
            pl.BlockSpec((nc, LANE), lambda bi, g, qi: (0, 0)),
        ],
        out_specs=pl.BlockSpec((qb, NSA_Q_PER_KV * hd), lambda bi, g, qi: (bi * nq + qi, g)),
        out_shape=jax.ShapeDtypeStruct((b * t, NSA_QD), BF16),
        scratch_shapes=[pltpu.VMEM((NSA_Q_PER_KV, qb, LANE), F32), pltpu.VMEM((NSA_Q_PER_KV, qb, LANE), F32),
                        pltpu.VMEM((NSA_Q_PER_KV, qb, hd), F32),
                        pltpu.VMEM((NSA_Q_PER_KV * qb, max(kb, nc)), BF16)],
        compiler_params=_params("parallel", "parallel", "arbitrary"),
        name="nsa_prompt_attn",
    )(slopes, z, z, z, z, z, z, gb, ckv, ckv, ov)


_NT = (((1,), (1,)), ((), ()))
_TN = (((0,), (0,)), ((), ()))


def _page_gather_kernel(n_pages, tail_rows, pt_ref, cache_ref, tail_ref, o_ref, sem):
    b = pl.program_id(0)
    base = b * (n_pages * PAGE_SIZE + tail_rows)

    def page_copy(p):
        dst = o_ref.at[pl.ds(pl.multiple_of(base + p * PAGE_SIZE, PAGE_SIZE), PAGE_SIZE)]
        return pltpu.make_async_copy(cache_ref.at[pt_ref[b, p]], dst, sem)

    tail_copy = pltpu.make_async_copy(
        tail_ref.at[b], o_ref.at[pl.ds(pl.multiple_of(base + n_pages * PAGE_SIZE, PAGE_SIZE), tail_rows)], sem)

    def start(p, carry):
        page_copy(p).start()
        return carry

    def wait(p, carry):
        page_copy(p).wait()
        return carry

    lax.fori_loop(0, n_pages, start, 0)
    tail_copy.start()
    lax.fori_loop(0, n_pages, wait, 0)
    tail_copy.wait()


def page_gather(cache, page_table, tail):
    nb, n_pages = page_table.shape
    tail_rows, w = tail.shape[1], tail.shape[2]
    return pl.pallas_call(
        functools.partial(_page_gather_kernel, n_pages, tail_rows),
        grid_spec=pltpu.PrefetchScalarGridSpec(
            num_scalar_prefetch=1,
            grid=(nb,),
            in_specs=[pl.BlockSpec(memory_space=pl.ANY), pl.BlockSpec(memory_space=pl.ANY)],
            out_specs=pl.BlockSpec(memory_space=pl.ANY),
            scratch_shapes=[pltpu.SemaphoreType.DMA(())],
        ),
        out_shape=jax.ShapeDtypeStruct((nb * (n_pages * PAGE_SIZE + tail_rows), w), F32),
        compiler_params=_params("arbitrary"),
        name="nsa_page_gather",
    )(page_table, cache, tail)


def _row_slopes(slope_ref, g, n_rows, t):
    r_idx = lax.broadcasted_iota(jnp.int32, (n_rows, 1), 0) // t
    col = jnp.zeros((n_rows, 1), F32)
    for r in range(NSA_Q_PER_KV):
        col = jnp.where(r_idx == r, slope_ref[g * NSA_Q_PER_KV + r], col)
    return col


def _nsa_sample_select_kernel(t, pos0, nc, n_cmp, n_slc, slope_ref, q_ref, ck_ref, cv_ref, ov_ref,
                              ocmp_ref, idx_ref, valid_ref):
    g = pl.program_id(0)
    r8 = NSA_Q_PER_KV
    n_rows = r8 * t
    nsp = ov_ref.shape[1]
    q2 = (q_ref[...] * (NSA_HEAD_DIM ** -0.5)).astype(BF16)
    slope = _row_slopes(slope_ref, g, n_rows, t)
    tq = pos0 + lax.broadcasted_iota(jnp.int32, (n_rows, 1), 0) % t
    n_i = lax.broadcasted_iota(jnp.int32, (n_rows, nc), 1)
    dist_ci = tq - (CMP_STRIDE * n_i + (CMP_LEN - 1))
    vis_c = (dist_ci >= 0) & (n_i < n_cmp)
    s = lax.dot_general(q2, ck_ref[...], _NT, preferred_element_type=F32)
    s = jnp.where(vis_c, s - slope * dist_ci.astype(F32), NEG_INF)
    e = jnp.exp(s - jnp.max(s, axis=-1, keepdims=True))
    p = jnp.where(vis_c, e / jnp.sum(e, axis=-1, keepdims=True), 0.0)
    ocmp_ref[...] = jnp.dot(p.astype(BF16), cv_ref[...], preferred_element_type=F32)

    p_hi = p.astype(BF16)
    p_mid = (p - p_hi.astype(F32)).astype(BF16)
    p_lo = (p - p_hi.astype(F32) - p_mid.astype(F32)).astype(BF16)
    ov = ov_ref[...]
    imp = (jnp.dot(p_hi, ov, preferred_element_type=F32) + jnp.dot(p_mid, ov, preferred_element_type=F32)
           + jnp.dot(p_lo, ov, preferred_element_type=F32))
    sh = t
    while sh < n_rows:
        imp = imp + pltpu.roll(imp, sh, 0)
        sh *= 2
    imp = imp[0:8]
    j_i = lax.broadcasted_iota(jnp.int32, (8, nsp), 1)
    cur = (pos0 + lax.broadcasted_iota(jnp.int32, (8, 1), 0) % t) // SEL_BLOCK
    vis_j = j_i <= cur
    forced = (j_i == 0) | (j_i == cur) | (j_i == cur - 1)
    score = jnp.where(forced, FORCE_SCORE, jnp.where(vis_j, imp, -FORCE_SCORE))
    score = jnp.where(j_i < n_slc, score, -jnp.inf)
    rank = jnp.zeros((8, nsp), jnp.int32)
    for jp in range(n_slc):
        col = score[:, jp:jp + 1]
        ahead = (col > score) | ((col == score) & (jp < j_i))
        rank = rank + ahead.astype(jnp.int32)
    lane = lax.broadcasted_iota(jnp.int32, (8, LANE), 1)
    idx = jnp.zeros((8, LANE), F32)
    valid = jnp.zeros((8, LANE), F32)
    j_f = j_i.astype(F32)
    for k in range(min(N_SEL, n_slc)):
        hit = (rank == k) & (j_i < n_slc)
        idx_k = jnp.sum(jnp.where(hit, j_f, 0.0), axis=-1, keepdims=True)
        valid_k = jnp.sum(jnp.where(hit & vis_j, 1.0, 0.0), axis=-1, keepdims=True)
        idx = jnp.where(lane == k, idx_k, idx)
        valid = jnp.where(lane == k, valid_k, valid)
    idx_ref[...] = idx.astype(jnp.int32)
    valid_ref[...] = valid.astype(jnp.int32)


def nsa_sample_select(q, ckv, nb, t, pos0, n_cmp, n_slc):
    g4, hd = NSA_KV_HEADS, NSA_HEAD_DIM
    nc = ckv.shape[3]
    n_rows = NSA_Q_PER_KV * t
    nsp = -(-n_slc // LANE) * LANE
    slopes = jnp.exp2(-8.0 * (jnp.arange(NSA_HEADS, dtype=F32) + 1.0) / NSA_HEADS)
    ci = jnp.arange(nc)[:, None]
    sj = jnp.arange(nsp)[None, :]
    ov = ((CMP_STRIDE * ci < SEL_BLOCK * (sj + 1)) & (CMP_STRIDE * ci + CMP_LEN > SEL_BLOCK * sj)
          & (sj < n_slc) & (ci < n_cmp)).astype(BF16)
    return pl.pallas_call(
        functools.partial(_nsa_sample_select_kernel, t, pos0, nc, n_cmp, n_slc),
        grid=(g4, nb),
        in_specs=[
            pl.BlockSpec(memory_space=pltpu.SMEM),
            pl.BlockSpec((None, None, n_rows, hd), lambda g, b: (g, b, 0, 0)),
            pl.BlockSpec((None, None, None, nc, hd), lambda g, b: (b, 0, g, 0, 0)),
            pl.BlockSpec((None, None, None, nc, hd), lambda g, b: (b, 1, g, 0, 0)),
            pl.BlockSpec((nc, nsp), lambda g, b: (0, 0)),
        ],
        out_specs=[
            pl.BlockSpec((None, None, n_rows, hd), lambda g, b: (g, b, 0, 0)),
            pl.BlockSpec((None, None, 8, LANE), lambda g, b: (g, b, 0, 0)),
            pl.BlockSpec((None, None, 8, LANE), lambda g, b: (g, b, 0, 0)),
        ],
        out_shape=[jax.ShapeDtypeStruct((g4, nb, n_rows, hd), F32),
                   jax.ShapeDtypeStruct((g4, nb, 8, LANE), jnp.int32),
                   jax.ShapeDtypeStruct((g4, nb, 8, LANE), jnp.int32)],
        compiler_params=_params("parallel", "parallel"),
        name="nsa_sample_select",
    )(slopes, q, ckv, ckv, ov)


def _nsa_sample_attn_kernel(nb, t, pos0, n_pages, wb, idx_ref, valid_ref, pt_ref, slope_ref, q_ref, ocmp_ref,
                            gl_ref, gb_ref, kwin_ref, vwin_ref, kwt_ref, vwt_ref, cache_ref, tail_ref, o_ref,
                            kbuf, vbuf, q2_ref, osel_ref, sem):
    b = pl.program_id(0)
    g = pl.program_id(1)
    r8 = NSA_Q_PER_KV
    hd = NSA_HEAD_DIM
    n_rows = r8 * t
    n_sel = kbuf.shape[1]
    past_blocks = n_pages * (PAGE_SIZE // SEL_BLOCK)
    per_page = PAGE_SIZE // SEL_BLOCK

    def sel_entry(tt, k):
        return ((g * nb + b) * t + tt) * n_sel + k

    def block_copies(tt, k, in_tail):
        i = idx_ref[sel_entry(tt, k)]
        out = []
        for c, buf in ((0, kbuf), (1, vbuf)):
            col = c * NSA_KV_HEADS + g
            if in_tail:
                src = tail_ref.at[b, :, pl.ds(col * hd, hd)]
            else:
                ic = jnp.minimum(i, past_blocks - 1)
                page = pt_ref[b * n_pages + ic // per_page]
                src = cache_ref.at[page, pl.ds((ic % per_page) * SEL_BLOCK, SEL_BLOCK), pl.ds(col * hd, hd)]
            out.append(pltpu.make_async_copy(src, buf.at[tt, k], sem))
        return out

    def for_each_block(fn):
        for tt in range(t):
            for k in range(n_sel):
                in_tail = idx_ref[sel_entry(tt, k)] >= past_blocks

                @pl.when(in_tail)
                def _():
                    for cp in block_copies(tt, k, True):
                        fn(cp)

                @pl.when(jnp.logical_not(in_tail))
                def _():
                    for cp in block_copies(tt, k, False):
                        fn(cp)

    for_each_block(lambda cp: cp.start())

    q2_ref[...] = q_ref[...] * (hd ** -0.5)
    q2 = q2_ref[...].astype(BF16)
    slope = _row_slopes(slope_ref, g, n_rows, t)
    tq = pos0 + lax.broadcasted_iota(jnp.int32, (n_rows, 1), 0) % t

    kw = jnp.concatenate([kwin_ref[...], kwt_ref[...]], axis=0).astype(BF16)
    vw = jnp.concatenate([vwin_ref[...], vwt_ref[...]], axis=0).astype(BF16)
    nk = kw.shape[0]
    s_pos = (pos0 - wb) + lax.broadcasted_iota(jnp.int32, (n_rows, nk), 1)
    dist = tq - s_pos
    mask = (dist >= 0) & (dist <= WINDOW) & (s_pos >= 0)
    s = lax.dot_general(q2, kw, _NT, preferred_element_type=F32)
    s = jnp.where(mask, s - slope * dist.astype(F32), NEG_INF)
    e = jnp.exp(s - jnp.max(s, axis=-1, keepdims=True))
    p = e / jnp.sum(e, axis=-1, keepdims=True)
    o_win = jnp.dot(p.astype(BF16), vw, preferred_element_type=F32)

    for_each_block(lambda cp: cp.wait())

    nkeys = n_sel * SEL_BLOCK
    key_lane = lax.broadcasted_iota(jnp.int32, (1, nkeys), 1)
    slope8 = _row_slopes(slope_ref, g, r8, 1)
    for tt in range(t):
        blk = jnp.zeros((1, nkeys), jnp.int32)
        ok = jnp.zeros((1, nkeys), jnp.int32)
        for k in range(n_sel):
            here = key_lane // SEL_BLOCK == k
            blk = jnp.where(here, idx_ref[sel_entry(tt, k)], blk)
            ok = jnp.where(here, valid_ref[sel_entry(tt, k)], ok)
        dist = (pos0 + tt) - (blk * SEL_BLOCK + key_lane % SEL_BLOCK)
        mask = (dist >= 0) & (ok > 0)
        q_t = q2_ref[pl.ds(tt, r8, stride=t), :].astype(BF16)
        k_t = kbuf[tt].reshape(nkeys, hd).astype(BF16)
        v_t = vbuf[tt].reshape(nkeys, hd).astype(BF16)
        s = lax.dot_general(q_t, k_t, _NT, preferred_element_type=F32)
        s = jnp.where(mask, s - slope8 * dist.astype(F32), NEG_INF)
        e = jnp.exp(s - jnp.max(s, axis=-1, keepdims=True))
        p = e / jnp.sum(e, axis=-1, keepdims=True)
        osel_ref[pl.ds(tt, r8, stride=t), :] = jnp.dot(p.astype(BF16), v_t, preferred_element_type=F32)

    gates = jax.nn.sigmoid(gl_ref[...] + gb_ref[...])
    o = gates[:, 0:1] * ocmp_ref[...] + gates[:, 1:2] * osel_ref[...] + gates[:, 2:3] * o_win
    o_ref[...] = o.astype(BF16)


def nsa_sample_attention(q, o_cmp, gl, gb, idx, valid, page_table, cache_sel, tail_sel, cache_win, tail_win,
                         nb, t, pos0):
    g4, hd = NSA_KV_HEADS, NSA_HEAD_DIM
    n_rows = NSA_Q_PER_KV * t
    n_pages = page_table.shape[1]
    n_sel = idx.shape[-1]
    wb = cache_win.shape[1]
    slopes = jnp.exp2(-8.0 * (jnp.arange(NSA_HEADS, dtype=F32) + 1.0) / NSA_HEADS)
    row_spec = lambda w: pl.BlockSpec((None, None, n_rows, w), lambda b, g, *_: (g, b, 0, 0))
    win_spec = lambda rows, c: pl.BlockSpec((None, rows, hd), lambda b, g, *_: (b, 0, c * g4 + g))
    return pl.pallas_call(
        functools.partial(_nsa_sample_attn_kernel, nb, t, pos0, n_pages, wb),
        grid_spec=pltpu.PrefetchScalarGridSpec(
            num_scalar_prefetch=3,
            grid=(nb, g4),
            in_specs=[
                pl.BlockSpec(memory_space=pltpu.SMEM),
                row_spec(hd), row_spec(hd), row_spec(3),
                pl.BlockSpec((None, n_rows, 3), lambda b, g, *_: (g, 0, 0)),
                win_spec(wb, 0), win_spec(wb, 1), win_spec(LANE, 0), win_spec(LANE, 1),
                pl.BlockSpec(memory_space=pl.ANY), pl.BlockSpec(memory_space=pl.ANY),
            ],
            out_specs=row_spec(hd),
            scratch_shapes=[pltpu.VMEM((t, n_sel, SEL_BLOCK, hd), F32), pltpu.VMEM((t, n_sel, SEL_BLOCK, hd), F32),
                            pltpu.VMEM((n_rows, hd), F32), pltpu.VMEM((n_rows, hd), F32),
                            pltpu.SemaphoreType.DMA(())],
        ),
        out_shape=jax.ShapeDtypeStruct((g4, nb, n_rows, hd), BF16),
        compiler_params=_params("arbitrary", "arbitrary"),
        name="nsa_sample_attn",
    )(idx.reshape(-1), valid.reshape(-1), page_table.reshape(-1), slopes, q, o_cmp, gl, gb,
      cache_win, cache_win, tail_win, tail_win, cache_sel, tail_sel)


def nsa_sample_mix(z, nb, t, cache_cmp, cache_sel, cache_win, page_table, gate_bias, w_cmp, pe_cmp):
    g4, r8, hd = NSA_KV_HEADS, NSA_Q_PER_KV, NSA_HEAD_DIM
    n_pages = page_table.shape[1]
    past = n_pages * PAGE_SIZE
    n_pool = cache_cmp.shape[0]
    wb = cache_win.shape[1]
    t_all = past + t
    tp = -(-t_all // SEL_BLOCK) * SEL_BLOCK
    n_slc = tp // SEL_BLOCK
    n_cmp = tp // CMP_STRIDE - 1
    nc = -(-(tp // CMP_STRIDE) // 16) * 16
    kv = z[:, NSA_QD:NSA_QD + 6 * NSA_KVD].reshape(nb, t, 3, 2 * NSA_KVD)

    tail_cmp = jnp.pad(kv[:, :, 0], ((0, 0), (0, nc * CMP_STRIDE - past - t), (0, 0)))
    rows = page_gather(cache_cmp.reshape(n_pool, PAGE_SIZE, 2 * NSA_KVD), page_table, tail_cmp)
    ckv = compress(rows, nb, nc, 0, w_cmp, pe_cmp)

    q = z[:, :NSA_QD].reshape(nb, t, g4, r8, hd).transpose(2, 0, 3, 1, 4).reshape(g4, nb, r8 * t, hd)
    o_cmp, idx, valid = nsa_sample_select(q, ckv, nb, t, past, n_cmp, n_slc)
    n_sel = min(N_SEL, n_slc)
    idx = idx[:, :, :t, :n_sel]
    valid = valid[:, :, :t, :n_sel]

    gl = z[:, NSA_QD + 6 * NSA_KVD:NSA_QD + 6 * NSA_KVD + 3 * NSA_HEADS].reshape(nb, t, 3, g4, r8)
    gl = gl.transpose(3, 0, 4, 1, 2).reshape(g4, nb, r8 * t, 3)
    gb = jnp.broadcast_to(gate_bias.reshape(3, g4, r8, 1), (3, g4, r8, t)).transpose(1, 2, 3, 0).reshape(g4, r8 * t, 3)
    tail_sel = jnp.pad(kv[:, :, 1], ((0, 0), (0, SEL_BLOCK - t), (0, 0)))
    tail_win = jnp.pad(kv[:, :, 2], ((0, 0), (0, LANE - t), (0, 0)))
    o = nsa_sample_attention(q, o_cmp, gl, gb, idx, valid, page_table,
                             cache_sel.reshape(n_pool, PAGE_SIZE, 2 * NSA_KVD), tail_sel,
                             cache_win.reshape(nb, wb, 2 * NSA_KVD), tail_win, nb, t, past)
    o = o.reshape(g4, nb, r8, t, hd).transpose(1, 3, 0, 2, 4).reshape(nb * t, NSA_QD)
    kv6 = kv.reshape(nb, t, 3, 2, g4, hd)
    win_new = jnp.concatenate([cache_win, kv6[:, :, 2]], axis=1)[:, t:]
    return o, kv6[:, :, 0], kv6[:, :, 1], win_new


GLA_QK = GLA_HEADS * GLA_DK
GLA_VD = GLA_HEADS * GLA_DV
GLA_SUPER = 256


def _chunk_scan(la, c):
    n = la.shape[0]
    ri = lax.broadcasted_iota(jnp.int32, la.shape, 0) % c
    b = la
    s = 1
    while s < c:
        b = b + jnp.where(ri >= s, pltpu.roll(b, s, 0), 0.0)
        s *= 2
    tot = jnp.where(ri == c - 1, b, 0.0)
    s = 1
    while s < c:
        tot = tot + jnp.where(ri + s < c, pltpu.roll(tot, n - s, 0), 0.0)
        s *= 2
    return b, tot


def _gla_gates(a_rows, wa_ref, ba_ref):
    x = jnp.dot(a_rows.astype(BF16), wa_ref[...], preferred_element_type=F32) + ba_ref[...]
    return jax.nn.log_sigmoid(x) / GLA_TAU


def _gla_finish(o, r, g_ref):
    return (_norm_rows(o, g_ref[...]) * (r * jax.nn.sigmoid(r))).astype(BF16)


def _gla_prompt_kernel(t, q_ref, k_ref, v_ref, r_ref, a_ref, wa_ref, ba_ref, g_ref, o_ref, s_ref, st_ref, oc_ref):
    c = GLA_CHUNK
    sc = GLA_SUPER
    st_ref[...] = jnp.zeros(st_ref.shape, F32)
    ti = lax.broadcasted_iota(jnp.int32, (sc, sc), 0)
    si = lax.broadcasted_iota(jnp.int32, (sc, sc), 1)
    causal = (ti // c == si // c) & (si <= ti)

    def body(i, carry):
        r0 = pl.multiple_of(i * sc, sc)
        la = _gla_gates(a_ref[pl.ds(r0, sc), :], wa_ref, ba_ref)
        b, b_last = _chunk_scan(la, c)
        q = q_ref[pl.ds(r0, sc), :] * (GLA_DK ** -0.5)
        k = k_ref[pl.ds(r0, sc), :]
        qe = (q * jnp.exp(b)).astype(BF16)
        ke = (k * jnp.exp(-b)).astype(BF16)
        kd = (k * jnp.exp(b_last - b)).astype(BF16)
        vb = v_ref[pl.ds(r0, sc), :].astype(BF16)
        att = jnp.where(causal, lax.dot_general(qe, ke, _NT, preferred_element_type=F32), 0.0)
        o_intra = jnp.dot(att.astype(BF16), vb, preferred_element_type=F32)
        for j in range(sc // c):
            rows = slice(j * c, (j + 1) * c)
            st = st_ref[...]
            o_inter = lax.dot_general(qe[rows], st.astype(BF16), _NT, preferred_element_type=F32)
            oc_ref[rows, :] = o_intra[rows] + o_inter
            dec = jnp.exp(b_last[j * c:j * c + 1, :])
            st_ref[...] = dec * st + lax.dot_general(vb[rows], kd[rows], _TN, preferred_element_type=F32)
        o_ref[pl.ds(r0, sc), :] = _gla_finish(oc_ref[...], r_ref[pl.ds(r0, sc), :], g_ref)
        return carry

    lax.fori_loop(0, t // sc, body, 0)
    s_ref[...] = st_ref[...].T


def gla_prompt(z, b, t, w_alpha, b_alpha, norm_g):
    dk, dv, h = GLA_DK, GLA_DV, GLA_HEADS
    wa = jnp.pad(w_alpha, ((0, LANE - GLA_RANK), (0, 0))).astype(BF16)
    return pl.pallas_call(
        functools.partial(_gla_prompt_kernel, t),
        grid=(b, h),
        in_specs=[
            pl.BlockSpec((t, dk), lambda bi, hi: (bi, hi)),
            pl.BlockSpec((t, dk), lambda bi, hi: (bi, h + hi)),
            pl.BlockSpec((t, dv), lambda bi, hi: (bi, 2 * GLA_QK // dv + hi)),
            pl.BlockSpec((t, dv), lambda bi, hi: (bi, (2 * GLA_QK + GLA_VD) // dv + hi)),
            pl.BlockSpec((t, LANE), lambda bi, hi: (bi, (2 * GLA_QK + 2 * GLA_VD) // LANE)),
            pl.BlockSpec((LANE, dk), lambda bi, hi: (0, hi)),
            pl.BlockSpec((1, dk), lambda bi, hi: (0, hi)),
            pl.BlockSpec((1, dv), lambda bi, hi: (0, 0)),
        ],
        out_specs=[
            pl.BlockSpec((t, dv), lambda bi, hi: (bi, hi)),
            pl.BlockSpec((None, None, dk, dv), lambda bi, hi: (bi, hi, 0, 0)),
        ],
        out_shape=[jax.ShapeDtypeStruct((b * t, GLA_VD), BF16), jax.ShapeDtypeStruct((b, h, dk, dv), F32)],
        scratch_shapes=[pltpu.VMEM((dv, dk), F32), pltpu.VMEM((GLA_SUPER, dv), F32)],
        compiler_params=_params("parallel", "arbitrary"),
        name="gla_prompt",
    )(z, z, z, z, z, wa, b_alpha.reshape(1, -1), norm_g.reshape(1, -1))


def _gla_sample_kernel(nb, t, q_ref, k_ref, v_ref, r_ref, a_ref, wa_ref, ba_ref, g_ref, s0_ref, o_ref, s_ref):
    n = nb * t
    la = _gla_gates(a_ref[...], wa_ref, ba_ref)
    b, b_last = _chunk_scan(la, t)
    q = q_ref[...] * (GLA_DK ** -0.5)
    k = k_ref[...]
    qe = q * jnp.exp(b)
    ke = (k * jnp.exp(-b)).astype(BF16)
    kd = k * jnp.exp(b_last - b)
    vb = v_ref[...].astype(BF16)
    ti = lax.broadcasted_iota(jnp.int32, (n, n), 0)
    si = lax.broadcasted_iota(jnp.int32, (n, n), 1)
    causal = (ti // t == si // t) & (si <= ti)
    att = jnp.where(causal, lax.dot_general(qe.astype(BF16), ke, _NT, preferred_element_type=F32), 0.0)
    o = jnp.dot(att.astype(BF16), vb, preferred_element_type=F32)
    row = lax.broadcasted_iota(jnp.int32, (n, 1), 0) // t
    for i in range(nb):
        mine = row == i
        st = s0_ref[i].T
        qe_i = jnp.where(mine, qe, 0.0).astype(BF16)
        kd_i = jnp.where(mine, kd, 0.0).astype(BF16)
        o = o + lax.dot_general(qe_i, st.astype(BF16), _NT, preferred_element_type=F32)
        dec = jnp.exp(b_last[i * t:i * t + 1, :])
        s_ref[i] = (dec * st + lax.dot_general(vb, kd_i, _TN, preferred_element_type=F32)).T
    o_ref[...] = _gla_finish(o, r_ref[...], g_ref)


def gla_sample(z, nb, t, s0, w_alpha, b_alpha, norm_g):
    dk, dv, h = GLA_DK, GLA_DV, GLA_HEADS
    n = nb * t
    wa = jnp.pad(w_alpha, ((0, LANE - GLA_RANK), (0, 0))).astype(BF16)
    return pl.pallas_call(
        functools.partial(_gla_sample_kernel, nb, t),
        grid=(h,),
        in_specs=[
            pl.BlockSpec((n, dk), lambda hi: (0, hi)),
            pl.BlockSpec((n, dk), lambda hi: (0, h + hi)),
            pl.BlockSpec((n, dv), lambda hi: (0, 2 * GLA_QK // dv + hi)),
            pl.BlockSpec((n, dv), lambda hi: (0, (2 * GLA_QK + GLA_VD) // dv + hi)),
            pl.BlockSpec((n, LANE), lambda hi: (0, (2 * GLA_QK + 2 * GLA_VD) // LANE)),
            pl.BlockSpec((LANE, dk), lambda hi: (0, hi)),
            pl.BlockSpec((1, dk), lambda hi: (0, hi)),
            pl.BlockSpec((1, dv), lambda hi: (0, 0)),
            pl.BlockSpec((nb, None, dk, dv), lambda hi: (0, hi, 0, 0)),
        ],
        out_specs=[
            pl.BlockSpec((n, dv), lambda hi: (0, hi)),
            pl.BlockSpec((nb, None, dk, dv), lambda hi: (0, hi, 0, 0)),
        ],
        out_shape=[jax.ShapeDtypeStruct((n, GLA_VD), BF16), jax.ShapeDtypeStruct((nb, h, dk, dv), F32)],
        compiler_params=_params("parallel"),
        name="gla_sample",
    )(z, z, z, z, z, wa, b_alpha.reshape(1, -1), norm_g.reshape(1, -1), s0)


def nsa_prompt_mix(z, b, t, gate_bias, w_cmp, pe_cmp):
    ckv = compress(z, b, t // CMP_STRIDE, NSA_QD // NSA_HEAD_DIM, w_cmp, pe_cmp)
    o = nsa_prompt_attention(z, ckv, gate_bias, b, t)
    kv = z[:, NSA_QD:NSA_QD + 6 * NSA_KVD].reshape(b, t, 3, 2, NSA_KV_HEADS, NSA_HEAD_DIM)
    return o, kv[:, :, 0], kv[:, :, 1], kv[:, t - min(WINDOW, t):, 2]


def _pad_cols(w, mult):
    n = w.shape[-1]
    return jnp.pad(w, ((0, 0), (0, -(-n // mult) * mult - n)))


def kernel(x_prompt, x_sample, p_prompt, p_sample, cache_cmp_kv, cache_sel_kv, cache_win_kv, state_gla, page_table, ffn1_norm, ffn1_w_gu, ffn1_w_down, mix_norm, nsa_w_in, nsa_gate_bias, nsa_w_cmp, nsa_pe_cmp, nsa_w_out, gla_w_in, gla_w_alpha, gla_b_alpha, gla_norm, gla_w_out, ffn2_norm, ffn2_w_gu, ffn2_w_down, ple_norm, ple_w_gate, ple_w_proj, final_norm):
    bp, tp, d = x_prompt.shape
    bs, ts, _ = x_sample.shape
    mp, ms = bp * tp, bs * ts
    xp = x_prompt.reshape(mp, d)
    xs = x_sample.reshape(ms, d)
    tm_p = _row_tile(mp, 512)
    tm_p2 = _row_tile(mp, 512)

    cmp_p, sel_p, win_p, gla_p = [], [], [], []
    cmp_s, sel_s, win_s, gla_s = [], [], [], []
    for i in range(DEPTH):
        w_gu1 = ffn1_w_gu[i].astype(BF16)
        w_d1 = ffn1_w_down[i].astype(BF16)
        xp = ffn(xp, ffn1_norm[i], w_gu1, w_d1, tm_p)
        xs = ffn(xs, ffn1_norm[i], w_gu1, w_d1, ms)
        j = i // 2
        if i % 2 == 0:
            w_in = _pad_cols(nsa_w_in[j], 512).astype(BF16)
            w_out = nsa_w_out[j].astype(BF16)
            zp = norm_matmul(xp, mix_norm[i], w_in, tm_p2)
            zs = norm_matmul(xs, mix_norm[i], w_in, ms)
            op, kc, kl, kw = nsa_prompt_mix(zp, bp, tp, nsa_gate_bias[j], nsa_w_cmp[j], nsa_pe_cmp[j])
            os_, kc2, kl2, kw2 = nsa_sample_mix(zs, bs, ts, cache_cmp_kv[j], cache_sel_kv[j], cache_win_kv[j],
                                                page_table, nsa_gate_bias[j], nsa_w_cmp[j], nsa_pe_cmp[j])
            cmp_p.append(kc)
            sel_p.append(kl)
            win_p.append(kw)
            cmp_s.append(kc2)
            sel_s.append(kl2)
            win_s.append(kw2)
        else:
            w_in = _pad_cols(gla_w_in[j], 512).astype(BF16)
            w_out = gla_w_out[j].astype(BF16)
            zp = norm_matmul(xp, mix_norm[i], w_in, tm_p2)
            zs = norm_matmul(xs, mix_norm[i], w_in, ms)
            op, sp = gla_prompt(zp, bp, tp, gla_w_alpha[j], gla_b_alpha[j], gla_norm[j])
            os_, ss = gla_sample(zs, bs, ts, state_gla[j], gla_w_alpha[j], gla_b_alpha[j], gla_norm[j])
            gla_p.append(sp)
            gla_s.append(ss)
        xp = matmul_residual(op, w_out, xp, tm_p)
        xs = matmul_residual(os_, w_out, xs, ms)
        w_gu2 = ffn2_w_gu[i].astype(BF16)
        w_d2 = ffn2_w_down[i].astype(BF16)
        xp = ffn(xp, ffn2_norm[i], w_gu2, w_d2, tm_p)
        xs = ffn(xs, ffn2_norm[i], w_gu2, w_d2, ms)
        w_pg = ple_w_gate[i].astype(BF16)
        w_pp = ple_w_proj[i].astype(BF16)
        xp = ple(xp, p_prompt[i].reshape(mp, -1), ple_norm[i], w_pg, w_pp, tm_p2)
        xs = ple(xs, p_sample[i].reshape(ms, -1), ple_norm[i], w_pg, w_pp, ms)
    y_prompt = rmsnorm(xp, final_norm, tm_p2).reshape(bp, tp, d)
    y_sample = rmsnorm(xs, final_norm, ms).reshape(bs, ts, d)
    return (y_prompt, y_sample, jnp.stack(cmp_p), jnp.stack(sel_p), jnp.stack(win_p), jnp.stack(gla_p),
            jnp.stack(cmp_s), jnp.stack(sel_s), jnp.stack(win_s), jnp.stack(gla_s))
```

```python
import functools
import math

import jax
import jax.numpy as jnp
from jax import lax
from jax.experimental import pallas as pl
from jax.experimental.pallas import tpu as pltpu

F32 = jnp.float32
BF16 = jnp.bfloat16

D_MODEL = 4096
DEPTH = 2
PAST_LEN = 16384
PAGE_SIZE = 128
NSA_HEADS = 32
NSA_HEAD_DIM = 128
NSA_KV_HEADS = 4
NSA_Q_PER_KV = 8
CMP_STRIDE = 16
CMP_LEN = 32
SEL_BLOCK = 64
N_SEL = 16
WINDOW = 512
NSA_QBLOCK = 32
NSA_QD = NSA_HEADS * NSA_HEAD_DIM
NSA_KVD = NSA_KV_HEADS * NSA_HEAD_DIM
GLA_HEADS = 8
GLA_DK = 256
GLA_DV = 512
GLA_RANK = 16
GLA_TAU = 16.0
GLA_CHUNK = 32
D_FF = 11008
EPS = 1e-6
NEG_INF = -1e30
FORCE_SCORE = 1e30

V7X_VMEM_LIMIT_BYTES = 56 * 1024 * 1024
LANE = 128


def _params(*sem):
    return pltpu.CompilerParams(dimension_semantics=sem, vmem_limit_bytes=V7X_VMEM_LIMIT_BYTES)


def _norm_rows(x, g):
    ms = jnp.mean(x * x, axis=-1, keepdims=True)
    return (x * lax.rsqrt(ms + EPS)) * g


def _row_tile(m, want):
    return want if m % want == 0 else m


def _ffn_kernel(x_ref, g_ref, wa_ref, wu_ref, wd_ref, o_ref, h_ref):
    @pl.when(pl.program_id(1) == 0)
    def _():
        x = x_ref[...]
        h_ref[...] = _norm_rows(x, g_ref[...]).astype(BF16)
        o_ref[...] = x

    h = h_ref[...]
    a = jnp.dot(h, wa_ref[...], preferred_element_type=F32)
    u = jnp.dot(h, wu_ref[...], preferred_element_type=F32)
    act = (0.5 * (a * jax.nn.sigmoid(a)) * u).astype(BF16)
    o_ref[...] += jnp.dot(act, wd_ref[...], preferred_element_type=F32)


def ffn(x, g, w_gu, w_down, tm, tf=256):
    m, d = x.shape
    nf = D_FF // tf
    return pl.pallas_call(
        _ffn_kernel,
        grid=(m // tm, nf),
        in_specs=[
            pl.BlockSpec((tm, d), lambda i, j: (i, 0), pipeline_mode=pl.Buffered(1)),
            pl.BlockSpec((1, d), lambda i, j: (0, 0)),
            pl.BlockSpec((d, tf), lambda i, j: (0, j)),
            pl.BlockSpec((d, tf), lambda i, j: (0, j + nf)),
            pl.BlockSpec((tf, d), lambda i, j: (j, 0)),
        ],
        out_specs=pl.BlockSpec((tm, d), lambda i, j: (i, 0)),
        out_shape=jax.ShapeDtypeStruct((m, d), F32),
        scratch_shapes=[pltpu.VMEM((tm, d), BF16)],
        compiler_params=_params("parallel", "arbitrary"),
    )(x, g.reshape(1, d), w_gu, w_gu, w_down)


def _norm_matmul_kernel(x_ref, g_ref, w_ref, o_ref, h_ref):
    @pl.when(pl.program_id(1) == 0)
    def _():
        h_ref[...] = _norm_rows(x_ref[...], g_ref[...]).astype(BF16)

    o_ref[...] = jnp.dot(h_ref[...], w_ref[...], preferred_element_type=F32)


def norm_matmul(x, g, w, tm, tn=512):
    m, d = x.shape
    n = w.shape[1]
    return pl.pallas_call(
        _norm_matmul_kernel,
        grid=(m // tm, n // tn),
        in_specs=[
            pl.BlockSpec((tm, d), lambda i, j: (i, 0)),
            pl.BlockSpec((1, d), lambda i, j: (0, 0)),
            pl.BlockSpec((d, tn), lambda i, j: (0, j)),
        ],
        out_specs=pl.BlockSpec((tm, tn), lambda i, j: (i, j)),
        out_shape=jax.ShapeDtypeStruct((m, n), F32),
        scratch_shapes=[pltpu.VMEM((tm, d), BF16)],
        compiler_params=_params("parallel", "arbitrary"),
    )(x, g.reshape(1, d), w)


def _nsa_in_proj_kernel(tn, x_ref, g_ref, w_ref, o_ref, kv_ref, h_ref):
    j = pl.program_id(1)

    @pl.when(j == 0)
    def _():
        h_ref[...] = _norm_rows(x_ref[...], g_ref[...]).astype(BF16)

    res = jnp.dot(h_ref[...], w_ref[...], preferred_element_type=F32)
    o_ref[...] = res

    @pl.when((j >= NSA_QD // tn) & (j < (NSA_QD + 6 * NSA_KVD) // tn))
    def _():
        for g in range(NSA_KV_HEADS):
            kv_ref[:, g, :] = res[:, g * NSA_HEAD_DIM:(g + 1) * NSA_HEAD_DIM]


def nsa_in_proj(x, g, w, tm):
    m, d = x.shape
    n = w.shape[1]
    tn = NSA_KVD
    q_tiles = NSA_QD // tn

    def kv_index(i, j):
        t = jnp.clip(j - q_tiles, 0, 5)
        return (t // 2, i, t % 2, 0, 0)

    return pl.pallas_call(
        functools.partial(_nsa_in_proj_kernel, tn),
        grid=(m // tm, n // tn),
        in_specs=[
            pl.BlockSpec((tm, d), lambda i, j: (i, 0)),
            pl.BlockSpec((1, d), lambda i, j: (0, 0)),
            pl.BlockSpec((d, tn), lambda i, j: (0, j)),
        ],
        out_specs=[
            pl.BlockSpec((tm, tn), lambda i, j: (i, j)),
            pl.BlockSpec((None, tm, None, NSA_KV_HEADS, NSA_HEAD_DIM), kv_index),
        ],
        out_shape=[jax.ShapeDtypeStruct((m, n), F32),
                   jax.ShapeDtypeStruct((3, m, 2, NSA_KV_HEADS, NSA_HEAD_DIM), F32)],
        scratch_shapes=[pltpu.VMEM((tm, d), BF16)],
        compiler_params=_params("parallel", "arbitrary"),
        name="nsa_in_proj",
    )(x, g.reshape(1, d), w)


def _matmul_residual_kernel(a_ref, w_ref, x_ref, o_ref):
    o_ref[...] = x_ref[...] + jnp.dot(a_ref[...], w_ref[...], preferred_element_type=F32)


def matmul_residual(a, w, x, tm, tn=512):
    m, k = a.shape
    n = w.shape[1]
    return pl.pallas_call(
        _matmul_residual_kernel,
        grid=(m // tm, n // tn),
        in_specs=[
            pl.BlockSpec((tm, k), lambda i, j: (i, 0)),
            pl.BlockSpec((k, tn), lambda i, j: (0, j)),
            pl.BlockSpec((tm, tn), lambda i, j: (i, j)),
        ],
        out_specs=pl.BlockSpec((tm, tn), lambda i, j: (i, j)),
        out_shape=jax.ShapeDtypeStruct((m, n), F32),
        compiler_params=_params("parallel", "arbitrary"),
    )(a, w, x)


def _ple_kernel(tn, x_ref, g_ref, wg_ref, p_ref, wp_ref, o_ref, h_ref):
    j = pl.program_id(1)

    @pl.when(j == 0)
    def _():
        h_ref[...] = _norm_rows(x_ref[...], g_ref[...]).astype(BF16)

    gate = jax.nn.sigmoid(jnp.dot(h_ref[...], wg_ref[...], preferred_element_type=F32))
    proj = jnp.dot(p_ref[...].astype(BF16), wp_ref[...], preferred_element_type=F32)
    col = pl.multiple_of(j * tn, tn)
    o_ref[...] = x_ref[:, pl.ds(col, tn)] + gate * proj


def ple(x, p, g, w_gate, w_proj, tm, tn=512):
    m, d = x.shape
    pd = p.shape[1]
    return pl.pallas_call(
        functools.partial(_ple_kernel, tn),
        grid=(m // tm, d // tn),
        in_specs=[
            pl.BlockSpec((tm, d), lambda i, j: (i, 0)),
            pl.BlockSpec((1, d), lambda i, j: (0, 0)),
            pl.BlockSpec((d, tn), lambda i, j: (0, j)),
            pl.BlockSpec((tm, pd), lambda i, j: (i, 0)),
            pl.BlockSpec((pd, tn), lambda i, j: (0, j)),
        ],
        out_specs=pl.BlockSpec((tm, tn), lambda i, j: (i, j)),
        out_shape=jax.ShapeDtypeStruct((m, d), F32),
        scratch_shapes=[pltpu.VMEM((tm, d), BF16)],
        compiler_params=_params("parallel", "arbitrary"),
    )(x, g.reshape(1, d), w_gate, p, w_proj)


def _rmsnorm_kernel(x_ref, g_ref, o_ref):
    o_ref[...] = _norm_rows(x_ref[...], g_ref[...])


def rmsnorm(x, g, tm):
    m, d = x.shape
    return pl.pallas_call(
        _rmsnorm_kernel,
        grid=(m // tm,),
        in_specs=[pl.BlockSpec((tm, d), lambda i: (i, 0)), pl.BlockSpec((1, d), lambda i: (0, 0))],
        out_specs=pl.BlockSpec((tm, d), lambda i: (i, 0)),
        out_shape=jax.ShapeDtypeStruct((m, d), F32),
        compiler_params=_params("parallel"),
    )(x, g.reshape(1, d))


Z_GATE_COL = (NSA_QD + 6 * NSA_KVD) // LANE


def _compress_kernel(nc, x_ref, w2_ref, wf_ref, pe_ref, o_ref):
    acc = jnp.zeros((nc, 2 * NSA_HEAD_DIM), F32)
    for l in range(CMP_STRIDE):
        a = x_ref[pl.ds(l, nc, stride=CMP_STRIDE), :]
        acc += jnp.dot(a.astype(BF16), w2_ref[l], preferred_element_type=F32)
    n_idx = lax.broadcasted_iota(jnp.int32, (nc, NSA_HEAD_DIM), 0)
    first = acc[:, :NSA_HEAD_DIM]
    second = jnp.where(n_idx == nc - 1, 0.0, pltpu.roll(acc[:, NSA_HEAD_DIM:], nc - 1, 0))
    bias = jnp.dot(pe_ref[...], wf_ref[...], preferred_element_type=F32)[0:1]
    o_ref[...] = (first + second + bias).astype(BF16)


def compress(x, nb, nc, col0, w_cmp, pe_cmp):
    hd = NSA_HEAD_DIM
    w2 = w_cmp.reshape(2, 2, CMP_STRIDE, hd, hd).transpose(0, 2, 3, 1, 4).reshape(2, CMP_STRIDE, hd, 2 * hd)
    wf = w_cmp.reshape(2, CMP_LEN * hd, hd)
    pe = jnp.broadcast_to(pe_cmp.reshape(2, 1, CMP_LEN * hd), (2, 8, CMP_LEN * hd))
    return pl.pallas_call(
        functools.partial(_compress_kernel, nc),
        grid=(nb, 2, NSA_KV_HEADS),
        in_specs=[
            pl.BlockSpec((CMP_STRIDE * nc, hd), lambda i, c, g: (i, col0 + c * NSA_KV_HEADS + g)),
            pl.BlockSpec((None, CMP_STRIDE, hd, 2 * hd), lambda i, c, g: (c, 0, 0, 0)),
            pl.BlockSpec((None, CMP_LEN * hd, hd), lambda i, c, g: (c, 0, 0)),
            pl.BlockSpec((None, 8, CMP_LEN * hd), lambda i, c, g: (c, 0, 0)),
        ],
        out_specs=pl.BlockSpec((None, None, None, nc, hd), lambda i, c, g: (i, c, g, 0, 0)),
        out_shape=jax.ShapeDtypeStruct((nb, 2, NSA_KV_HEADS, nc, hd), BF16),
        compiler_params=_params("parallel", "parallel", "parallel"),
        name="nsa_compress",
    )(x, w2.astype(BF16), wf.astype(BF16), pe.astype(BF16))


def _nsa_prompt_kernel(t, qb, kb, nc, slope_ref, zq_ref, ks_ref, vs_ref, kw_ref, vw_ref, gz_ref, gb_ref,
                       ck_ref, cv_ref, ov_ref, o_ref, m_ref, l_ref, acc_ref, p_ref):
    g = pl.program_id(1)
    qi = pl.program_id(2)
    t0 = qi * qb
    r8 = NSA_Q_PER_KV
    hd = NSA_HEAD_DIM
    n_slc = t // SEL_BLOCK
    n_cmp = nc - 1

    zq = zq_ref[...] * (hd ** -0.5)
    q2 = jnp.concatenate([zq[:, r * hd:(r + 1) * hd] for r in range(r8)], axis=0).astype(BF16)

    slope = [slope_ref[g * r8 + r] for r in range(r8)]
    tq = lax.broadcasted_iota(jnp.int32, (qb, 1), 0) + t0
    rows = lambda r: slice(r * qb, (r + 1) * qb)

    n_i = lax.broadcasted_iota(jnp.int32, (qb, nc), 1)
    dist_ci = tq - (CMP_STRIDE * n_i + (CMP_LEN - 1))
    vis_c = (dist_ci >= 0) & (n_i < n_cmp)
    dist_c = dist_ci.astype(F32)
    s_all = lax.dot_general(q2, ck_ref[...], (((1,), (1,)), ((), ())), preferred_element_type=F32)
    psum = jnp.zeros((qb, nc), F32)
    for r in range(r8):
        s = jnp.where(vis_c, s_all[rows(r)] - slope[r] * dist_c, NEG_INF)
        e = jnp.exp(s - jnp.max(s, axis=-1, keepdims=True))
        p = jnp.where(vis_c, e / jnp.sum(e, axis=-1, keepdims=True), 0.0)
        psum = psum + p
        p_ref[rows(r), :nc] = p.astype(BF16)
    o_cmp = jnp.dot(p_ref[:, :nc], cv_ref[...], preferred_element_type=F32)

    p_hi = psum.astype(BF16)
    p_mid = (psum - p_hi.astype(F32)).astype(BF16)
    p_lo = (psum - p_hi.astype(F32) - p_mid.astype(F32)).astype(BF16)
    ov = ov_ref[...]
    imp = (jnp.dot(p_hi, ov, preferred_element_type=F32) + jnp.dot(p_mid, ov, preferred_element_type=F32)
           + jnp.dot(p_lo, ov, preferred_element_type=F32))
    j_i = lax.broadcasted_iota(jnp.int32, (qb, LANE), 1)
    cur = tq // SEL_BLOCK
    vis_j = j_i <= cur
    forced = (j_i == 0) | (j_i == cur) | (j_i == cur - 1)
    score = jnp.where(forced, FORCE_SCORE, jnp.where(vis_j, imp, -FORCE_SCORE))
    score = jnp.where(j_i < n_slc, score, -jnp.inf)
    rank = jnp.zeros((qb, LANE), jnp.int32)
    for jp in range(n_slc):
        col = score[:, jp:jp + 1]
        ahead = (col > score) | ((col == score) & (jp < j_i))
        rank = rank + ahead.astype(jnp.int32)
    selected = ((rank < min(N_SEL, n_slc)) & vis_j).astype(BF16)

    d0 = (lax.broadcasted_iota(jnp.int32, (qb, kb), 0) - lax.broadcasted_iota(jnp.int32, (qb, kb), 1))

    def flash_init():
        m_ref[...] = jnp.full(m_ref.shape, NEG_INF, F32)
        l_ref[...] = jnp.zeros(l_ref.shape, F32)
        acc_ref[...] = jnp.zeros(acc_ref.shape, F32)

    def flash_step(k_ref, v_ref, k0, mask, dist):
        k = k_ref[pl.ds(k0, kb), :].astype(BF16)
        v = v_ref[pl.ds(k0, kb), :].astype(BF16)
        s_all = lax.dot_general(q2, k, (((1,), (1,)), ((), ())), preferred_element_type=F32)
        alphas = []
        for r in range(r8):
            s = jnp.where(mask, s_all[rows(r)] - slope[r] * dist, NEG_INF)
            m_prev = m_ref[r]
            m_next = jnp.maximum(m_prev, jnp.max(s, axis=-1, keepdims=True))
            alpha = jnp.exp(m_prev - m_next)
            p = jnp.exp(s - jnp.concatenate([m_next] * (kb // LANE), axis=1))
            l_ref[r] = alpha * l_ref[r] + jnp.sum(p, axis=-1, keepdims=True)
            m_ref[r] = m_next
            p_ref[rows(r), :kb] = p.astype(BF16)
            alphas.append(alpha)
        pv = jnp.dot(p_ref[:, :kb], v, preferred_element_type=F32)
        for r in range(r8):
            acc_ref[r] = alphas[r] * acc_ref[r] + pv[rows(r)]

    flash_init()

    def sel_body(i, carry):
        k0 = pl.multiple_of(i * kb, kb)
        dist = d0 + (t0 - k0)
        blk_of_key = (lax.broadcasted_iota(jnp.int32, (LANE, kb), 1) + k0) // SEL_BLOCK
        expand = (blk_of_key == lax.broadcasted_iota(jnp.int32, (LANE, kb), 0)).astype(BF16)
        chosen = jnp.dot(selected, expand, preferred_element_type=F32)
        flash_step(ks_ref, vs_ref, k0, (chosen > 0.5) & (dist >= 0), dist.astype(F32))
        return carry

    lax.fori_loop(0, (t0 + qb - 1) // kb + 1, sel_body, 0)
    o_sel = [acc_ref[r] / l_ref[r] for r in range(r8)]

    flash_init()

    def win_body(i, carry):
        k0 = pl.multiple_of(i * kb, kb)
        dist = d0 + (t0 - k0)
        flash_step(kw_ref, vw_ref, k0, (dist >= 0) & (dist <= WINDOW), dist.astype(F32))
        return carry

    lax.fori_loop(jnp.maximum(t0 - WINDOW, 0) // kb, (t0 + qb - 1) // kb + 1, win_body, 0)
    o_win = [acc_ref[r] / l_ref[r] for r in range(r8)]

    gates = jax.nn.sigmoid(gz_ref[...] + gb_ref[...])
    lane = lax.broadcasted_iota(jnp.int32, (qb, LANE), 1)

    def gate(branch, r):
        pick = lane == branch * NSA_HEADS + g * r8 + r
        return jnp.sum(jnp.where(pick, gates, 0.0), axis=-1, keepdims=True)

    for r in range(r8):
        o = gate(0, r) * o_cmp[rows(r)] + gate(1, r) * o_sel[r] + gate(2, r) * o_win[r]
        o_ref[:, r * hd:(r + 1) * hd] = o.astype(BF16)


def nsa_prompt_attention(z, ckv, gate_bias, b, t, qb=128, kb=256):
    nc = t // CMP_STRIDE
    nq = t // qb
    hd = NSA_HEAD_DIM
    n_slc = t // SEL_BLOCK
    slopes = jnp.exp2(-8.0 * (jnp.arange(NSA_HEADS, dtype=F32) + 1.0) / NSA_HEADS)
    ci = jnp.arange(nc)[:, None]
    sj = jnp.arange(LANE)[None, :]
    ov = ((CMP_STRIDE * ci < SEL_BLOCK * (sj + 1)) & (CMP_STRIDE * ci + CMP_LEN > SEL_BLOCK * sj)
          & (sj < n_slc)).astype(BF16)
    gb = jnp.pad(gate_bias, (0, LANE - gate_bias.shape[0])).reshape(1, LANE)
    kv_spec = lambda off: pl.BlockSpec((t, hd), lambda bi, g, qi: (bi, off + g))
    kv_col = (NSA_QD + 2 * NSA_KVD) // hd
    return pl.pallas_call(
        functools.partial(_nsa_prompt_kernel, t, qb, kb, nc),
        grid=(b, NSA_KV_HEADS, nq),
        in_specs=[
            pl.BlockSpec(memory_space=pltpu.SMEM),
            pl.BlockSpec((qb, NSA_Q_PER_KV * hd), lambda bi, g, qi: (bi * nq + qi, g)),
            kv_spec(kv_col), kv_spec(kv_col + 4), kv_spec(kv_col + 8), kv_spec(kv_col + 12),
            pl.BlockSpec((qb, LANE), lambda bi, g, qi: (bi * nq + qi, Z_GATE_COL)),
            pl.BlockSpec((1, LANE), lambda bi, g, qi: (0, 0)),
            pl.BlockSpec((None, None, None, nc, hd), lambda bi, g, qi: (bi, 0, g, 0, 0)),
            pl.BlockSpec((None, None, None, nc, hd), lambda bi, g, qi: (bi, 1, g, 0, 0)),
            pl.BlockSpec((nc, LANE), lambda bi, g, qi: (0, 0)),
        ],
        out_specs=pl.BlockSpec((qb, NSA_Q_PER_KV * hd), lambda bi, g, qi: (bi * nq + qi, g)),
        out_shape=jax.ShapeDtypeStruct((b * t, NSA_QD), BF16),
        scratch_shapes=[pltpu.VMEM((NSA_Q_PER_KV, qb, LANE), F32), pltpu.VMEM((NSA_Q_PER_KV, qb, LANE), F32),
                        pltpu.VMEM((NSA_Q_PER_KV, qb, hd), F32),
                        pltpu.VMEM((NSA_Q_PER_KV * qb, max(kb, nc)), BF16)],
        compiler_params=_params("parallel", "parallel", "arbitrary"),
        name="nsa_prompt_attn",
    )(slopes, z, z, z, z, z, z, gb, ckv, ckv, ov)


_NT = (((1,), (1,)), ((), ()))
_TN = (((0,), (0,)), ((), ()))


GATHER_PAGES = 4


def _page_gather_kernel(n_steps, pt_ref, *refs):
    page_refs, tail_ref, o_ref = refs[:GATHER_PAGES], refs[GATHER_PAGES], refs[GATHER_PAGES + 1]
    p = pl.program_id(1)

    @pl.when(p < n_steps)
    def _():
        for k, x_ref in enumerate(page_refs):
            for c in range(2):
                for g in range(NSA_KV_HEADS):
                    col = (c * NSA_KV_HEADS + g) * NSA_HEAD_DIM
                    o_ref[k * PAGE_SIZE:(k + 1) * PAGE_SIZE, col:col + NSA_HEAD_DIM] = x_ref[:, c, g, :]

    @pl.when(p == n_steps)
    def _():
        o_ref[...] = tail_ref[...]


def page_gather(cache, page_table, tail):
    nb, n_pages = page_table.shape
    step_rows = GATHER_PAGES * PAGE_SIZE
    n_steps = n_pages // GATHER_PAGES
    w = tail.shape[2]

    def page_spec(k):
        def index(b, p, pt):
            return (pt[b, jnp.minimum(p, n_steps - 1) * GATHER_PAGES + k], 0, 0, 0, 0)
        return pl.BlockSpec((None, PAGE_SIZE, 2, NSA_KV_HEADS, NSA_HEAD_DIM), index)

    return pl.pallas_call(
        functools.partial(_page_gather_kernel, n_steps),
        grid_spec=pltpu.PrefetchScalarGridSpec(
            num_scalar_prefetch=1,
            grid=(nb, n_steps + 1),
            in_specs=[page_spec(k) for k in range(GATHER_PAGES)]
            + [pl.BlockSpec((None, step_rows, w), lambda b, p, pt: (b, 0, 0))],
            out_specs=pl.BlockSpec((step_rows, w), lambda b, p, pt: (b * (n_steps + 1) + p, 0)),
        ),
        out_shape=jax.ShapeDtypeStruct((nb * (n_steps + 1) * step_rows, w), F32),
        compiler_params=_params("parallel", "arbitrary"),
        name="nsa_page_gather",
    )(page_table, *([cache] * GATHER_PAGES), tail)


def _row_slopes(slope_ref, g, n_rows, t):
    r_idx = lax.broadcasted_iota(jnp.int32, (n_rows, 1), 0) // t
    col = jnp.zeros((n_rows, 1), F32)
    for r in range(NSA_Q_PER_KV):
        col = jnp.where(r_idx == r, slope_ref[g * NSA_Q_PER_KV + r], col)
    return col


def _nsa_sample_select_kernel(t, pos0, nc, n_cmp, n_slc, slope_ref, q_ref, ck_ref, cv_ref, ov_ref,
                              ocmp_ref, idx_ref, valid_ref):
    g = pl.program_id(0)
    r8 = NSA_Q_PER_KV
    n_rows = r8 * t
    nsp = ov_ref.shape[1]
    q2 = (q_ref[...] * (NSA_HEAD_DIM ** -0.5)).astype(BF16)
    slope = _row_slopes(slope_ref, g, n_rows, t)
    tq = pos0 + lax.broadcasted_iota(jnp.int32, (n_rows, 1), 0) % t
    n_i = lax.broadcasted_iota(jnp.int32, (n_rows, nc), 1)
    dist_ci = tq - (CMP_STRIDE * n_i + (CMP_LEN - 1))
    vis_c = (dist_ci >= 0) & (n_i < n_cmp)
    s = lax.dot_general(q2, ck_ref[...], _NT, preferred_element_type=F32)
    s = jnp.where(vis_c, s - slope * dist_ci.astype(F32), NEG_INF)
    e = jnp.exp(s - jnp.max(s, axis=-1, keepdims=True))
    p = jnp.where(vis_c, e / jnp.sum(e, axis=-1, keepdims=True), 0.0)
    ocmp_ref[...] = jnp.dot(p.astype(BF16), cv_ref[...], preferred_element_type=F32)

    p_hi = p.astype(BF16)
    p_mid = (p - p_hi.astype(F32)).astype(BF16)
    p_lo = (p - p_hi.astype(F32) - p_mid.astype(F32)).astype(BF16)
    ov = ov_ref[...]
    imp = (jnp.dot(p_hi, ov, preferred_element_type=F32) + jnp.dot(p_mid, ov, preferred_element_type=F32)
           + jnp.dot(p_lo, ov, preferred_element_type=F32))
    sh = t
    while sh < n_rows:
        imp = imp + pltpu.roll(imp, sh, 0)
        sh *= 2
    imp = imp[0:8]
    j_i = lax.broadcasted_iota(jnp.int32, (8, nsp), 1)
    cur = (pos0 + lax.broadcasted_iota(jnp.int32, (8, 1), 0) % t) // SEL_BLOCK
    vis_j = j_i <= cur
    forced = (j_i == 0) | (j_i == cur) | (j_i == cur - 1)
    score = jnp.where(forced, FORCE_SCORE, jnp.where(vis_j, imp, -FORCE_SCORE))
    score = jnp.where(j_i < n_slc, score, -jnp.inf)
    rank = jnp.zeros((8, nsp), jnp.int32)
    for jp in range(n_slc):
        col = score[:, jp:jp + 1]
        ahead = (col > score) | ((col == score) & (jp < j_i))
        rank = rank + ahead.astype(jnp.int32)
    lane = lax.broadcasted_iota(jnp.int32, (8, LANE), 1)
    idx = jnp.zeros((8, LANE), F32)
    valid = jnp.zeros((8, LANE), F32)
    j_f = j_i.astype(F32)
    for k in range(min(N_SEL, n_slc)):
        hit = (rank == k) & (j_i < n_slc)
        idx_k = jnp.sum(jnp.where(hit, j_f, 0.0), axis=-1, keepdims=True)
        valid_k = jnp.sum(jnp.where(hit & vis_j, 1.0, 0.0), axis=-1, keepdims=True)
        idx = jnp.where(lane == k, idx_k, idx)
        valid = jnp.where(lane == k, valid_k, valid)
    idx_ref[...] = idx.astype(jnp.int32)
    valid_ref[...] = valid.astype(jnp.int32)


def nsa_sample_select(q, ckv, nb, t, pos0, n_cmp, n_slc):
    g4, hd = NSA_KV_HEADS, NSA_HEAD_DIM
    nc = ckv.shape[3]
    n_rows = NSA_Q_PER_KV * t
    nsp = -(-n_slc // LANE) * LANE
    slopes = jnp.exp2(-8.0 * (jnp.arange(NSA_HEADS, dtype=F32) + 1.0) / NSA_HEADS)
    ci = jnp.arange(nc)[:, None]
    sj = jnp.arange(nsp)[None, :]
    ov = ((CMP_STRIDE * ci < SEL_BLOCK * (sj + 1)) & (CMP_STRIDE * ci + CMP_LEN > SEL_BLOCK * sj)
          & (sj < n_slc) & (ci < n_cmp)).astype(BF16)
    return pl.pallas_call(
        functools.partial(_nsa_sample_select_kernel, t, pos0, nc, n_cmp, n_slc),
        grid=(g4, nb),
        in_specs=[
            pl.BlockSpec(memory_space=pltpu.SMEM),
            pl.BlockSpec((None, None, n_rows, hd), lambda g, b: (g, b, 0, 0)),
            pl.BlockSpec((None, None, None, nc, hd), lambda g, b: (b, 0, g, 0, 0)),
            pl.BlockSpec((None, None, None, nc, hd), lambda g, b: (b, 1, g, 0, 0)),
            pl.BlockSpec((nc, nsp), lambda g, b: (0, 0)),
        ],
        out_specs=[
            pl.BlockSpec((None, None, n_rows, hd), lambda g, b: (g, b, 0, 0)),
            pl.BlockSpec((None, None, 8, LANE), lambda g, b: (g, b, 0, 0)),
            pl.BlockSpec((None, None, 8, LANE), lambda g, b: (g, b, 0, 0)),
        ],
        out_shape=[jax.ShapeDtypeStruct((g4, nb, n_rows, hd), F32),
                   jax.ShapeDtypeStruct((g4, nb, 8, LANE), jnp.int32),
                   jax.ShapeDtypeStruct((g4, nb, 8, LANE), jnp.int32)],
        compiler_params=_params("parallel", "parallel"),
        name="nsa_sample_select",
    )(slopes, q, ckv, ckv, ov)


def _nsa_sample_attn_kernel(nb, t, pos0, n_pages, wb, idx_ref, valid_ref, pt_ref, slope_ref, q_ref, ocmp_ref,
                            gl_ref, gb_ref, kwin_ref, vwin_ref, kwt_ref, vwt_ref, cache_ref, tail_ref, o_ref,
                            kbuf, vbuf, q2_ref, osel_ref, sem):
    b = pl.program_id(0)
    g = pl.program_id(1)
    r8 = NSA_Q_PER_KV
    hd = NSA_HEAD_DIM
    n_rows = r8 * t
    n_sel = kbuf.shape[1]
    past_blocks = n_pages * (PAGE_SIZE // SEL_BLOCK)
    per_page = PAGE_SIZE // SEL_BLOCK

    def sel_entry(tt, k):
        return ((g * nb + b) * t + tt) * n_sel + k

    def block_copies(tt, k, in_tail):
        i = idx_ref[sel_entry(tt, k)]
        out = []
        for c, buf in ((0, kbuf), (1, vbuf)):
            col = c * NSA_KV_HEADS + g
            if in_tail:
                src = tail_ref.at[b, :, pl.ds(col * hd, hd)]
            else:
                ic = jnp.minimum(i, past_blocks - 1)
                page = pt_ref[b * n_pages + ic // per_page]
                src = cache_ref.at[page, pl.ds((ic % per_page) * SEL_BLOCK, SEL_BLOCK), c, g, :]
            out.append(pltpu.make_async_copy(src, buf.at[tt, k], sem))
        return out

    def for_each_block(fn):
        for tt in range(t):
            for k in range(n_sel):
                in_tail = idx_ref[sel_entry(tt, k)] >= past_blocks

                @pl.when(in_tail)
                def _():
                    for cp in block_copies(tt, k, True):
                        fn(cp)

                @pl.when(jnp.logical_not(in_tail))
                def _():
                    for cp in block_copies(tt, k, False):
                        fn(cp)

    for_each_block(lambda cp: cp.start())

    q2_ref[...] = q_ref[...] * (hd ** -0.5)
    q2 = q2_ref[...].astype(BF16)
    slope = _row_slopes(slope_ref, g, n_rows, t)
    tq = pos0 + lax.broadcasted_iota(jnp.int32, (n_rows, 1), 0) % t

    kw = jnp.concatenate([kwin_ref[...], kwt_ref[...]], axis=0).astype(BF16)
    vw = jnp.concatenate([vwin_ref[...], vwt_ref[...]], axis=0).astype(BF16)
    nk = kw.shape[0]
    s_pos = (pos0 - wb) + lax.broadcasted_iota(jnp.int32, (n_rows, nk), 1)
    dist = tq - s_pos
    mask = (dist >= 0) & (dist <= WINDOW) & (s_pos >= 0)
    s = lax.dot_general(q2, kw, _NT, preferred_element_type=F32)
    s = jnp.where(mask, s - slope * dist.astype(F32), NEG_INF)
    e = jnp.exp(s - jnp.max(s, axis=-1, keepdims=True))
    p = e / jnp.sum(e, axis=-1, keepdims=True)
    o_win = jnp.dot(p.astype(BF16), vw, preferred_element_type=F32)

    for_each_block(lambda cp: cp.wait())

    nkeys = n_sel * SEL_BLOCK
    key_lane = lax.broadcasted_iota(jnp.int32, (1, nkeys), 1)
    slope8 = _row_slopes(slope_ref, g, r8, 1)
    for tt in range(t):
        blk = jnp.zeros((1, nkeys), jnp.int32)
        ok = jnp.zeros((1, nkeys), jnp.int32)
        for k in range(n_sel):
            here = key_lane // SEL_BLOCK == k
            blk = jnp.where(here, idx_ref[sel_entry(tt, k)], blk)
            ok = jnp.where(here, valid_ref[sel_entry(tt, k)], ok)
        dist = (pos0 + tt) - (blk * SEL_BLOCK + key_lane % SEL_BLOCK)
        mask = (dist >= 0) & (ok > 0)
        q_t = q2_ref[pl.ds(tt, r8, stride=t), :].astype(BF16)
        k_t = kbuf[tt].reshape(nkeys, hd).astype(BF16)
        v_t = vbuf[tt].reshape(nkeys, hd).astype(BF16)
        s = lax.dot_general(q_t, k_t, _NT, preferred_element_type=F32)
        s = jnp.where(mask, s - slope8 * dist.astype(F32), NEG_INF)
        e = jnp.exp(s - jnp.max(s, axis=-1, keepdims=True))
        p = e / jnp.sum(e, axis=-1, keepdims=True)
        osel_ref[pl.ds(tt, r8, stride=t), :] = jnp.dot(p.astype(BF16), v_t, preferred_element_type=F32)

    gates = jax.nn.sigmoid(gl_ref[...] + gb_ref[...])
    o = gates[:, 0:1] * ocmp_ref[...] + gates[:, 1:2] * osel_ref[...] + gates[:, 2:3] * o_win
    o_ref[...] = o.astype(BF16)


def nsa_sample_attention(q, o_cmp, gl, gb, idx, valid, page_table, cache_sel, tail_sel, cache_win, tail_win,
                         nb, t, pos0):
    g4, hd = NSA_KV_HEADS, NSA_HEAD_DIM
    n_rows = NSA_Q_PER_KV * t
    n_pages = page_table.shape[1]
    n_sel = idx.shape[-1]
    wb = cache_win.shape[1]
    slopes = jnp.exp2(-8.0 * (jnp.arange(NSA_HEADS, dtype=F32) + 1.0) / NSA_HEADS)
    row_spec = lambda w: pl.BlockSpec((None, None, n_rows, w), lambda b, g, *_: (g, b, 0, 0))
    win_spec = lambda rows, c: pl.BlockSpec((None, rows, hd), lambda b, g, *_: (b, 0, c * g4 + g))
    return pl.pallas_call(
        functools.partial(_nsa_sample_attn_kernel, nb, t, pos0, n_pages, wb),
        grid_spec=pltpu.PrefetchScalarGridSpec(
            num_scalar_prefetch=3,
            grid=(nb, g4),
            in_specs=[
                pl.BlockSpec(memory_space=pltpu.SMEM),
                row_spec(hd), row_spec(hd), row_spec(3),
                pl.BlockSpec((None, n_rows, 3), lambda b, g, *_: (g, 0, 0)),
                win_spec(wb, 0), win_spec(wb, 1), win_spec(LANE, 0), win_spec(LANE, 1),
                pl.BlockSpec(memory_space=pl.ANY), pl.BlockSpec(memory_space=pl.ANY),
            ],
            out_specs=row_spec(hd),
            scratch_shapes=[pltpu.VMEM((t, n_sel, SEL_BLOCK, hd), F32), pltpu.VMEM((t, n_sel, SEL_BLOCK, hd), F32),
                            pltpu.VMEM((n_rows, hd), F32), pltpu.VMEM((n_rows, hd), F32),
                            pltpu.SemaphoreType.DMA(())],
        ),
        out_shape=jax.ShapeDtypeStruct((g4, nb, n_rows, hd), BF16),
        compiler_params=_params("arbitrary", "arbitrary"),
        name="nsa_sample_attn",
    )(idx.reshape(-1), valid.reshape(-1), page_table.reshape(-1), slopes, q, o_cmp, gl, gb,
      cache_win, cache_win, tail_win, tail_win, cache_sel, tail_sel)


def nsa_sample_mix(z, nb, t, cache_cmp, cache_sel, cache_win, page_table, gate_bias, w_cmp, pe_cmp):
    g4, r8, hd = NSA_KV_HEADS, NSA_Q_PER_KV, NSA_HEAD_DIM
    n_pages = page_table.shape[1]
    past = n_pages * PAGE_SIZE
    wb = cache_win.shape[1]
    t_all = past + t
    tp = -(-t_all // SEL_BLOCK) * SEL_BLOCK
    n_slc = tp // SEL_BLOCK
    n_cmp = tp // CMP_STRIDE - 1
    gather_rows = (n_pages + GATHER_PAGES) * PAGE_SIZE
    nc = gather_rows // CMP_STRIDE
    kv = z[:, NSA_QD:NSA_QD + 6 * NSA_KVD].reshape(nb, t, 3, 2 * NSA_KVD)

    tail_cmp = jnp.pad(kv[:, :, 0], ((0, 0), (0, gather_rows - past - t), (0, 0)))
    rows = page_gather(cache_cmp, page_table, tail_cmp)
    ckv = compress(rows, nb, nc, 0, w_cmp, pe_cmp)

    q = z[:, :NSA_QD].reshape(nb, t, g4, r8, hd).transpose(2, 0, 3, 1, 4).reshape(g4, nb, r8 * t, hd)
    o_cmp, idx, valid = nsa_sample_select(q, ckv, nb, t, past, n_cmp, n_slc)
    n_sel = min(N_SEL, n_slc)
    idx = idx[:, :, :t, :n_sel]
    valid = valid[:, :, :t, :n_sel]

    gl = z[:, NSA_QD + 6 * NSA_KVD:NSA_QD + 6 * NSA_KVD + 3 * NSA_HEADS].reshape(nb, t, 3, g4, r8)
    gl = gl.transpose(3, 0, 4, 1, 2).reshape(g4, nb, r8 * t, 3)
    gb = jnp.broadcast_to(gate_bias.reshape(3, g4, r8, 1), (3, g4, r8, t)).transpose(1, 2, 3, 0).reshape(g4, r8 * t, 3)
    tail_sel = jnp.pad(kv[:, :, 1], ((0, 0), (0, SEL_BLOCK - t), (0, 0)))
    tail_win = jnp.pad(kv[:, :, 2], ((0, 0), (0, LANE - t), (0, 0)))
    o = nsa_sample_attention(q, o_cmp, gl, gb, idx, valid, page_table,
                             cache_sel, tail_sel,
                             cache_win.reshape(nb, wb, 2 * NSA_KVD), tail_win, nb, t, past)
    o = o.reshape(g4, nb, r8, t, hd).transpose(1, 3, 0, 2, 4).reshape(nb * t, NSA_QD)
    kv6 = kv.reshape(nb, t, 3, 2, g4, hd)
    win_new = jnp.concatenate([cache_win, kv6[:, :, 2]], axis=1)[:, t:]
    return o, kv6[:, :, 0], kv6[:, :, 1], win_new


GLA_QK = GLA_HEADS * GLA_DK
GLA_VD = GLA_HEADS * GLA_DV
GLA_SUPER = 256


def _chunk_scan(la, c):
    n = la.shape[0]
    ri = lax.broadcasted_iota(jnp.int32, la.shape, 0) % c
    b = la
    s = 1
    while s < c:
        b = b + jnp.where(ri >= s, pltpu.roll(b, s, 0), 0.0)
        s *= 2
    tot = jnp.where(ri == c - 1, b, 0.0)
    s = 1
    while s < c:
        tot = tot + jnp.where(ri + s < c, pltpu.roll(tot, n - s, 0), 0.0)
        s *= 2
    return b, tot


def _gla_gates(a_rows, wa_ref, ba_ref):
    x = jnp.dot(a_rows.astype(BF16), wa_ref[...], preferred_element_type=F32) + ba_ref[...]
    return jax.nn.log_sigmoid(x) / GLA_TAU


def _gla_finish(o, r, g_ref):
    return (_norm_rows(o, g_ref[...]) * (r * jax.nn.sigmoid(r))).astype(BF16)


def _gla_prompt_kernel(t, q_ref, k_ref, v_ref, r_ref, a_ref, wa_ref, ba_ref, g_ref, o_ref, s_ref, st_ref, oc_ref):
    c = GLA_CHUNK
    sc = GLA_SUPER
    st_ref[...] = jnp.zeros(st_ref.shape, F32)
    ti = lax.broadcasted_iota(jnp.int32, (sc, sc), 0)
    si = lax.broadcasted_iota(jnp.int32, (sc, sc), 1)
    causal = (ti // c == si // c) & (si <= ti)

    def body(i, carry):
        r0 = pl.multiple_of(i * sc, sc)
        la = _gla_gates(a_ref[pl.ds(r0, sc), :], wa_ref, ba_ref)
        b, b_last = _chunk_scan(la, c)
        q = q_ref[pl.ds(r0, sc), :] * (GLA_DK ** -0.5)
        k = k_ref[pl.ds(r0, sc), :]
        qe = (q * jnp.exp(b)).astype(BF16)
        ke = (k * jnp.exp(-b)).astype(BF16)
        kd = (k * jnp.exp(b_last - b)).astype(BF16)
        vb = v_ref[pl.ds(r0, sc), :].astype(BF16)
        att = jnp.where(causal, lax.dot_general(qe, ke, _NT, preferred_element_type=F32), 0.0)
        o_intra = jnp.dot(att.astype(BF16), vb, preferred_element_type=F32)
        for j in range(sc // c):
            rows = slice(j * c, (j + 1) * c)
            st = st_ref[...]
            o_inter = lax.dot_general(qe[rows], st.astype(BF16), _NT, preferred_element_type=F32)
            oc_ref[rows, :] = o_intra[rows] + o_inter
            dec = jnp.exp(b_last[j * c:j * c + 1, :])
            st_ref[...] = dec * st + lax.dot_general(vb[rows], kd[rows], _TN, preferred_element_type=F32)
        o_ref[pl.ds(r0, sc), :] = _gla_finish(oc_ref[...], r_ref[pl.ds(r0, sc), :], g_ref)
        return carry

    lax.fori_loop(0, t // sc, body, 0)
    s_ref[...] = st_ref[...].T


def gla_prompt(z, b, t, w_alpha, b_alpha, norm_g):
    dk, dv, h = GLA_DK, GLA_DV, GLA_HEADS
    wa = jnp.pad(w_alpha, ((0, LANE - GLA_RANK), (0, 0))).astype(BF16)
    return pl.pallas_call(
        functools.partial(_gla_prompt_kernel, t),
        grid=(b, h),
        in_specs=[
            pl.BlockSpec((t, dk), lambda bi, hi: (bi, hi)),
            pl.BlockSpec((t, dk), lambda bi, hi: (bi, h + hi)),
            pl.BlockSpec((t, dv), lambda bi, hi: (bi, 2 * GLA_QK // dv + hi)),
            pl.BlockSpec((t, dv), lambda bi, hi: (bi, (2 * GLA_QK + GLA_VD) // dv + hi)),
            pl.BlockSpec((t, LANE), lambda bi, hi: (bi, (2 * GLA_QK + 2 * GLA_VD) // LANE)),
            pl.BlockSpec((LANE, dk), lambda bi, hi: (0, hi)),
            pl.BlockSpec((1, dk), lambda bi, hi: (0, hi)),
            pl.BlockSpec((1, dv), lambda bi, hi: (0, 0)),
        ],
        out_specs=[
            pl.BlockSpec((t, dv), lambda bi, hi: (bi, hi)),
            pl.BlockSpec((None, None, dk, dv), lambda bi, hi: (bi, hi, 0, 0)),
        ],
        out_shape=[jax.ShapeDtypeStruct((b * t, GLA_VD), BF16), jax.ShapeDtypeStruct((b, h, dk, dv), F32)],
        scratch_shapes=[pltpu.VMEM((dv, dk), F32), pltpu.VMEM((GLA_SUPER, dv), F32)],
        compiler_params=_params("parallel", "arbitrary"),
        name="gla_prompt",
    )(z, z, z, z, z, wa, b_alpha.reshape(1, -1), norm_g.reshape(1, -1))


def _gla_sample_kernel(nb, t, q_ref, k_ref, v_ref, r_ref, a_ref, wa_ref, ba_ref, g_ref, s0_ref, o_ref, s_ref):
    n = nb * t
    la = _gla_gates(a_ref[...], wa_ref, ba_ref)
    b, b_last = _chunk_scan(la, t)
    q = q_ref[...] * (GLA_DK ** -0.5)
    k = k_ref[...]
    qe = q * jnp.exp(b)
    ke = (k * jnp.exp(-b)).astype(BF16)
    kd = k * jnp.exp(b_last - b)
    vb = v_ref[...].astype(BF16)
    ti = lax.broadcasted_iota(jnp.int32, (n, n), 0)
    si = lax.broadcasted_iota(jnp.int32, (n, n), 1)
    causal = (ti // t == si // t) & (si <= ti)
    att = jnp.where(causal, lax.dot_general(qe.astype(BF16), ke, _NT, preferred_element_type=F32), 0.0)
    o = jnp.dot(att.astype(BF16), vb, preferred_element_type=F32)
    row = lax.broadcasted_iota(jnp.int32, (n, 1), 0) // t
    for i in range(nb):
        mine = row == i
        st = s0_ref[i].T
        qe_i = jnp.where(mine, qe, 0.0).astype(BF16)
        kd_i = jnp.where(mine, kd, 0.0).astype(BF16)
        o = o + lax.dot_general(qe_i, st.astype(BF16), _NT, preferred_element_type=F32)
        dec = jnp.exp(b_last[i * t:i * t + 1, :])
        s_ref[i] = (dec * st + lax.dot_general(vb, kd_i, _TN, preferred_element_type=F32)).T
    o_ref[...] = _gla_finish(o, r_ref[...], g_ref)


def gla_sample(z, nb, t, s0, w_alpha, b_alpha, norm_g):
    dk, dv, h = GLA_DK, GLA_DV, GLA_HEADS
    n = nb * t
    wa = jnp.pad(w_alpha, ((0, LANE - GLA_RANK), (0, 0))).astype(BF16)
    return pl.pallas_call(
        functools.partial(_gla_sample_kernel, nb, t),
        grid=(h,),
        in_specs=[
            pl.BlockSpec((n, dk), lambda hi: (0, hi)),
            pl.BlockSpec((n, dk), lambda hi: (0, h + hi)),
            pl.BlockSpec((n, dv), lambda hi: (0, 2 * GLA_QK // dv + hi)),
            pl.BlockSpec((n, dv), lambda hi: (0, (2 * GLA_QK + GLA_VD) // dv + hi)),
            pl.BlockSpec((n, LANE), lambda hi: (0, (2 * GLA_QK + 2 * GLA_VD) // LANE)),
            pl.BlockSpec((LANE, dk), lambda hi: (0, hi)),
            pl.BlockSpec((1, dk), lambda hi: (0, hi)),
            pl.BlockSpec((1, dv), lambda hi: (0, 0)),
            pl.BlockSpec((nb, None, dk, dv), lambda hi: (0, hi, 0, 0)),
        ],
        out_specs=[
            pl.BlockSpec((n, dv), lambda hi: (0, hi)),
            pl.BlockSpec((nb, None, dk, dv), lambda hi: (0, hi, 0, 0)),
        ],
        out_shape=[jax.ShapeDtypeStruct((n, GLA_VD), BF16), jax.ShapeDtypeStruct((nb, h, dk, dv), F32)],
        compiler_params=_params("parallel"),
        name="gla_sample",
    )(z, z, z, z, z, wa, b_alpha.reshape(1, -1), norm_g.reshape(1, -1), s0)


def nsa_prompt_mix(z, kv, b, t, gate_bias, w_cmp, pe_cmp):
    ckv = compress(z, b, t // CMP_STRIDE, NSA_QD // NSA_HEAD_DIM, w_cmp, pe_cmp)
    o = nsa_prompt_attention(z, ckv, gate_bias, b, t)
    kv = kv.reshape(3, b, t, 2, NSA_KV_HEADS, NSA_HEAD_DIM)
    return o, kv[0], kv[1], kv[2, :, t - min(WINDOW, t):]


def _pad_cols(w, mult):
    n = w.shape[-1]
    return jnp.pad(w, ((0, 0), (0, -(-n // mult) * mult - n)))


def kernel(x_prompt, x_sample, p_prompt, p_sample, cache_cmp_kv, cache_sel_kv, cache_win_kv, state_gla, page_table, ffn1_norm, ffn1_w_gu, ffn1_w_down, mix_norm, nsa_w_in, nsa_gate_bias, nsa_w_cmp, nsa_pe_cmp, nsa_w_out, gla_w_in, gla_w_alpha, gla_b_alpha, gla_norm, gla_w_out, ffn2_norm, ffn2_w_gu, ffn2_w_down, ple_norm, ple_w_gate, ple_w_proj, final_norm):
    bp, tp, d = x_prompt.shape
    bs, ts, _ = x_sample.shape
    mp, ms = bp * tp, bs * ts
    xp = x_prompt.reshape(mp, d)
    xs = x_sample.reshape(ms, d)
    tm_p = _row_tile(mp, 512)
    tm_p2 = _row_tile(mp, 512)

    cmp_p, sel_p, win_p, gla_p = [], [], [], []
    cmp_s, sel_s, win_s, gla_s = [], [], [], []
    for i in range(DEPTH):
        w_gu1 = ffn1_w_gu[i].astype(BF16)
        w_d1 = ffn1_w_down[i].astype(BF16)
        xp = ffn(xp, ffn1_norm[i], w_gu1, w_d1, tm_p)
        xs = ffn(xs, ffn1_norm[i], w_gu1, w_d1, ms)
        j = i // 2
        if i % 2 == 0:
            w_in = _pad_cols(nsa_w_in[j], 512).astype(BF16)
            w_out = nsa_w_out[j].astype(BF16)
            zp, kvp = nsa_in_proj(xp, mix_norm[i], w_in, tm_p2)
            zs = norm_matmul(xs, mix_norm[i], w_in, ms)
            op, kc, kl, kw = nsa_prompt_mix(zp, kvp, bp, tp, nsa_gate_bias[j], nsa_w_cmp[j], nsa_pe_cmp[j])
            os_, kc2, kl2, kw2 = nsa_sample_mix(zs, bs, ts, cache_cmp_kv[j], cache_sel_kv[j], cache_win_kv[j],
                                                page_table, nsa_gate_bias[j], nsa_w_cmp[j], nsa_pe_cmp[j])
            cmp_p.append(kc)
            sel_p.append(kl)
            win_p.append(kw)
            cmp_s.append(kc2)
            sel_s.append(kl2)
            win_s.append(kw2)
        else:
            w_in = _pad_cols(gla_w_in[j], 512).astype(BF16)
            w_out = gla_w_out[j].astype(BF16)
            zp = norm_matmul(xp, mix_norm[i], w_in, tm_p2)
            zs = norm_matmul(xs, mix_norm[i], w_in, ms)
            op, sp = gla_prompt(zp, bp, tp, gla_w_alpha[j], gla_b_alpha[j], gla_norm[j])
            os_, ss = gla_sample(zs, bs, ts, state_gla[j], gla_w_alpha[j], gla_b_alpha[j], gla_norm[j])
            gla_p.append(sp)
            gla_s.append(ss)
        xp = matmul_residual(op, w_out, xp, tm_p)
        xs = matmul_residual(os_, w_out, xs, ms)
        w_gu2 = ffn2_w_gu[i].astype(BF16)
        w_d2 = ffn2_w_down[i].astype(BF16)
        xp = ffn(xp, ffn2_norm[i], w_gu2, w_d2, tm_p)
        xs = ffn(xs, ffn2_norm[i], w_gu2, w_d2, ms)
        w_pg = ple_w_gate[i].astype(BF16)
        w_pp = ple_w_proj[i].astype(BF16)
        xp = ple(xp, p_prompt[i].reshape(mp, -1), ple_norm[i], w_pg, w_pp, tm_p2)
        xs = ple(xs, p_sample[i].reshape(ms, -1), ple_norm[i], w_pg, w_pp, ms)
    y_prompt = rmsnorm(xp, final_norm, tm_p2).reshape(bp, tp, d)
    y_sample = rmsnorm(xs, final_norm, ms).reshape(bs, ts, d)
    return (y_prompt, y_sample, jnp.stack(cmp_p), jnp.stack(sel_p), jnp.stack(win_p), jnp.stack(gla_p),
            jnp.stack(cmp_s), jnp.stack(sel_s), jnp.stack(win_s), jnp.stack(gla_s))
```

```python
import functools
import math

import jax
import jax.numpy as jnp
from jax import lax
from jax.experimental import pallas as pl
from jax.experimental.pallas import tpu as pltpu

F32 = jnp.float32
BF16 = jnp.bfloat16

D_MODEL = 4096
DEPTH = 2
PAST_LEN = 16384
PAGE_SIZE = 128
NSA_HEADS = 32
NSA_HEAD_DIM = 128
NSA_KV_HEADS = 4
NSA_Q_PER_KV = 8
CMP_STRIDE = 16
CMP_LEN = 32
SEL_BLOCK = 64
N_SEL = 16
WINDOW = 512
NSA_QBLOCK = 32
NSA_QD = NSA_HEADS * NSA_HEAD_DIM
NSA_KVD = NSA_KV_HEADS * NSA_HEAD_DIM
GLA_HEADS = 8
GLA_DK = 256
GLA_DV = 512
GLA_RANK = 16
GLA_TAU = 16.0
GLA_CHUNK = 32
D_FF = 11008
EPS = 1e-6
NEG_INF = -1e30
FORCE_SCORE = 1e30

V7X_VMEM_LIMIT_BYTES = 56 * 1024 * 1024
LANE = 128


def _params(*sem):
    return pltpu.CompilerParams(dimension_semantics=sem, vmem_limit_bytes=V7X_VMEM_LIMIT_BYTES)


def _norm_rows(x, g):
    ms = jnp.mean(x * x, axis=-1, keepdims=True)
    return (x * lax.rsqrt(ms + EPS)) * g


def _row_tile(m, want):
    return want if m % want == 0 else m


def _ffn_step(x_ref, g_ref, wa, wu, wd, o_ref, h_ref):
    @pl.when(pl.program_id(1) == 0)
    def _():
        x = x_ref[...]
        h_ref[...] = _norm_rows(x, g_ref[...]).astype(BF16)
        o_ref[...] = x

    h = h_ref[...]
    a = jnp.dot(h, wa, preferred_element_type=F32)
    u = jnp.dot(h, wu, preferred_element_type=F32)
    act = (0.5 * (a * jax.nn.sigmoid(a)) * u).astype(BF16)
    o_ref[...] += jnp.dot(act, wd, preferred_element_type=F32)


def _ffn_kernel(x_ref, g_ref, wa_ref, wu_ref, wd_ref, o_ref, h_ref):
    _ffn_step(x_ref, g_ref, wa_ref[...], wu_ref[...], wd_ref[...], o_ref, h_ref)


def ffn(x, g, wa, wu, wd, tm, tf=256):
    m, d = x.shape
    return pl.pallas_call(
        _ffn_kernel,
        grid=(m // tm, D_FF // tf),
        in_specs=[
            pl.BlockSpec((tm, d), lambda i, j: (i, 0), pipeline_mode=pl.Buffered(1)),
            pl.BlockSpec((1, d), lambda i, j: (0, 0)),
            pl.BlockSpec((d, tf), lambda i, j: (0, j)),
            pl.BlockSpec((d, tf), lambda i, j: (0, j)),
            pl.BlockSpec((tf, d), lambda i, j: (j, 0)),
        ],
        out_specs=pl.BlockSpec((tm, d), lambda i, j: (i, 0)),
        out_shape=jax.ShapeDtypeStruct((m, d), F32),
        scratch_shapes=[pltpu.VMEM((tm, d), BF16)],
        compiler_params=_params("parallel", "arbitrary"),
        name="ffn",
    )(x, g.reshape(1, d), wa, wu, wd)


def _ffn_cast_kernel(x_ref, g_ref, wa_ref, wu_ref, wd_ref, o_ref, wa_out, wu_out, wd_out, h_ref):
    wa_out[...] = wa_ref[...].astype(BF16)
    wu_out[...] = wu_ref[...].astype(BF16)
    wd_out[...] = wd_ref[...].astype(BF16)
    _ffn_step(x_ref, g_ref, wa_out[...], wu_out[...], wd_out[...], o_ref, h_ref)


def ffn_cast(x, g, w_gu, w_down, layer, tf=256):
    m, d = x.shape
    nf = D_FF // tf
    return pl.pallas_call(
        _ffn_cast_kernel,
        grid=(1, nf),
        in_specs=[
            pl.BlockSpec((m, d), lambda i, j: (0, 0)),
            pl.BlockSpec((1, d), lambda i, j: (0, 0)),
            pl.BlockSpec((None, d, tf), lambda i, j: (layer, 0, j)),
            pl.BlockSpec((None, d, tf), lambda i, j: (layer, 0, j + nf)),
            pl.BlockSpec((None, tf, d), lambda i, j: (layer, j, 0)),
        ],
        out_specs=[
            pl.BlockSpec((m, d), lambda i, j: (0, 0)),
            pl.BlockSpec((d, tf), lambda i, j: (0, j)),
            pl.BlockSpec((d, tf), lambda i, j: (0, j)),
            pl.BlockSpec((tf, d), lambda i, j: (j, 0)),
        ],
        out_shape=[jax.ShapeDtypeStruct((m, d), F32), jax.ShapeDtypeStruct((d, D_FF), BF16),
                   jax.ShapeDtypeStruct((d, D_FF), BF16), jax.ShapeDtypeStruct((D_FF, d), BF16)],
        scratch_shapes=[pltpu.VMEM((m, d), BF16)],
        compiler_params=_params("arbitrary", "arbitrary"),
        name="ffn_cast",
    )(x, g.reshape(1, d), w_gu, w_gu, w_down)


def _norm_matmul_kernel(x_ref, g_ref, w_ref, o_ref, h_ref):
    @pl.when(pl.program_id(1) == 0)
    def _():
        h_ref[...] = _norm_rows(x_ref[...], g_ref[...]).astype(BF16)

    o_ref[...] = jnp.dot(h_ref[...], w_ref[...], preferred_element_type=F32)


def norm_matmul(x, g, w, tm, tn=512):
    m, d = x.shape
    n = w.shape[1]
    return pl.pallas_call(
        _norm_matmul_kernel,
        grid=(m // tm, n // tn),
        in_specs=[
            pl.BlockSpec((tm, d), lambda i, j: (i, 0)),
            pl.BlockSpec((1, d), lambda i, j: (0, 0)),
            pl.BlockSpec((d, tn), lambda i, j: (0, j)),
        ],
        out_specs=pl.BlockSpec((tm, tn), lambda i, j: (i, j)),
        out_shape=jax.ShapeDtypeStruct((m, n), F32),
        scratch_shapes=[pltpu.VMEM((tm, d), BF16)],
        compiler_params=_params("parallel", "arbitrary"),
    )(x, g.reshape(1, d), w)


def _nsa_in_proj_kernel(tn, x_ref, g_ref, w_ref, o_ref, kv_ref, h_ref):
    j = pl.program_id(1)

    @pl.when(j == 0)
    def _():
        h_ref[...] = _norm_rows(x_ref[...], g_ref[...]).astype(BF16)

    res = jnp.dot(h_ref[...], w_ref[...], preferred_element_type=F32)
    o_ref[...] = res

    @pl.when((j >= NSA_QD // tn) & (j < (NSA_QD + 6 * NSA_KVD) // tn))
    def _():
        for g in range(NSA_KV_HEADS):
            kv_ref[:, g, :] = res[:, g * NSA_HEAD_DIM:(g + 1) * NSA_HEAD_DIM]


def nsa_in_proj(x, g, w, tm):
    m, d = x.shape
    n = w.shape[1]
    tn = NSA_KVD
    q_tiles = NSA_QD // tn

    def kv_index(i, j):
        t = jnp.clip(j - q_tiles, 0, 5)
        return (t // 2, i, t % 2, 0, 0)

    return pl.pallas_call(
        functools.partial(_nsa_in_proj_kernel, tn),
        grid=(m // tm, n // tn),
        in_specs=[
            pl.BlockSpec((tm, d), lambda i, j: (i, 0)),
            pl.BlockSpec((1, d), lambda i, j: (0, 0)),
            pl.BlockSpec((d, tn), lambda i, j: (0, j)),
        ],
        out_specs=[
            pl.BlockSpec((tm, tn), lambda i, j: (i, j)),
            pl.BlockSpec((None, tm, None, NSA_KV_HEADS, NSA_HEAD_DIM), kv_index),
        ],
        out_shape=[jax.ShapeDtypeStruct((m, n), F32),
                   jax.ShapeDtypeStruct((3, m, 2, NSA_KV_HEADS, NSA_HEAD_DIM), F32)],
        scratch_shapes=[pltpu.VMEM((tm, d), BF16)],
        compiler_params=_params("parallel", "arbitrary"),
        name="nsa_in_proj",
    )(x, g.reshape(1, d), w)


def _matmul_residual_kernel(a_ref, w_ref, x_ref, o_ref):
    o_ref[...] = x_ref[...] + jnp.dot(a_ref[...], w_ref[...], preferred_element_type=F32)


def matmul_residual(a, w, x, tm, tn=512):
    m, k = a.shape
    n = w.shape[1]
    return pl.pallas_call(
        _matmul_residual_kernel,
        grid=(m // tm, n // tn),
        in_specs=[
            pl.BlockSpec((tm, k), lambda i, j: (i, 0)),
            pl.BlockSpec((k, tn), lambda i, j: (0, j)),
            pl.BlockSpec((tm, tn), lambda i, j: (i, j)),
        ],
        out_specs=pl.BlockSpec((tm, tn), lambda i, j: (i, j)),
        out_shape=jax.ShapeDtypeStruct((m, n), F32),
        compiler_params=_params("parallel", "arbitrary"),
    )(a, w, x)


def _ple_kernel(tn, x_ref, g_ref, wg_ref, p_ref, wp_ref, o_ref, h_ref):
    j = pl.program_id(1)

    @pl.when(j == 0)
    def _():
        h_ref[...] = _norm_rows(x_ref[...], g_ref[...]).astype(BF16)

    gate = jax.nn.sigmoid(jnp.dot(h_ref[...], wg_ref[...], preferred_element_type=F32))
    proj = jnp.dot(p_ref[...].astype(BF16), wp_ref[...], preferred_element_type=F32)
    col = pl.multiple_of(j * tn, tn)
    o_ref[...] = x_ref[:, pl.ds(col, tn)] + gate * proj


def ple(x, p, g, w_gate, w_proj, tm, tn=512):
    m, d = x.shape
    pd = p.shape[1]
    return pl.pallas_call(
        functools.partial(_ple_kernel, tn),
        grid=(m // tm, d // tn),
        in_specs=[
            pl.BlockSpec((tm, d), lambda i, j: (i, 0)),
            pl.BlockSpec((1, d), lambda i, j: (0, 0)),
            pl.BlockSpec((d, tn), lambda i, j: (0, j)),
            pl.BlockSpec((tm, pd), lambda i, j: (i, 0)),
            pl.BlockSpec((pd, tn), lambda i, j: (0, j)),
        ],
        out_specs=pl.BlockSpec((tm, tn), lambda i, j: (i, j)),
        out_shape=jax.ShapeDtypeStruct((m, d), F32),
        scratch_shapes=[pltpu.VMEM((tm, d), BF16)],
        compiler_params=_params("parallel", "arbitrary"),
    )(x, g.reshape(1, d), w_gate, p, w_proj)


def _rmsnorm_kernel(x_ref, g_ref, o_ref):
    o_ref[...] = _norm_rows(x_ref[...], g_ref[...])


def rmsnorm(x, g, tm):
    m, d = x.shape
    return pl.pallas_call(
        _rmsnorm_kernel,
        grid=(m // tm,),
        in_specs=[pl.BlockSpec((tm, d), lambda i: (i, 0)), pl.BlockSpec((1, d), lambda i: (0, 0))],
        out_specs=pl.BlockSpec((tm, d), lambda i: (i, 0)),
        out_shape=jax.ShapeDtypeStruct((m, d), F32),
        compiler_params=_params("parallel"),
    )(x, g.reshape(1, d))


Z_GATE_COL = (NSA_QD + 6 * NSA_KVD) // LANE


def _compress_kernel(nc, x_ref, w2_ref, wf_ref, pe_ref, o_ref):
    acc = jnp.zeros((nc, 2 * NSA_HEAD_DIM), F32)
    for l in range(CMP_STRIDE):
        a = x_ref[pl.ds(l, nc, stride=CMP_STRIDE), :]
        acc += jnp.dot(a.astype(BF16), w2_ref[l], preferred_element_type=F32)
    n_idx = lax.broadcasted_iota(jnp.int32, (nc, NSA_HEAD_DIM), 0)
    first = acc[:, :NSA_HEAD_DIM]
    second = jnp.where(n_idx == nc - 1, 0.0, pltpu.roll(acc[:, NSA_HEAD_DIM:], nc - 1, 0))
    bias = jnp.dot(pe_ref[...], wf_ref[...], preferred_element_type=F32)[0:1]
    o_ref[...] = (first + second + bias).astype(BF16)


def compress(x, nb, nc, col0, w_cmp, pe_cmp):
    hd = NSA_HEAD_DIM
    w2 = w_cmp.reshape(2, 2, CMP_STRIDE, hd, hd).transpose(0, 2, 3, 1, 4).reshape(2, CMP_STRIDE, hd, 2 * hd)
    wf = w_cmp.reshape(2, CMP_LEN * hd, hd)
    pe = jnp.broadcast_to(pe_cmp.reshape(2, 1, CMP_LEN * hd), (2, 8, CMP_LEN * hd))
    return pl.pallas_call(
        functools.partial(_compress_kernel, nc),
        grid=(nb, 2, NSA_KV_HEADS),
        in_specs=[
            pl.BlockSpec((CMP_STRIDE * nc, hd), lambda i, c, g: (i, col0 + c * NSA_KV_HEADS + g)),
            pl.BlockSpec((None, CMP_STRIDE, hd, 2 * hd), lambda i, c, g: (c, 0, 0, 0)),
            pl.BlockSpec((None, CMP_LEN * hd, hd), lambda i, c, g: (c, 0, 0)),
            pl.BlockSpec((None, 8, CMP_LEN * hd), lambda i, c, g: (c, 0, 0)),
        ],
        out_specs=pl.BlockSpec((None, None, None, nc, hd), lambda i, c, g: (i, c, g, 0, 0)),
        out_shape=jax.ShapeDtypeStruct((nb, 2, NSA_KV_HEADS, nc, hd), BF16),
        compiler_params=_params("parallel", "parallel", "parallel"),
        name="nsa_compress",
    )(x, w2.astype(BF16), wf.astype(BF16), pe.astype(BF16))


def _nsa_prompt_kernel(t, qb, kb, nc, slope_ref, zq_ref, ks_ref, vs_ref, kw_ref, vw_ref, gz_ref, gb_ref,
                       ck_ref, cv_ref, ov_ref, o_ref, m_ref, l_ref, acc_ref, p_ref):
    g = pl.program_id(1)
    qi = pl.program_id(2)
    t0 = qi * qb
    r8 = NSA_Q_PER_KV
    hd = NSA_HEAD_DIM
    n_slc = t // SEL_BLOCK
    n_cmp = nc - 1

    zq = zq_ref[...] * (hd ** -0.5)
    q2 = jnp.concatenate([zq[:, r * hd:(r + 1) * hd] for r in range(r8)], axis=0).astype(BF16)

    slope = [slope_ref[g * r8 + r] for r in range(r8)]
    tq = lax.broadcasted_iota(jnp.int32, (qb, 1), 0) + t0
    rows = lambda r: slice(r * qb, (r + 1) * qb)

    n_i = lax.broadcasted_iota(jnp.int32, (qb, nc), 1)
    dist_ci = tq - (CMP_STRIDE * n_i + (CMP_LEN - 1))
    vis_c = (dist_ci >= 0) & (n_i < n_cmp)
    dist_c = dist_ci.astype(F32)
    s_all = lax.dot_general(q2, ck_ref[...], (((1,), (1,)), ((), ())), preferred_element_type=F32)
    psum = jnp.zeros((qb, nc), F32)
    for r in range(r8):
        s = jnp.where(vis_c, s_all[rows(r)] - slope[r] * dist_c, NEG_INF)
        e = jnp.exp(s - jnp.max(s, axis=-1, keepdims=True))
        p = jnp.where(vis_c, e / jnp.sum(e, axis=-1, keepdims=True), 0.0)
        psum = psum + p
        p_ref[rows(r), :nc] = p.astype(BF16)
    o_cmp = jnp.dot(p_ref[:, :nc], cv_ref[...], preferred_element_type=F32)

    p_hi = psum.astype(BF16)
    p_mid = (psum - p_hi.astype(F32)).astype(BF16)
    p_lo = (psum - p_hi.astype(F32) - p_mid.astype(F32)).astype(BF16)
    ov = ov_ref[...]
    imp = (jnp.dot(p_hi, ov, preferred_element_type=F32) + jnp.dot(p_mid, ov, preferred_element_type=F32)
           + jnp.dot(p_lo, ov, preferred_element_type=F32))
    nj = -(-n_slc // 8) * 8
    j_t = lax.broadcasted_iota(jnp.int32, (nj, qb), 0)
    cur = (t0 + lax.broadcasted_iota(jnp.int32, (nj, qb), 1)) // SEL_BLOCK
    vis_j = j_t <= cur
    forced = (j_t == 0) | (j_t == cur) | (j_t == cur - 1)
    score = jnp.where(forced, FORCE_SCORE, jnp.where(vis_j, imp.T[:nj], -FORCE_SCORE))
    score = jnp.where(j_t < n_slc, score, -jnp.inf)
    rank = jnp.zeros((nj, qb), jnp.int32)
    for jp in range(n_slc):
        row = score[jp:jp + 1, :]
        ahead = (row > score) | ((row == score) & (jp < j_t))
        rank = rank + ahead.astype(jnp.int32)
    selected = jnp.where((rank < min(N_SEL, n_slc)) & vis_j, 1.0, 0.0).astype(BF16)

    d0 = (lax.broadcasted_iota(jnp.int32, (qb, kb), 0) - lax.broadcasted_iota(jnp.int32, (qb, kb), 1))

    def flash_init():
        m_ref[...] = jnp.full(m_ref.shape, NEG_INF, F32)
        l_ref[...] = jnp.zeros(l_ref.shape, F32)
        acc_ref[...] = jnp.zeros(acc_ref.shape, F32)

    def flash_step(k_ref, v_ref, k0, mask, dist):
        k = k_ref[pl.ds(k0, kb), :].astype(BF16)
        v = v_ref[pl.ds(k0, kb), :].astype(BF16)
        s_all = lax.dot_general(q2, k, (((1,), (1,)), ((), ())), preferred_element_type=F32)
        alphas = []
        for r in range(r8):
            s = jnp.where(mask, s_all[rows(r)] - slope[r] * dist, NEG_INF)
            m_prev = m_ref[r]
            m_next = jnp.maximum(m_prev, jnp.max(s, axis=-1, keepdims=True))
            alpha = jnp.exp(m_prev - m_next)
            p = jnp.exp(s - jnp.concatenate([m_next] * (kb // LANE), axis=1))
            l_ref[r] = alpha * l_ref[r] + jnp.sum(p, axis=-1, keepdims=True)
            m_ref[r] = m_next
            p_ref[rows(r), :kb] = p.astype(BF16)
            alphas.append(alpha)
        pv = jnp.dot(p_ref[:, :kb], v, preferred_element_type=F32)
        for r in range(r8):
            acc_ref[r] = alphas[r] * acc_ref[r] + pv[rows(r)]

    flash_init()

    def sel_body(i, carry):
        k0 = pl.multiple_of(i * kb, kb)
        dist = d0 + (t0 - k0)
        blk_of_key = (lax.broadcasted_iota(jnp.int32, (nj, kb), 1) + k0) // SEL_BLOCK
        expand = jnp.where(blk_of_key == lax.broadcasted_iota(jnp.int32, (nj, kb), 0), 1.0, 0.0).astype(BF16)
        chosen = lax.dot_general(selected, expand, _TN, preferred_element_type=F32)
        flash_step(ks_ref, vs_ref, k0, (chosen > 0.5) & (dist >= 0), dist.astype(F32))
        return carry

    lax.fori_loop(0, (t0 + qb - 1) // kb + 1, sel_body, 0)
    o_sel = [acc_ref[r] / l_ref[r] for r in range(r8)]

    flash_init()

    def win_body(i, carry):
        k0 = pl.multiple_of(i * kb, kb)
        dist = d0 + (t0 - k0)
        flash_step(kw_ref, vw_ref, k0, (dist >= 0) & (dist <= WINDOW), dist.astype(F32))
        return carry

    lax.fori_loop(jnp.maximum(t0 - WINDOW, 0) // kb, (t0 + qb - 1) // kb + 1, win_body, 0)
    o_win = [acc_ref[r] / l_ref[r] for r in range(r8)]

    gates = jax.nn.sigmoid(gz_ref[...] + gb_ref[...])
    src = lax.broadcasted_iota(jnp.int32, (LANE, LANE), 0)
    dst = lax.broadcasted_iota(jnp.int32, (LANE, LANE), 1)
    pick = jnp.where((dst < 3 * r8) & (src == (dst // r8) * NSA_HEADS + g * r8 + dst % r8), 1.0, 0.0).astype(BF16)
    g_hi = gates.astype(BF16)
    g_mid = (gates - g_hi.astype(F32)).astype(BF16)
    g_lo = (gates - g_hi.astype(F32) - g_mid.astype(F32)).astype(BF16)
    gsel = (jnp.dot(g_hi, pick, preferred_element_type=F32) + jnp.dot(g_mid, pick, preferred_element_type=F32)
            + jnp.dot(g_lo, pick, preferred_element_type=F32))

    def gate(branch, r):
        k = branch * r8 + r
        return gsel[:, k:k + 1]

    for r in range(r8):
        o = gate(0, r) * o_cmp[rows(r)] + gate(1, r) * o_sel[r] + gate(2, r) * o_win[r]
        o_ref[:, r * hd:(r + 1) * hd] = o.astype(BF16)


def nsa_prompt_attention(z, ckv, gate_bias, b, t, qb=128, kb=512):
    nc = t // CMP_STRIDE
    nq = t // qb
    hd = NSA_HEAD_DIM
    n_slc = t // SEL_BLOCK
    slopes = jnp.exp2(-8.0 * (jnp.arange(NSA_HEADS, dtype=F32) + 1.0) / NSA_HEADS)
    ci = jnp.arange(nc)[:, None]
    sj = jnp.arange(LANE)[None, :]
    ov = ((CMP_STRIDE * ci < SEL_BLOCK * (sj + 1)) & (CMP_STRIDE * ci + CMP_LEN > SEL_BLOCK * sj)
          & (sj < n_slc)).astype(BF16)
    gb = jnp.pad(gate_bias, (0, LANE - gate_bias.shape[0])).reshape(1, LANE)
    kv_spec = lambda off: pl.BlockSpec((t, hd), lambda bi, g, qi: (bi, off + g))
    kv_col = (NSA_QD + 2 * NSA_KVD) // hd
    return pl.pallas_call(
        functools.partial(_nsa_prompt_kernel, t, qb, kb, nc),
        grid=(b, NSA_KV_HEADS, nq),
        in_specs=[
            pl.BlockSpec(memory_space=pltpu.SMEM),
            pl.BlockSpec((qb, NSA_Q_PER_KV * hd), lambda bi, g, qi: (bi * nq + qi, g)),
            kv_spec(kv_col), kv_spec(kv_col + 4), kv_spec(kv_col + 8), kv_spec(kv_col + 12),
            pl.BlockSpec((qb, LANE), lambda bi, g, qi: (bi * nq + qi, Z_GATE_COL)),
            pl.BlockSpec((1, LANE), lambda bi, g, qi: (0, 0)),
            pl.BlockSpec((None, None, None, nc, hd), lambda bi, g, qi: (bi, 0, g, 0, 0)),
            pl.BlockSpec((None, None, None, nc, hd), lambda bi, g, qi: (bi, 1, g, 0, 0)),
            pl.BlockSpec((nc, LANE), lambda bi, g, qi: (0, 0)),
        ],
        out_specs=pl.BlockSpec((qb, NSA_Q_PER_KV * hd), lambda bi, g, qi: (bi * nq + qi, g)),
        out_shape=jax.ShapeDtypeStruct((b * t, NSA_QD), BF16),
        scratch_shapes=[pltpu.VMEM((NSA_Q_PER_KV, qb, LANE), F32), pltpu.VMEM((NSA_Q_PER_KV, qb, LANE), F32),
                        pltpu.VMEM((NSA_Q_PER_KV, qb, hd), F32),
                        pltpu.VMEM((NSA_Q_PER_KV * qb, max(kb, nc)), BF16)],
        compiler_params=_params("parallel", "parallel", "arbitrary"),
        name="nsa_prompt_attn",
    )(slopes, z, z, z, z, z, z, gb, ckv, ckv, ov)


_NT = (((1,), (1,)), ((), ()))
_TN = (((0,), (0,)), ((), ()))


GATHER_PAGES = 4


def _page_gather_kernel(n_steps, pt_ref, *refs):
    page_refs, tail_ref, o_ref = refs[:GATHER_PAGES], refs[GATHER_PAGES], refs[GATHER_PAGES + 1]
    p = pl.program_id(1)

    @pl.when(p < n_steps)
    def _():
        for k, x_ref in enumerate(page_refs):
            for c in range(2):
                for g in range(NSA_KV_HEADS):
                    col = (c * NSA_KV_HEADS + g) * NSA_HEAD_DIM
                    o_ref[k * PAGE_SIZE:(k + 1) * PAGE_SIZE, col:col + NSA_HEAD_DIM] = x_ref[:, c, g, :]

    @pl.when(p == n_steps)
    def _():
        o_ref[...] = tail_ref[...]


def page_gather(cache, page_table, tail):
    nb, n_pages = page_table.shape
    step_rows = GATHER_PAGES * PAGE_SIZE
    n_steps = n_pages // GATHER_PAGES
    w = tail.shape[2]

    def page_spec(k):
        def index(b, p, pt):
            return (pt[b, jnp.minimum(p, n_steps - 1) * GATHER_PAGES + k], 0, 0, 0, 0)
        return pl.BlockSpec((None, PAGE_SIZE, 2, NSA_KV_HEADS, NSA_HEAD_DIM), index)

    return pl.pallas_call(
        functools.partial(_page_gather_kernel, n_steps),
        grid_spec=pltpu.PrefetchScalarGridSpec(
            num_scalar_prefetch=1,
            grid=(nb, n_steps + 1),
            in_specs=[page_spec(k) for k in range(GATHER_PAGES)]
            + [pl.BlockSpec((None, step_rows, w), lambda b, p, pt: (b, 0, 0))],
            out_specs=pl.BlockSpec((step_rows, w), lambda b, p, pt: (b * (n_steps + 1) + p, 0)),
        ),
        out_shape=jax.ShapeDtypeStruct((nb * (n_steps + 1) * step_rows, w), F32),
        compiler_params=_params("parallel", "arbitrary"),
        name="nsa_page_gather",
    )(page_table, *([cache] * GATHER_PAGES), tail)


def _row_slopes(slope_ref, g, n_rows, t):
    r_idx = lax.broadcasted_iota(jnp.int32, (n_rows, 1), 0) // t
    col = jnp.zeros((n_rows, 1), F32)
    for r in range(NSA_Q_PER_KV):
        col = jnp.where(r_idx == r, slope_ref[g * NSA_Q_PER_KV + r], col)
    return col


def _nsa_sample_select_kernel(t, pos0, nc, n_cmp, n_slc, slope_ref, q_ref, ck_ref, cv_ref, ov_ref,
                              ocmp_ref, idx_ref, valid_ref):
    g = pl.program_id(0)
    r8 = NSA_Q_PER_KV
    n_rows = r8 * t
    nsp = ov_ref.shape[1]
    q2 = (q_ref[...] * (NSA_HEAD_DIM ** -0.5)).astype(BF16)
    slope = _row_slopes(slope_ref, g, n_rows, t)
    tq = pos0 + lax.broadcasted_iota(jnp.int32, (n_rows, 1), 0) % t
    n_i = lax.broadcasted_iota(jnp.int32, (n_rows, nc), 1)
    dist_ci = tq - (CMP_STRIDE * n_i + (CMP_LEN - 1))
    vis_c = (dist_ci >= 0) & (n_i < n_cmp)
    s = lax.dot_general(q2, ck_ref[...], _NT, preferred_element_type=F32)
    s = jnp.where(vis_c, s - slope * dist_ci.astype(F32), NEG_INF)
    e = jnp.exp(s - jnp.max(s, axis=-1, keepdims=True))
    p = jnp.where(vis_c, e / jnp.sum(e, axis=-1, keepdims=True), 0.0)
    ocmp_ref[...] = jnp.dot(p.astype(BF16), cv_ref[...], preferred_element_type=F32)

    p_hi = p.astype(BF16)
    p_mid = (p - p_hi.astype(F32)).astype(BF16)
    p_lo = (p - p_hi.astype(F32) - p_mid.astype(F32)).astype(BF16)
    ov = ov_ref[...]
    imp = (jnp.dot(p_hi, ov, preferred_element_type=F32) + jnp.dot(p_mid, ov, preferred_element_type=F32)
           + jnp.dot(p_lo, ov, preferred_element_type=F32))
    sh = t
    while sh < n_rows:
        imp = imp + pltpu.roll(imp, sh, 0)
        sh *= 2
    imp = imp[0:8]
    j_i = lax.broadcasted_iota(jnp.int32, (8, nsp), 1)
    cur = (pos0 + lax.broadcasted_iota(jnp.int32, (8, 1), 0) % t) // SEL_BLOCK
    vis_j = j_i <= cur
    forced = (j_i == 0) | (j_i == cur) | (j_i == cur - 1)
    score = jnp.where(forced, FORCE_SCORE, jnp.where(vis_j, imp, -FORCE_SCORE))
    score = jnp.where(j_i < n_slc, score, -jnp.inf)
    rank = jnp.zeros((8, nsp), jnp.int32)
    for jp in range(n_slc):
        col = score[:, jp:jp + 1]
        ahead = (col > score) | ((col == score) & (jp < j_i))
        rank = rank + ahead.astype(jnp.int32)
    lane = lax.broadcasted_iota(jnp.int32, (8, LANE), 1)
    idx = jnp.zeros((8, LANE), F32)
    valid = jnp.zeros((8, LANE), F32)
    j_f = j_i.astype(F32)
    for k in range(min(N_SEL, n_slc)):
        hit = (rank == k) & (j_i < n_slc)
        idx_k = jnp.sum(jnp.where(hit, j_f, 0.0), axis=-1, keepdims=True)
        valid_k = jnp.sum(jnp.where(hit & vis_j, 1.0, 0.0), axis=-1, keepdims=True)
        idx = jnp.where(lane == k, idx_k, idx)
        valid = jnp.where(lane == k, valid_k, valid)
    idx_ref[...] = idx.astype(jnp.int32)
    valid_ref[...] = valid.astype(jnp.int32)


def nsa_sample_select(q, ckv, nb, t, pos0, n_cmp, n_slc):
    g4, hd = NSA_KV_HEADS, NSA_HEAD_DIM
    nc = ckv.shape[3]
    n_rows = NSA_Q_PER_KV * t
    nsp = -(-n_slc // LANE) * LANE
    slopes = jnp.exp2(-8.0 * (jnp.arange(NSA_HEADS, dtype=F32) + 1.0) / NSA_HEADS)
    ci = jnp.arange(nc)[:, None]
    sj = jnp.arange(nsp)[None, :]
    ov = ((CMP_STRIDE * ci < SEL_BLOCK * (sj + 1)) & (CMP_STRIDE * ci + CMP_LEN > SEL_BLOCK * sj)
          & (sj < n_slc) & (ci < n_cmp)).astype(BF16)
    return pl.pallas_call(
        functools.partial(_nsa_sample_select_kernel, t, pos0, nc, n_cmp, n_slc),
        grid=(g4, nb),
        in_specs=[
            pl.BlockSpec(memory_space=pltpu.SMEM),
            pl.BlockSpec((None, None, n_rows, hd), lambda g, b: (g, b, 0, 0)),
            pl.BlockSpec((None, None, None, nc, hd), lambda g, b: (b, 0, g, 0, 0)),
            pl.BlockSpec((None, None, None, nc, hd), lambda g, b: (b, 1, g, 0, 0)),
            pl.BlockSpec((nc, nsp), lambda g, b: (0, 0)),
        ],
        out_specs=[
            pl.BlockSpec((None, None, n_rows, hd), lambda g, b: (g, b, 0, 0)),
            pl.BlockSpec((None, None, 8, LANE), lambda g, b: (g, b, 0, 0)),
            pl.BlockSpec((None, None, 8, LANE), lambda g, b: (g, b, 0, 0)),
        ],
        out_shape=[jax.ShapeDtypeStruct((g4, nb, n_rows, hd), F32),
                   jax.ShapeDtypeStruct((g4, nb, 8, LANE), jnp.int32),
                   jax.ShapeDtypeStruct((g4, nb, 8, LANE), jnp.int32)],
        compiler_params=_params("parallel", "parallel"),
        name="nsa_sample_select",
    )(slopes, q, ckv, ckv, ov)


def _nsa_sample_attn_kernel(nb, t, pos0, n_pages, wb, idx_ref, valid_ref, pt_ref, slope_ref, q_ref, ocmp_ref,
                            gl_ref, gb_ref, kwin_ref, vwin_ref, kwt_ref, vwt_ref, cache_ref, tail_ref, o_ref,
                            kbuf, vbuf, q2_ref, osel_ref, sem):
    b = pl.program_id(0)
    g = pl.program_id(1)
    r8 = NSA_Q_PER_KV
    hd = NSA_HEAD_DIM
    n_rows = r8 * t
    n_sel = kbuf.shape[1]
    past_blocks = n_pages * (PAGE_SIZE // SEL_BLOCK)
    per_page = PAGE_SIZE // SEL_BLOCK

    def sel_entry(tt, k):
        return ((g * nb + b) * t + tt) * n_sel + k

    def block_copies(tt, k, in_tail):
        i = idx_ref[sel_entry(tt, k)]
        out = []
        for c, buf in ((0, kbuf), (1, vbuf)):
            col = c * NSA_KV_HEADS + g
            if in_tail:
                src = tail_ref.at[b, :, pl.ds(col * hd, hd)]
            else:
                ic = jnp.minimum(i, past_blocks - 1)
                page = pt_ref[b * n_pages + ic // per_page]
                src = cache_ref.at[page, pl.ds((ic % per_page) * SEL_BLOCK, SEL_BLOCK), c, g, :]
            out.append(pltpu.make_async_copy(src, buf.at[tt, k], sem))
        return out

    def for_each_block(fn):
        for tt in range(t):
            for k in range(n_sel):
                in_tail = idx_ref[sel_entry(tt, k)] >= past_blocks

                @pl.when(in_tail)
                def _():
                    for cp in block_copies(tt, k, True):
                        fn(cp)

                @pl.when(jnp.logical_not(in_tail))
                def _():
                    for cp in block_copies(tt, k, False):
                        fn(cp)

    for_each_block(lambda cp: cp.start())

    q2_ref[...] = q_ref[...] * (hd ** -0.5)
    q2 = q2_ref[...].astype(BF16)
    slope = _row_slopes(slope_ref, g, n_rows, t)
    tq = pos0 + lax.broadcasted_iota(jnp.int32, (n_rows, 1), 0) % t

    kw = jnp.concatenate([kwin_ref[...], kwt_ref[...]], axis=0).astype(BF16)
    vw = jnp.concatenate([vwin_ref[...], vwt_ref[...]], axis=0).astype(BF16)
    nk = kw.shape[0]
    s_pos = (pos0 - wb) + lax.broadcasted_iota(jnp.int32, (n_rows, nk), 1)
    dist = tq - s_pos
    mask = (dist >= 0) & (dist <= WINDOW) & (s_pos >= 0)
    s = lax.dot_general(q2, kw, _NT, preferred_element_type=F32)
    s = jnp.where(mask, s - slope * dist.astype(F32), NEG_INF)
    e = jnp.exp(s - jnp.max(s, axis=-1, keepdims=True))
    p = e / jnp.sum(e, axis=-1, keepdims=True)
    o_win = jnp.dot(p.astype(BF16), vw, preferred_element_type=F32)

    for_each_block(lambda cp: cp.wait())

    nkeys = n_sel * SEL_BLOCK
    key_lane = lax.broadcasted_iota(jnp.int32, (1, nkeys), 1)
    slope8 = _row_slopes(slope_ref, g, r8, 1)
    for tt in range(t):
        blk = jnp.zeros((1, nkeys), jnp.int32)
        ok = jnp.zeros((1, nkeys), jnp.int32)
        for k in range(n_sel):
            here = key_lane // SEL_BLOCK == k
            blk = jnp.where(here, idx_ref[sel_entry(tt, k)], blk)
            ok = jnp.where(here, valid_ref[sel_entry(tt, k)], ok)
        dist = (pos0 + tt) - (blk * SEL_BLOCK + key_lane % SEL_BLOCK)
        mask = (dist >= 0) & (ok > 0)
        q_t = q2_ref[pl.ds(tt, r8, stride=t), :].astype(BF16)
        k_t = kbuf[tt].reshape(nkeys, hd).astype(BF16)
        v_t = vbuf[tt].reshape(nkeys, hd).astype(BF16)
        s = lax.dot_general(q_t, k_t, _NT, preferred_element_type=F32)
        s = jnp.where(mask, s - slope8 * dist.astype(F32), NEG_INF)
        e = jnp.exp(s - jnp.max(s, axis=-1, keepdims=True))
        p = e / jnp.sum(e, axis=-1, keepdims=True)
        osel_ref[pl.ds(tt, r8, stride=t), :] = jnp.dot(p.astype(BF16), v_t, preferred_element_type=F32)

    gates = jax.nn.sigmoid(gl_ref[...] + gb_ref[...])
    o = gates[:, 0:1] * ocmp_ref[...] + gates[:, 1:2] * osel_ref[...] + gates[:, 2:3] * o_win
    o_ref[...] = o.astype(BF16)


def nsa_sample_attention(q, o_cmp, gl, gb, idx, valid, page_table, cache_sel, tail_sel, cache_win, tail_win,
                         nb, t, pos0):
    g4, hd = NSA_KV_HEADS, NSA_HEAD_DIM
    n_rows = NSA_Q_PER_KV * t
    n_pages = page_table.shape[1]
    n_sel = idx.shape[-1]
    wb = cache_win.shape[1]
    slopes = jnp.exp2(-8.0 * (jnp.arange(NSA_HEADS, dtype=F32) + 1.0) / NSA_HEADS)
    row_spec = lambda w: pl.BlockSpec((None, None, n_rows, w), lambda b, g, *_: (g, b, 0, 0))
    win_spec = lambda rows, c: pl.BlockSpec((None, rows, hd), lambda b, g, *_: (b, 0, c * g4 + g))
    return pl.pallas_call(
        functools.partial(_nsa_sample_attn_kernel, nb, t, pos0, n_pages, wb),
        grid_spec=pltpu.PrefetchScalarGridSpec(
            num_scalar_prefetch=3,
            grid=(nb, g4),
            in_specs=[
                pl.BlockSpec(memory_space=pltpu.SMEM),
                row_spec(hd), row_spec(hd), row_spec(3),
                pl.BlockSpec((None, n_rows, 3), lambda b, g, *_: (g, 0, 0)),
                win_spec(wb, 0), win_spec(wb, 1), win_spec(LANE, 0), win_spec(LANE, 1),
                pl.BlockSpec(memory_space=pl.ANY), pl.BlockSpec(memory_space=pl.ANY),
            ],
            out_specs=row_spec(hd),
            scratch_shapes=[pltpu.VMEM((t, n_sel, SEL_BLOCK, hd), F32), pltpu.VMEM((t, n_sel, SEL_BLOCK, hd), F32),
                            pltpu.VMEM((n_rows, hd), F32), pltpu.VMEM((n_rows, hd), F32),
                            pltpu.SemaphoreType.DMA(())],
        ),
        out_shape=jax.ShapeDtypeStruct((g4, nb, n_rows, hd), BF16),
        compiler_params=_params("arbitrary", "arbitrary"),
        name="nsa_sample_attn",
    )(idx.reshape(-1), valid.reshape(-1), page_table.reshape(-1), slopes, q, o_cmp, gl, gb,
      cache_win, cache_win, tail_win, tail_win, cache_sel, tail_sel)


def nsa_sample_mix(z, nb, t, cache_cmp, cache_sel, cache_win, page_table, gate_bias, w_cmp, pe_cmp):
    g4, r8, hd = NSA_KV_HEADS, NSA_Q_PER_KV, NSA_HEAD_DIM
    n_pages = page_table.shape[1]
    past = n_pages * PAGE_SIZE
    wb = cache_win.shape[1]
    t_all = past + t
    tp = -(-t_all // SEL_BLOCK) * SEL_BLOCK
    n_slc = tp // SEL_BLOCK
    n_cmp = tp // CMP_STRIDE - 1
    gather_rows = (n_pages + GATHER_PAGES) * PAGE_SIZE
    nc = gather_rows // CMP_STRIDE
    kv = z[:, NSA_QD:NSA_QD + 6 * NSA_KVD].reshape(nb, t, 3, 2 * NSA_KVD)

    tail_cmp = jnp.pad(kv[:, :, 0], ((0, 0), (0, gather_rows - past - t), (0, 0)))
    rows = page_gather(cache_cmp, page_table, tail_cmp)
    ckv = compress(rows, nb, nc, 0, w_cmp, pe_cmp)

    q = z[:, :NSA_QD].reshape(nb, t, g4, r8, hd).transpose(2, 0, 3, 1, 4).reshape(g4, nb, r8 * t, hd)
    o_cmp, idx, valid = nsa_sample_select(q, ckv, nb, t, past, n_cmp, n_slc)
    n_sel = min(N_SEL, n_slc)
    idx = idx[:, :, :t, :n_sel]
    valid = valid[:, :, :t, :n_sel]

    gl = z[:, NSA_QD + 6 * NSA_KVD:NSA_QD + 6 * NSA_KVD + 3 * NSA_HEADS].reshape(nb, t, 3, g4, r8)
    gl = gl.transpose(3, 0, 4, 1, 2).reshape(g4, nb, r8 * t, 3)
    gb = jnp.broadcast_to(gate_bias.reshape(3, g4, r8, 1), (3, g4, r8, t)).transpose(1, 2, 3, 0).reshape(g4, r8 * t, 3)
    tail_sel = jnp.pad(kv[:, :, 1], ((0, 0), (0, SEL_BLOCK - t), (0, 0)))
    tail_win = jnp.pad(kv[:, :, 2], ((0, 0), (0, LANE - t), (0, 0)))
    o = nsa_sample_attention(q, o_cmp, gl, gb, idx, valid, page_table,
                             cache_sel, tail_sel,
                             cache_win.reshape(nb, wb, 2 * NSA_KVD), tail_win, nb, t, past)
    o = o.reshape(g4, nb, r8, t, hd).transpose(1, 3, 0, 2, 4).reshape(nb * t, NSA_QD)
    kv6 = kv.reshape(nb, t, 3, 2, g4, hd)
    win_new = jnp.concatenate([cache_win, kv6[:, :, 2]], axis=1)[:, t:]
    return o, kv6[:, :, 0], kv6[:, :, 1], win_new


GLA_QK = GLA_HEADS * GLA_DK
GLA_VD = GLA_HEADS * GLA_DV
GLA_SUPER = 256


def _chunk_scan(la, c):
    n = la.shape[0]
    ri = lax.broadcasted_iota(jnp.int32, la.shape, 0) % c
    b = la
    s = 1
    while s < c:
        b = b + jnp.where(ri >= s, pltpu.roll(b, s, 0), 0.0)
        s *= 2
    tot = jnp.where(ri == c - 1, b, 0.0)
    s = 1
    while s < c:
        tot = tot + jnp.where(ri + s < c, pltpu.roll(tot, n - s, 0), 0.0)
        s *= 2
    return b, tot


def _gla_gates(a_rows, wa_ref, ba_ref):
    x = jnp.dot(a_rows.astype(BF16), wa_ref[...], preferred_element_type=F32) + ba_ref[...]
    return jax.nn.log_sigmoid(x) / GLA_TAU


def _gla_finish(o, r, g_ref):
    return (_norm_rows(o, g_ref[...]) * (r * jax.nn.sigmoid(r))).astype(BF16)


def _gla_prompt_kernel(t, q_ref, k_ref, v_ref, r_ref, a_ref, wa_ref, ba_ref, g_ref, o_ref, s_ref, st_ref, oc_ref):
    c = GLA_CHUNK
    sc = GLA_SUPER
    st_ref[...] = jnp.zeros(st_ref.shape, F32)
    ti = lax.broadcasted_iota(jnp.int32, (sc, sc), 0)
    si = lax.broadcasted_iota(jnp.int32, (sc, sc), 1)
    causal = (ti // c == si // c) & (si <= ti)

    def body(i, carry):
        r0 = pl.multiple_of(i * sc, sc)
        la = _gla_gates(a_ref[pl.ds(r0, sc), :], wa_ref, ba_ref)
        b, b_last = _chunk_scan(la, c)
        q = q_ref[pl.ds(r0, sc), :] * (GLA_DK ** -0.5)
        k = k_ref[pl.ds(r0, sc), :]
        qe = (q * jnp.exp(b)).astype(BF16)
        ke = (k * jnp.exp(-b)).astype(BF16)
        kd = (k * jnp.exp(b_last - b)).astype(BF16)
        vb = v_ref[pl.ds(r0, sc), :].astype(BF16)
        att = jnp.where(causal, lax.dot_general(qe, ke, _NT, preferred_element_type=F32), 0.0)
        o_intra = jnp.dot(att.astype(BF16), vb, preferred_element_type=F32)
        for j in range(sc // c):
            rows = slice(j * c, (j + 1) * c)
            st = st_ref[...]
            o_inter = lax.dot_general(qe[rows], st.astype(BF16), _NT, preferred_element_type=F32)
            oc_ref[rows, :] = o_intra[rows] + o_inter
            dec = jnp.exp(b_last[j * c:j * c + 1, :])
            st_ref[...] = dec * st + lax.dot_general(vb[rows], kd[rows], _TN, preferred_element_type=F32)
        o_ref[pl.ds(r0, sc), :] = _gla_finish(oc_ref[...], r_ref[pl.ds(r0, sc), :], g_ref)
        return carry

    lax.fori_loop(0, t // sc, body, 0)
    s_ref[...] = st_ref[...].T


def gla_prompt(z, b, t, w_alpha, b_alpha, norm_g):
    dk, dv, h = GLA_DK, GLA_DV, GLA_HEADS
    wa = jnp.pad(w_alpha, ((0, LANE - GLA_RANK), (0, 0))).astype(BF16)
    return pl.pallas_call(
        functools.partial(_gla_prompt_kernel, t),
        grid=(b, h),
        in_specs=[
            pl.BlockSpec((t, dk), lambda bi, hi: (bi, hi)),
            pl.BlockSpec((t, dk), lambda bi, hi: (bi, h + hi)),
            pl.BlockSpec((t, dv), lambda bi, hi: (bi, 2 * GLA_QK // dv + hi)),
            pl.BlockSpec((t, dv), lambda bi, hi: (bi, (2 * GLA_QK + GLA_VD) // dv + hi)),
            pl.BlockSpec((t, LANE), lambda bi, hi: (bi, (2 * GLA_QK + 2 * GLA_VD) // LANE)),
            pl.BlockSpec((LANE, dk), lambda bi, hi: (0, hi)),
            pl.BlockSpec((1, dk), lambda bi, hi: (0, hi)),
            pl.BlockSpec((1, dv), lambda bi, hi: (0, 0)),
        ],
        out_specs=[
            pl.BlockSpec((t, dv), lambda bi, hi: (bi, hi)),
            pl.BlockSpec((None, None, dk, dv), lambda bi, hi: (bi, hi, 0, 0)),
        ],
        out_shape=[jax.ShapeDtypeStruct((b * t, GLA_VD), BF16), jax.ShapeDtypeStruct((b, h, dk, dv), F32)],
        scratch_shapes=[pltpu.VMEM((dv, dk), F32), pltpu.VMEM((GLA_SUPER, dv), F32)],
        compiler_params=_params("parallel", "arbitrary"),
        name="gla_prompt",
    )(z, z, z, z, z, wa, b_alpha.reshape(1, -1), norm_g.reshape(1, -1))


def _gla_sample_kernel(nb, t, q_ref, k_ref, v_ref, r_ref, a_ref, wa_ref, ba_ref, g_ref, s0_ref, o_ref, s_ref):
    n = nb * t
    la = _gla_gates(a_ref[...], wa_ref, ba_ref)
    b, b_last = _chunk_scan(la, t)
    q = q_ref[...] * (GLA_DK ** -0.5)
    k = k_ref[...]
    qe = q * jnp.exp(b)
    ke = (k * jnp.exp(-b)).astype(BF16)
    kd = k * jnp.exp(b_last - b)
    vb = v_ref[...].astype(BF16)
    ti = lax.broadcasted_iota(jnp.int32, (n, n), 0)
    si = lax.broadcasted_iota(jnp.int32, (n, n), 1)
    causal = (ti // t == si // t) & (si <= ti)
    att = jnp.where(causal, lax.dot_general(qe.astype(BF16), ke, _NT, preferred_element_type=F32), 0.0)
    o = jnp.dot(att.astype(BF16), vb, preferred_element_type=F32)
    row = lax.broadcasted_iota(jnp.int32, (n, 1), 0) // t
    for i in range(nb):
        mine = row == i
        st = s0_ref[i].T
        qe_i = jnp.where(mine, qe, 0.0).astype(BF16)
        kd_i = jnp.where(mine, kd, 0.0).astype(BF16)
        o = o + lax.dot_general(qe_i, st.astype(BF16), _NT, preferred_element_type=F32)
        dec = jnp.exp(b_last[i * t:i * t + 1, :])
        s_ref[i] = (dec * st + lax.dot_general(vb, kd_i, _TN, preferred_element_type=F32)).T
    o_ref[...] = _gla_finish(o, r_ref[...], g_ref)


def gla_sample(z, nb, t, s0, w_alpha, b_alpha, norm_g):
    dk, dv, h = GLA_DK, GLA_DV, GLA_HEADS
    n = nb * t
    wa = jnp.pad(w_alpha, ((0, LANE - GLA_RANK), (0, 0))).astype(BF16)
    return pl.pallas_call(
        functools.partial(_gla_sample_kernel, nb, t),
        grid=(h,),
        in_specs=[
            pl.BlockSpec((n, dk), lambda hi: (0, hi)),
            pl.BlockSpec((n, dk), lambda hi: (0, h + hi)),
            pl.BlockSpec((n, dv), lambda hi: (0, 2 * GLA_QK // dv + hi)),
            pl.BlockSpec((n, dv), lambda hi: (0, (2 * GLA_QK + GLA_VD) // dv + hi)),
            pl.BlockSpec((n, LANE), lambda hi: (0, (2 * GLA_QK + 2 * GLA_VD) // LANE)),
            pl.BlockSpec((LANE, dk), lambda hi: (0, hi)),
            pl.BlockSpec((1, dk), lambda hi: (0, hi)),
            pl.BlockSpec((1, dv), lambda hi: (0, 0)),
            pl.BlockSpec((nb, None, dk, dv), lambda hi: (0, hi, 0, 0)),
        ],
        out_specs=[
            pl.BlockSpec((n, dv), lambda hi: (0, hi)),
            pl.BlockSpec((nb, None, dk, dv), lambda hi: (0, hi, 0, 0)),
        ],
        out_shape=[jax.ShapeDtypeStruct((n, GLA_VD), BF16), jax.ShapeDtypeStruct((nb, h, dk, dv), F32)],
        compiler_params=_params("parallel"),
        name="gla_sample",
    )(z, z, z, z, z, wa, b_alpha.reshape(1, -1), norm_g.reshape(1, -1), s0)


def nsa_prompt_mix(z, kv, b, t, gate_bias, w_cmp, pe_cmp):
    ckv = compress(z, b, t // CMP_STRIDE, NSA_QD // NSA_HEAD_DIM, w_cmp, pe_cmp)
    o = nsa_prompt_attention(z, ckv, gate_bias, b, t)
    kv = kv.reshape(3, b, t, 2, NSA_KV_HEADS, NSA_HEAD_DIM)
    return o, kv[0], kv[1], kv[2, :, t - min(WINDOW, t):]


def _pad_cols(w, mult):
    n = w.shape[-1]
    return jnp.pad(w, ((0, 0), (0, -(-n // mult) * mult - n)))


def kernel(x_prompt, x_sample, p_prompt, p_sample, cache_cmp_kv, cache_sel_kv, cache_win_kv, state_gla, page_table, ffn1_norm, ffn1_w_gu, ffn1_w_down, mix_norm, nsa_w_in, nsa_gate_bias, nsa_w_cmp, nsa_pe_cmp, nsa_w_out, gla_w_in, gla_w_alpha, gla_b_alpha, gla_norm, gla_w_out, ffn2_norm, ffn2_w_gu, ffn2_w_down, ple_norm, ple_w_gate, ple_w_proj, final_norm):
    bp, tp, d = x_prompt.shape
    bs, ts, _ = x_sample.shape
    mp, ms = bp * tp, bs * ts
    xp = x_prompt.reshape(mp, d)
    xs = x_sample.reshape(ms, d)
    tm_p = _row_tile(mp, 512)
    tm_p2 = _row_tile(mp, 512)

    cmp_p, sel_p, win_p, gla_p = [], [], [], []
    cmp_s, sel_s, win_s, gla_s = [], [], [], []
    for i in range(DEPTH):
        xs, wa, wu, wd = ffn_cast(xs, ffn1_norm[i], ffn1_w_gu, ffn1_w_down, i)
        xp = ffn(xp, ffn1_norm[i], wa, wu, wd, tm_p)
        j = i // 2
        if i % 2 == 0:
            w_in = _pad_cols(nsa_w_in[j], 512).astype(BF16)
            w_out = nsa_w_out[j].astype(BF16)
            zp, kvp = nsa_in_proj(xp, mix_norm[i], w_in, tm_p2)
            zs = norm_matmul(xs, mix_norm[i], w_in, ms)
            op, kc, kl, kw = nsa_prompt_mix(zp, kvp, bp, tp, nsa_gate_bias[j], nsa_w_cmp[j], nsa_pe_cmp[j])
            os_, kc2, kl2, kw2 = nsa_sample_mix(zs, bs, ts, cache_cmp_kv[j], cache_sel_kv[j], cache_win_kv[j],
                                                page_table, nsa_gate_bias[j], nsa_w_cmp[j], nsa_pe_cmp[j])
            cmp_p.append(kc)
            sel_p.append(kl)
            win_p.append(kw)
            cmp_s.append(kc2)
            sel_s.append(kl2)
            win_s.append(kw2)
        else:
            w_in = _pad_cols(gla_w_in[j], 512).astype(BF16)
            w_out = gla_w_out[j].astype(BF16)
            zp = norm_matmul(xp, mix_norm[i], w_in, tm_p2)
            zs = norm_matmul(xs, mix_norm[i], w_in, ms)
            op, sp = gla_prompt(zp, bp, tp, gla_w_alpha[j], gla_b_alpha[j], gla_norm[j])
            os_, ss = gla_sample(zs, bs, ts, state_gla[j], gla_w_alpha[j], gla_b_alpha[j], gla_norm[j])
            gla_p.append(sp)
            gla_s.append(ss)
        xp = matmul_residual(op, w_out, xp, tm_p)
        xs = matmul_residual(os_, w_out, xs, ms)
        xs, wa, wu, wd = ffn_cast(xs, ffn2_norm[i], ffn2_w_gu, ffn2_w_down, i)
        xp = ffn(xp, ffn2_norm[i], wa, wu, wd, tm_p)
        w_pg = ple_w_gate[i].astype(BF16)
        w_pp = ple_w_proj[i].astype(BF16)
        xp = ple(xp, p_prompt[i].reshape(mp, -1), ple_norm[i], w_pg, w_pp, tm_p2)
        xs = ple(xs, p_sample[i].reshape(ms, -1), ple_norm[i], w_pg, w_pp, ms)
    y_prompt = rmsnorm(xp, final_norm, tm_p2).reshape(bp, tp, d)
    y_sample = rmsnorm(xs, final_norm, ms).reshape(bs, ts, d)
    return (y_prompt, y_sample, jnp.stack(cmp_p), jnp.stack(sel_p), jnp.stack(win_p), jnp.stack(gla_p),
            jnp.stack(cmp_s), jnp.stack(sel_s), jnp.stack(win_s), jnp.stack(gla_s))
```

```python
import functools
import math

import jax
import jax.numpy as jnp
from jax import lax
from jax.experimental import pallas as pl
from jax.experimental.pallas import tpu as pltpu

F32 = jnp.float32
BF16 = jnp.bfloat16

D_MODEL = 4096
DEPTH = 2
PAST_LEN = 16384
PAGE_SIZE = 128
NSA_HEADS = 32
NSA_HEAD_DIM = 128
NSA_KV_HEADS = 4
NSA_Q_PER_KV = 8
CMP_STRIDE = 16
CMP_LEN = 32
SEL_BLOCK = 64
N_SEL = 16
WINDOW = 512
NSA_QBLOCK = 32
NSA_QD = NSA_HEADS * NSA_HEAD_DIM
NSA_KVD = NSA_KV_HEADS * NSA_HEAD_DIM
GLA_HEADS = 8
GLA_DK = 256
GLA_DV = 512
GLA_RANK = 16
GLA_TAU = 16.0
GLA_CHUNK = 32
D_FF = 11008
EPS = 1e-6
NEG_INF = -1e30
FORCE_SCORE = 1e30

V7X_VMEM_LIMIT_BYTES = 56 * 1024 * 1024
LANE = 128


def _params(*sem):
    return pltpu.CompilerParams(dimension_semantics=sem, vmem_limit_bytes=V7X_VMEM_LIMIT_BYTES)


def _norm_rows(x, g):
    ms = jnp.mean(x * x, axis=-1, keepdims=True)
    return (x * lax.rsqrt(ms + EPS)) * g


def _row_tile(m, want):
    return want if m % want == 0 else m


def _ffn_kernel(x_ref, g_ref, wa_ref, wu_ref, wd_ref, o_ref, h_ref):
    @pl.when(pl.program_id(1) == 0)
    def _():
        x = x_ref[...]
        h_ref[...] = _norm_rows(x, g_ref[...]).astype(BF16)
        o_ref[...] = x

    h = h_ref[...]
    a = jnp.dot(h, wa_ref[...], preferred_element_type=F32)
    u = jnp.dot(h, wu_ref[...], preferred_element_type=F32)
    act = (0.5 * (a * jax.nn.sigmoid(a)) * u).astype(BF16)
    o_ref[...] += jnp.dot(act, wd_ref[...], preferred_element_type=F32)


def ffn(x, g, wa, wu, wd, tm, tf=256):
    m, d = x.shape
    return pl.pallas_call(
        _ffn_kernel,
        grid=(m // tm, D_FF // tf),
        in_specs=[
            pl.BlockSpec((tm, d), lambda i, j: (i, 0), pipeline_mode=pl.Buffered(1)),
            pl.BlockSpec((1, d), lambda i, j: (0, 0)),
            pl.BlockSpec((d, tf), lambda i, j: (0, j)),
            pl.BlockSpec((d, tf), lambda i, j: (0, j)),
            pl.BlockSpec((tf, d), lambda i, j: (j, 0)),
        ],
        out_specs=pl.BlockSpec((tm, d), lambda i, j: (i, 0)),
        out_shape=jax.ShapeDtypeStruct((m, d), F32),
        scratch_shapes=[pltpu.VMEM((tm, d), BF16)],
        compiler_params=_params("parallel", "arbitrary"),
        name="ffn",
    )(x, g.reshape(1, d), wa, wu, wd)


def _ffn_cast_kernel(x_ref, g_ref, wa_ref, wu_ref, wd_ref, o_ref, wa_out, wu_out, wd_out, h_ref):
    wa_out[...] = wa_ref[...].astype(BF16)
    wu_out[...] = wu_ref[...].astype(BF16)
    wd_out[...] = wd_ref[...].astype(BF16)
    _ffn_kernel(x_ref, g_ref, wa_out, wu_out, wd_out, o_ref, h_ref)


def ffn_cast(x, g, w_gu, w_down, layer, tf=256):
    m, d = x.shape
    nf = D_FF // tf
    return pl.pallas_call(
        _ffn_cast_kernel,
        grid=(1, nf),
        in_specs=[
            pl.BlockSpec((m, d), lambda i, j: (0, 0)),
            pl.BlockSpec((1, d), lambda i, j: (0, 0)),
            pl.BlockSpec((None, d, tf), lambda i, j: (layer, 0, j)),
            pl.BlockSpec((None, d, tf), lambda i, j: (layer, 0, j + nf)),
            pl.BlockSpec((None, tf, d), lambda i, j: (layer, j, 0)),
        ],
        out_specs=[
            pl.BlockSpec((m, d), lambda i, j: (0, 0)),
            pl.BlockSpec((d, tf), lambda i, j: (0, j)),
            pl.BlockSpec((d, tf), lambda i, j: (0, j)),
            pl.BlockSpec((tf, d), lambda i, j: (j, 0)),
        ],
        out_shape=[jax.ShapeDtypeStruct((m, d), F32), jax.ShapeDtypeStruct((d, D_FF), BF16),
                   jax.ShapeDtypeStruct((d, D_FF), BF16), jax.ShapeDtypeStruct((D_FF, d), BF16)],
        scratch_shapes=[pltpu.VMEM((m, d), BF16)],
        compiler_params=_params("arbitrary", "arbitrary"),
        name="ffn_cast",
    )(x, g.reshape(1, d), w_gu, w_gu, w_down)


def _norm_matmul_kernel(x_ref, g_ref, w_ref, o_ref, h_ref):
    @pl.when(pl.program_id(1) == 0)
    def _():
        h_ref[...] = _norm_rows(x_ref[...], g_ref[...]).astype(BF16)

    o_ref[...] = jnp.dot(h_ref[...], w_ref[...], preferred_element_type=F32)


def norm_matmul(x, g, w, tm, tn=512):
    m, d = x.shape
    n = w.shape[1]
    return pl.pallas_call(
        _norm_matmul_kernel,
        grid=(m // tm, n // tn),
        in_specs=[
            pl.BlockSpec((tm, d), lambda i, j: (i, 0)),
            pl.BlockSpec((1, d), lambda i, j: (0, 0)),
            pl.BlockSpec((d, tn), lambda i, j: (0, j)),
        ],
        out_specs=pl.BlockSpec((tm, tn), lambda i, j: (i, j)),
        out_shape=jax.ShapeDtypeStruct((m, n), F32),
        scratch_shapes=[pltpu.VMEM((tm, d), BF16)],
        compiler_params=_params("parallel", "arbitrary"),
    )(x, g.reshape(1, d), w)


def _nsa_in_proj_kernel(tn, x_ref, g_ref, w_ref, o_ref, kv_ref, h_ref):
    j = pl.program_id(1)

    @pl.when(j == 0)
    def _():
        h_ref[...] = _norm_rows(x_ref[...], g_ref[...]).astype(BF16)

    res = jnp.dot(h_ref[...], w_ref[...], preferred_element_type=F32)
    o_ref[...] = res

    @pl.when((j >= NSA_QD // tn) & (j < (NSA_QD + 6 * NSA_KVD) // tn))
    def _():
        for g in range(NSA_KV_HEADS):
            kv_ref[:, g, :] = res[:, g * NSA_HEAD_DIM:(g + 1) * NSA_HEAD_DIM]


def nsa_in_proj(x, g, w, tm):
    m, d = x.shape
    n = w.shape[1]
    tn = NSA_KVD
    q_tiles = NSA_QD // tn

    def kv_index(i, j):
        t = jnp.clip(j - q_tiles, 0, 5)
        return (t // 2, i, t % 2, 0, 0)

    return pl.pallas_call(
        functools.partial(_nsa_in_proj_kernel, tn),
        grid=(m // tm, n // tn),
        in_specs=[
            pl.BlockSpec((tm, d), lambda i, j: (i, 0)),
            pl.BlockSpec((1, d), lambda i, j: (0, 0)),
            pl.BlockSpec((d, tn), lambda i, j: (0, j)),
        ],
        out_specs=[
            pl.BlockSpec((tm, tn), lambda i, j: (i, j)),
            pl.BlockSpec((None, tm, None, NSA_KV_HEADS, NSA_HEAD_DIM), kv_index),
        ],
        out_shape=[jax.ShapeDtypeStruct((m, n), F32),
                   jax.ShapeDtypeStruct((3, m, 2, NSA_KV_HEADS, NSA_HEAD_DIM), F32)],
        scratch_shapes=[pltpu.VMEM((tm, d), BF16)],
        compiler_params=_params("parallel", "arbitrary"),
        name="nsa_in_proj",
    )(x, g.reshape(1, d), w)


def _matmul_residual_kernel(a_ref, w_ref, x_ref, o_ref):
    o_ref[...] = x_ref[...] + jnp.dot(a_ref[...], w_ref[...], preferred_element_type=F32)


def matmul_residual(a, w, x, tm, tn=512):
    m, k = a.shape
    n = w.shape[1]
    return pl.pallas_call(
        _matmul_residual_kernel,
        grid=(m // tm, n // tn),
        in_specs=[
            pl.BlockSpec((tm, k), lambda i, j: (i, 0)),
            pl.BlockSpec((k, tn), lambda i, j: (0, j)),
            pl.BlockSpec((tm, tn), lambda i, j: (i, j)),
        ],
        out_specs=pl.BlockSpec((tm, tn), lambda i, j: (i, j)),
        out_shape=jax.ShapeDtypeStruct((m, n), F32),
        compiler_params=_params("parallel", "arbitrary"),
        name="out_proj",
    )(a, w, x)


def _matmul_residual_cast_kernel(a_ref, w_ref, x_ref, o_ref, w_out):
    w_out[...] = w_ref[...].astype(BF16)
    _matmul_residual_kernel(a_ref, w_out, x_ref, o_ref)


def matmul_residual_cast(a, w, x, tn=512):
    m, k = a.shape
    n = w.shape[1]
    return pl.pallas_call(
        _matmul_residual_cast_kernel,
        grid=(n // tn,),
        in_specs=[
            pl.BlockSpec((m, k), lambda j: (0, 0)),
            pl.BlockSpec((k, tn), lambda j: (0, j)),
            pl.BlockSpec((m, tn), lambda j: (0, j)),
        ],
        out_specs=[pl.BlockSpec((m, tn), lambda j: (0, j)), pl.BlockSpec((k, tn), lambda j: (0, j))],
        out_shape=[jax.ShapeDtypeStruct((m, n), F32), jax.ShapeDtypeStruct((k, n), BF16)],
        compiler_params=_params("arbitrary"),
        name="out_proj_cast",
    )(a, w, x)


def _ple_kernel(tn, x_ref, g_ref, wg_ref, p_ref, wp_ref, o_ref, h_ref):
    j = pl.program_id(1)

    @pl.when(j == 0)
    def _():
        h_ref[...] = _norm_rows(x_ref[...], g_ref[...]).astype(BF16)

    gate = jax.nn.sigmoid(jnp.dot(h_ref[...], wg_ref[...], preferred_element_type=F32))
    proj = jnp.dot(p_ref[...].astype(BF16), wp_ref[...], preferred_element_type=F32)
    col = pl.multiple_of(j * tn, tn)
    o_ref[...] = x_ref[:, pl.ds(col, tn)] + gate * proj


def ple(x, p, g, w_gate, w_proj, tm, tn=512):
    m, d = x.shape
    pd = p.shape[1]
    return pl.pallas_call(
        functools.partial(_ple_kernel, tn),
        grid=(m // tm, d // tn),
        in_specs=[
            pl.BlockSpec((tm, d), lambda i, j: (i, 0)),
            pl.BlockSpec((1, d), lambda i, j: (0, 0)),
            pl.BlockSpec((d, tn), lambda i, j: (0, j)),
            pl.BlockSpec((tm, pd), lambda i, j: (i, 0)),
            pl.BlockSpec((pd, tn), lambda i, j: (0, j)),
        ],
        out_specs=pl.BlockSpec((tm, tn), lambda i, j: (i, j)),
        out_shape=jax.ShapeDtypeStruct((m, d), F32),
        scratch_shapes=[pltpu.VMEM((tm, d), BF16)],
        compiler_params=_params("parallel", "arbitrary"),
        name="ple",
    )(x, g.reshape(1, d), w_gate, p, w_proj)


def _ple_cast_kernel(tn, x_ref, g_ref, wg_ref, p_ref, wp_ref, o_ref, wg_out, wp_out, h_ref):
    wg_out[...] = wg_ref[...].astype(BF16)
    wp_out[...] = wp_ref[...].astype(BF16)
    _ple_kernel(tn, x_ref, g_ref, wg_out, p_ref, wp_out, o_ref, h_ref)


def ple_cast(x, p, g, w_gate, w_proj, layer, tn=512):
    m, d = x.shape
    pd = p.shape[1]
    return pl.pallas_call(
        functools.partial(_ple_cast_kernel, tn),
        grid=(1, d // tn),
        in_specs=[
            pl.BlockSpec((m, d), lambda i, j: (0, 0)),
            pl.BlockSpec((1, d), lambda i, j: (0, 0)),
            pl.BlockSpec((None, d, tn), lambda i, j: (layer, 0, j)),
            pl.BlockSpec((m, pd), lambda i, j: (0, 0)),
            pl.BlockSpec((None, pd, tn), lambda i, j: (layer, 0, j)),
        ],
        out_specs=[
            pl.BlockSpec((m, tn), lambda i, j: (0, j)),
            pl.BlockSpec((d, tn), lambda i, j: (0, j)),
            pl.BlockSpec((pd, tn), lambda i, j: (0, j)),
        ],
        out_shape=[jax.ShapeDtypeStruct((m, d), F32), jax.ShapeDtypeStruct((d, d), BF16),
                   jax.ShapeDtypeStruct((pd, d), BF16)],
        scratch_shapes=[pltpu.VMEM((m, d), BF16)],
        compiler_params=_params("arbitrary", "arbitrary"),
        name="ple_cast",
    )(x, g.reshape(1, d), w_gate, p, w_proj)


def _rmsnorm_kernel(x_ref, g_ref, o_ref):
    o_ref[...] = _norm_rows(x_ref[...], g_ref[...])


def rmsnorm(x, g, tm):
    m, d = x.shape
    return pl.pallas_call(
        _rmsnorm_kernel,
        grid=(m // tm,),
        in_specs=[pl.BlockSpec((tm, d), lambda i: (i, 0)), pl.BlockSpec((1, d), lambda i: (0, 0))],
        out_specs=pl.BlockSpec((tm, d), lambda i: (i, 0)),
        out_shape=jax.ShapeDtypeStruct((m, d), F32),
        compiler_params=_params("parallel"),
    )(x, g.reshape(1, d))


Z_GATE_COL = (NSA_QD + 6 * NSA_KVD) // LANE


def _compress_kernel(nc, x_ref, w2_ref, wf_ref, pe_ref, o_ref):
    acc = jnp.zeros((nc, 2 * NSA_HEAD_DIM), F32)
    for l in range(CMP_STRIDE):
        a = x_ref[pl.ds(l, nc, stride=CMP_STRIDE), :]
        acc += jnp.dot(a.astype(BF16), w2_ref[l], preferred_element_type=F32)
    n_idx = lax.broadcasted_iota(jnp.int32, (nc, NSA_HEAD_DIM), 0)
    first = acc[:, :NSA_HEAD_DIM]
    second = jnp.where(n_idx == nc - 1, 0.0, pltpu.roll(acc[:, NSA_HEAD_DIM:], nc - 1, 0))
    bias = jnp.dot(pe_ref[...], wf_ref[...], preferred_element_type=F32)[0:1]
    o_ref[...] = (first + second + bias).astype(BF16)


def compress(x, nb, nc, col0, w_cmp, pe_cmp):
    hd = NSA_HEAD_DIM
    w2 = w_cmp.reshape(2, 2, CMP_STRIDE, hd, hd).transpose(0, 2, 3, 1, 4).reshape(2, CMP_STRIDE, hd, 2 * hd)
    wf = w_cmp.reshape(2, CMP_LEN * hd, hd)
    pe = jnp.broadcast_to(pe_cmp.reshape(2, 1, CMP_LEN * hd), (2, 8, CMP_LEN * hd))
    return pl.pallas_call(
        functools.partial(_compress_kernel, nc),
        grid=(nb, 2, NSA_KV_HEADS),
        in_specs=[
            pl.BlockSpec((CMP_STRIDE * nc, hd), lambda i, c, g: (i, col0 + c * NSA_KV_HEADS + g)),
            pl.BlockSpec((None, CMP_STRIDE, hd, 2 * hd), lambda i, c, g: (c, 0, 0, 0)),
            pl.BlockSpec((None, CMP_LEN * hd, hd), lambda i, c, g: (c, 0, 0)),
            pl.BlockSpec((None, 8, CMP_LEN * hd), lambda i, c, g: (c, 0, 0)),
        ],
        out_specs=pl.BlockSpec((None, None, None, nc, hd), lambda i, c, g: (i, c, g, 0, 0)),
        out_shape=jax.ShapeDtypeStruct((nb, 2, NSA_KV_HEADS, nc, hd), BF16),
        compiler_params=_params("parallel", "parallel", "parallel"),
        name="nsa_compress",
    )(x, w2.astype(BF16), wf.astype(BF16), pe.astype(BF16))


def _nsa_prompt_kernel(t, qb, kb, nc, slope_ref, zq_ref, ks_ref, vs_ref, kw_ref, vw_ref, gz_ref, gb_ref,
                       ck_ref, cv_ref, ov_ref, o_ref, m_ref, l_ref, acc_ref, p_ref):
    g = pl.program_id(1)
    qi = pl.program_id(2)
    t0 = qi * qb
    r8 = NSA_Q_PER_KV
    hd = NSA_HEAD_DIM
    n_slc = t // SEL_BLOCK
    n_cmp = nc - 1

    zq = zq_ref[...] * (hd ** -0.5)
    q2 = jnp.concatenate([zq[:, r * hd:(r + 1) * hd] for r in range(r8)], axis=0).astype(BF16)

    slope = [slope_ref[g * r8 + r] for r in range(r8)]
    tq = lax.broadcasted_iota(jnp.int32, (qb, 1), 0) + t0
    rows = lambda r: slice(r * qb, (r + 1) * qb)

    n_i = lax.broadcasted_iota(jnp.int32, (qb, nc), 1)
    dist_ci = tq - (CMP_STRIDE * n_i + (CMP_LEN - 1))
    vis_c = (dist_ci >= 0) & (n_i < n_cmp)
    dist_c = dist_ci.astype(F32)
    s_all = lax.dot_general(q2, ck_ref[...], (((1,), (1,)), ((), ())), preferred_element_type=F32)
    psum = jnp.zeros((qb, nc), F32)
    for r in range(r8):
        s = jnp.where(vis_c, s_all[rows(r)] - slope[r] * dist_c, NEG_INF)
        e = jnp.exp(s - jnp.max(s, axis=-1, keepdims=True))
        p = jnp.where(vis_c, e / jnp.sum(e, axis=-1, keepdims=True), 0.0)
        psum = psum + p
        p_ref[rows(r), :nc] = p.astype(BF16)
    o_cmp = jnp.dot(p_ref[:, :nc], cv_ref[...], preferred_element_type=F32)

    p_hi = psum.astype(BF16)
    p_mid = (psum - p_hi.astype(F32)).astype(BF16)
    p_lo = (psum - p_hi.astype(F32) - p_mid.astype(F32)).astype(BF16)
    ov = ov_ref[...]
    imp = (jnp.dot(p_hi, ov, preferred_element_type=F32) + jnp.dot(p_mid, ov, preferred_element_type=F32)
           + jnp.dot(p_lo, ov, preferred_element_type=F32))
    nj = -(-n_slc // 8) * 8
    j_t = lax.broadcasted_iota(jnp.int32, (nj, qb), 0)
    cur = (t0 + lax.broadcasted_iota(jnp.int32, (nj, qb), 1)) // SEL_BLOCK
    vis_j = j_t <= cur
    forced = (j_t == 0) | (j_t == cur) | (j_t == cur - 1)
    score = jnp.where(forced, FORCE_SCORE, jnp.where(vis_j, imp.T[:nj], -FORCE_SCORE))
    score = jnp.where(j_t < n_slc, score, -jnp.inf)
    rank = jnp.zeros((nj, qb), jnp.int32)
    for jp in range(n_slc):
        row = score[jp:jp + 1, :]
        ahead = (row > score) | ((row == score) & (jp < j_t))
        rank = rank + ahead.astype(jnp.int32)
    selected = jnp.where((rank < min(N_SEL, n_slc)) & vis_j, 1.0, 0.0).astype(BF16)

    d0 = (lax.broadcasted_iota(jnp.int32, (qb, kb), 0) - lax.broadcasted_iota(jnp.int32, (qb, kb), 1))

    def flash_init():
        m_ref[...] = jnp.full(m_ref.shape, NEG_INF, F32)
        l_ref[...] = jnp.zeros(l_ref.shape, F32)
        acc_ref[...] = jnp.zeros(acc_ref.shape, F32)

    def flash_step(k_ref, v_ref, k0, mask, dist):
        k = k_ref[pl.ds(k0, kb), :].astype(BF16)
        v = v_ref[pl.ds(k0, kb), :].astype(BF16)
        s_all = lax.dot_general(q2, k, (((1,), (1,)), ((), ())), preferred_element_type=F32)
        alphas = []
        for r in range(r8):
            s = jnp.where(mask, s_all[rows(r)] - slope[r] * dist, NEG_INF)
            m_prev = m_ref[r]
            m_next = jnp.maximum(m_prev, jnp.max(s, axis=-1, keepdims=True))
            alpha = jnp.exp(m_prev - m_next)
            p = jnp.exp(s - jnp.concatenate([m_next] * (kb // LANE), axis=1))
            l_ref[r] = alpha * l_ref[r] + jnp.sum(p, axis=-1, keepdims=True)
            m_ref[r] = m_next
            p_ref[rows(r), :kb] = p.astype(BF16)
            alphas.append(alpha)
        pv = jnp.dot(p_ref[:, :kb], v, preferred_element_type=F32)
        for r in range(r8):
            acc_ref[r] = alphas[r] * acc_ref[r] + pv[rows(r)]

    flash_init()

    def sel_body(i, carry):
        k0 = pl.multiple_of(i * kb, kb)
        dist = d0 + (t0 - k0)
        blk_of_key = (lax.broadcasted_iota(jnp.int32, (nj, kb), 1) + k0) // SEL_BLOCK
        expand = jnp.where(blk_of_key == lax.broadcasted_iota(jnp.int32, (nj, kb), 0), 1.0, 0.0).astype(BF16)
        chosen = lax.dot_general(selected, expand, _TN, preferred_element_type=F32)
        flash_step(ks_ref, vs_ref, k0, (chosen > 0.5) & (dist >= 0), dist.astype(F32))
        return carry

    lax.fori_loop(0, (t0 + qb - 1) // kb + 1, sel_body, 0)
    o_sel = [acc_ref[r] / l_ref[r] for r in range(r8)]

    flash_init()

    def win_body(i, carry):
        k0 = pl.multiple_of(i * kb, kb)
        dist = d0 + (t0 - k0)
        flash_step(kw_ref, vw_ref, k0, (dist >= 0) & (dist <= WINDOW), dist.astype(F32))
        return carry

    lax.fori_loop(jnp.maximum(t0 - WINDOW, 0) // kb, (t0 + qb - 1) // kb + 1, win_body, 0)
    o_win = [acc_ref[r] / l_ref[r] for r in range(r8)]

    gates = jax.nn.sigmoid(gz_ref[...] + gb_ref[...])
    src = lax.broadcasted_iota(jnp.int32, (LANE, LANE), 0)
    dst = lax.broadcasted_iota(jnp.int32, (LANE, LANE), 1)
    pick = jnp.where((dst < 3 * r8) & (src == (dst // r8) * NSA_HEADS + g * r8 + dst % r8), 1.0, 0.0).astype(BF16)
    g_hi = gates.astype(BF16)
    g_mid = (gates - g_hi.astype(F32)).astype(BF16)
    g_lo = (gates - g_hi.astype(F32) - g_mid.astype(F32)).astype(BF16)
    gsel = (jnp.dot(g_hi, pick, preferred_element_type=F32) + jnp.dot(g_mid, pick, preferred_element_type=F32)
            + jnp.dot(g_lo, pick, preferred_element_type=F32))

    def gate(branch, r):
        k = branch * r8 + r
        return gsel[:, k:k + 1]

    for r in range(r8):
        o = gate(0, r) * o_cmp[rows(r)] + gate(1, r) * o_sel[r] + gate(2, r) * o_win[r]
        o_ref[:, r * hd:(r + 1) * hd] = o.astype(BF16)


def nsa_prompt_attention(z, ckv, gate_bias, b, t, qb=128, kb=512):
    nc = t // CMP_STRIDE
    nq = t // qb
    hd = NSA_HEAD_DIM
    n_slc = t // SEL_BLOCK
    slopes = jnp.exp2(-8.0 * (jnp.arange(NSA_HEADS, dtype=F32) + 1.0) / NSA_HEADS)
    ci = jnp.arange(nc)[:, None]
    sj = jnp.arange(LANE)[None, :]
    ov = ((CMP_STRIDE * ci < SEL_BLOCK * (sj + 1)) & (CMP_STRIDE * ci + CMP_LEN > SEL_BLOCK * sj)
          & (sj < n_slc)).astype(BF16)
    gb = jnp.pad(gate_bias, (0, LANE - gate_bias.shape[0])).reshape(1, LANE)
    kv_spec = lambda off: pl.BlockSpec((t, hd), lambda bi, g, qi: (bi, off + g))
    kv_col = (NSA_QD + 2 * NSA_KVD) // hd
    return pl.pallas_call(
        functools.partial(_nsa_prompt_kernel, t, qb, kb, nc),
        grid=(b, NSA_KV_HEADS, nq),
        in_specs=[
            pl.BlockSpec(memory_space=pltpu.SMEM),
            pl.BlockSpec((qb, NSA_Q_PER_KV * hd), lambda bi, g, qi: (bi * nq + qi, g)),
            kv_spec(kv_col), kv_spec(kv_col + 4), kv_spec(kv_col + 8), kv_spec(kv_col + 12),
            pl.BlockSpec((qb, LANE), lambda bi, g, qi: (bi * nq + qi, Z_GATE_COL)),
            pl.BlockSpec((1, LANE), lambda bi, g, qi: (0, 0)),
            pl.BlockSpec((None, None, None, nc, hd), lambda bi, g, qi: (bi, 0, g, 0, 0)),
            pl.BlockSpec((None, None, None, nc, hd), lambda bi, g, qi: (bi, 1, g, 0, 0)),
            pl.BlockSpec((nc, LANE), lambda bi, g, qi: (0, 0)),
        ],
        out_specs=pl.BlockSpec((qb, NSA_Q_PER_KV * hd), lambda bi, g, qi: (bi * nq + qi, g)),
        out_shape=jax.ShapeDtypeStruct((b * t, NSA_QD), BF16),
        scratch_shapes=[pltpu.VMEM((NSA_Q_PER_KV, qb, LANE), F32), pltpu.VMEM((NSA_Q_PER_KV, qb, LANE), F32),
                        pltpu.VMEM((NSA_Q_PER_KV, qb, hd), F32),
                        pltpu.VMEM((NSA_Q_PER_KV * qb, max(kb, nc)), BF16)],
        compiler_params=_params("parallel", "parallel", "arbitrary"),
        name="nsa_prompt_attn",
    )(slopes, z, z, z, z, z, z, gb, ckv, ckv, ov)


_NT = (((1,), (1,)), ((), ()))
_TN = (((0,), (0,)), ((), ()))


GATHER_PAGES = 4


def _page_gather_kernel(n_steps, pt_ref, *refs):
    page_refs, tail_ref, o_ref = refs[:GATHER_PAGES], refs[GATHER_PAGES], refs[GATHER_PAGES + 1]
    p = pl.program_id(1)

    @pl.when(p < n_steps)
    def _():
        for k, x_ref in enumerate(page_refs):
            for c in range(2):
                for g in range(NSA_KV_HEADS):
                    col = (c * NSA_KV_HEADS + g) * NSA_HEAD_DIM
                    o_ref[k * PAGE_SIZE:(k + 1) * PAGE_SIZE, col:col + NSA_HEAD_DIM] = x_ref[:, c, g, :]

    @pl.when(p == n_steps)
    def _():
        o_ref[...] = tail_ref[...]


def page_gather(cache, page_table, tail):
    nb, n_pages = page_table.shape
    step_rows = GATHER_PAGES * PAGE_SIZE
    n_steps = n_pages // GATHER_PAGES
    w = tail.shape[2]

    def page_spec(k):
        def index(b, p, pt):
            return (pt[b, jnp.minimum(p, n_steps - 1) * GATHER_PAGES + k], 0, 0, 0, 0)
        return pl.BlockSpec((None, PAGE_SIZE, 2, NSA_KV_HEADS, NSA_HEAD_DIM), index)

    return pl.pallas_call(
        functools.partial(_page_gather_kernel, n_steps),
        grid_spec=pltpu.PrefetchScalarGridSpec(
            num_scalar_prefetch=1,
            grid=(nb, n_steps + 1),
            in_specs=[page_spec(k) for k in range(GATHER_PAGES)]
            + [pl.BlockSpec((None, step_rows, w), lambda b, p, pt: (b, 0, 0))],
            out_specs=pl.BlockSpec((step_rows, w), lambda b, p, pt: (b * (n_steps + 1) + p, 0)),
        ),
        out_shape=jax.ShapeDtypeStruct((nb * (n_steps + 1) * step_rows, w), F32),
        compiler_params=_params("parallel", "arbitrary"),
        name="nsa_page_gather",
    )(page_table, *([cache] * GATHER_PAGES), tail)


def _row_slopes(slope_ref, g, n_rows, t):
    r_idx = lax.broadcasted_iota(jnp.int32, (n_rows, 1), 0) // t
    col = jnp.zeros((n_rows, 1), F32)
    for r in range(NSA_Q_PER_KV):
        col = jnp.where(r_idx == r, slope_ref[g * NSA_Q_PER_KV + r], col)
    return col


def _nsa_sample_select_kernel(t, pos0, nc, n_cmp, n_slc, slope_ref, q_ref, ck_ref, cv_ref, ov_ref,
                              ocmp_ref, idx_ref, valid_ref):
    g = pl.program_id(0)
    r8 = NSA_Q_PER_KV
    n_rows = r8 * t
    nsp = ov_ref.shape[1]
    q2 = (q_ref[...] * (NSA_HEAD_DIM ** -0.5)).astype(BF16)
    slope = _row_slopes(slope_ref, g, n_rows, t)
    tq = pos0 + lax.broadcasted_iota(jnp.int32, (n_rows, 1), 0) % t
    n_i = lax.broadcasted_iota(jnp.int32, (n_rows, nc), 1)
    dist_ci = tq - (CMP_STRIDE * n_i + (CMP_LEN - 1))
    vis_c = (dist_ci >= 0) & (n_i < n_cmp)
    s = lax.dot_general(q2, ck_ref[...], _NT, preferred_element_type=F32)
    s = jnp.where(vis_c, s - slope * dist_ci.astype(F32), NEG_INF)
    e = jnp.exp(s - jnp.max(s, axis=-1, keepdims=True))
    p = jnp.where(vis_c, e / jnp.sum(e, axis=-1, keepdims=True), 0.0)
    ocmp_ref[...] = jnp.dot(p.astype(BF16), cv_ref[...], preferred_element_type=F32)

    p_hi = p.astype(BF16)
    p_mid = (p - p_hi.astype(F32)).astype(BF16)
    p_lo = (p - p_hi.astype(F32) - p_mid.astype(F32)).astype(BF16)
    ov = ov_ref[...]
    imp = (jnp.dot(p_hi, ov, preferred_element_type=F32) + jnp.dot(p_mid, ov, preferred_element_type=F32)
           + jnp.dot(p_lo, ov, preferred_element_type=F32))
    sh = t
    while sh < n_rows:
        imp = imp + pltpu.roll(imp, sh, 0)
        sh *= 2
    imp = imp[0:8]
    j_i = lax.broadcasted_iota(jnp.int32, (8, nsp), 1)
    cur = (pos0 + lax.broadcasted_iota(jnp.int32, (8, 1), 0) % t) // SEL_BLOCK
    vis_j = j_i <= cur
    forced = (j_i == 0) | (j_i == cur) | (j_i == cur - 1)
    score = jnp.where(forced, FORCE_SCORE, jnp.where(vis_j, imp, -FORCE_SCORE))
    score = jnp.where(j_i < n_slc, score, -jnp.inf)
    rank = jnp.zeros((8, nsp), jnp.int32)
    for jp in range(n_slc):
        col = score[:, jp:jp + 1]
        ahead = (col > score) | ((col == score) & (jp < j_i))
        rank = rank + ahead.astype(jnp.int32)
    lane = lax.broadcasted_iota(jnp.int32, (8, LANE), 1)
    idx = jnp.zeros((8, LANE), F32)
    valid = jnp.zeros((8, LANE), F32)
    j_f = j_i.astype(F32)
    for k in range(min(N_SEL, n_slc)):
        hit = (rank == k) & (j_i < n_slc)
        idx_k = jnp.sum(jnp.where(hit, j_f, 0.0), axis=-1, keepdims=True)
        valid_k = jnp.sum(jnp.where(hit & vis_j, 1.0, 0.0), axis=-1, keepdims=True)
        idx = jnp.where(lane == k, idx_k, idx)
        valid = jnp.where(lane == k, valid_k, valid)
    idx_ref[...] = idx.astype(jnp.int32)
    valid_ref[...] = valid.astype(jnp.int32)


def nsa_sample_select(q, ckv, nb, t, pos0, n_cmp, n_slc):
    g4, hd = NSA_KV_HEADS, NSA_HEAD_DIM
    nc = ckv.shape[3]
    n_rows = NSA_Q_PER_KV * t
    nsp = -(-n_slc // LANE) * LANE
    slopes = jnp.exp2(-8.0 * (jnp.arange(NSA_HEADS, dtype=F32) + 1.0) / NSA_HEADS)
    ci = jnp.arange(nc)[:, None]
    sj = jnp.arange(nsp)[None, :]
    ov = ((CMP_STRIDE * ci < SEL_BLOCK * (sj + 1)) & (CMP_STRIDE * ci + CMP_LEN > SEL_BLOCK * sj)
          & (sj < n_slc) & (ci < n_cmp)).astype(BF16)
    return pl.pallas_call(
        functools.partial(_nsa_sample_select_kernel, t, pos0, nc, n_cmp, n_slc),
        grid=(g4, nb),
        in_specs=[
            pl.BlockSpec(memory_space=pltpu.SMEM),
            pl.BlockSpec((None, None, n_rows, hd), lambda g, b: (g, b, 0, 0)),
            pl.BlockSpec((None, None, None, nc, hd), lambda g, b: (b, 0, g, 0, 0)),
            pl.BlockSpec((None, None, None, nc, hd), lambda g, b: (b, 1, g, 0, 0)),
            pl.BlockSpec((nc, nsp), lambda g, b: (0, 0)),
        ],
        out_specs=[
            pl.BlockSpec((None, None, n_rows, hd), lambda g, b: (g, b, 0, 0)),
            pl.BlockSpec((None, None, 8, LANE), lambda g, b: (g, b, 0, 0)),
            pl.BlockSpec((None, None, 8, LANE), lambda g, b: (g, b, 0, 0)),
        ],
        out_shape=[jax.ShapeDtypeStruct((g4, nb, n_rows, hd), F32),
                   jax.ShapeDtypeStruct((g4, nb, 8, LANE), jnp.int32),
                   jax.ShapeDtypeStruct((g4, nb, 8, LANE), jnp.int32)],
        compiler_params=_params("parallel", "parallel"),
        name="nsa_sample_select",
    )(slopes, q, ckv, ckv, ov)


def _nsa_sample_attn_kernel(nb, t, pos0, n_pages, wb, idx_ref, valid_ref, pt_ref, slope_ref, q_ref, ocmp_ref,
                            gl_ref, gb_ref, kwin_ref, vwin_ref, kwt_ref, vwt_ref, cache_ref, tail_ref, o_ref,
                            kbuf, vbuf, q2_ref, osel_ref, sem):
    b = pl.program_id(0)
    g = pl.program_id(1)
    r8 = NSA_Q_PER_KV
    hd = NSA_HEAD_DIM
    n_rows = r8 * t
    n_sel = kbuf.shape[1]
    past_blocks = n_pages * (PAGE_SIZE // SEL_BLOCK)
    per_page = PAGE_SIZE // SEL_BLOCK

    def sel_entry(tt, k):
        return ((g * nb + b) * t + tt) * n_sel + k

    def block_copies(tt, k, in_tail):
        i = idx_ref[sel_entry(tt, k)]
        out = []
        for c, buf in ((0, kbuf), (1, vbuf)):
            col = c * NSA_KV_HEADS + g
            if in_tail:
                src = tail_ref.at[b, :, pl.ds(col * hd, hd)]
            else:
                ic = jnp.minimum(i, past_blocks - 1)
                page = pt_ref[b * n_pages + ic // per_page]
                src = cache_ref.at[page, pl.ds((ic % per_page) * SEL_BLOCK, SEL_BLOCK), c, g, :]
            out.append(pltpu.make_async_copy(src, buf.at[tt, k], sem))
        return out

    def for_each_block(fn):
        for tt in range(t):
            for k in range(n_sel):
                in_tail = idx_ref[sel_entry(tt, k)] >= past_blocks

                @pl.when(in_tail)
                def _():
                    for cp in block_copies(tt, k, True):
                        fn(cp)

                @pl.when(jnp.logical_not(in_tail))
                def _():
                    for cp in block_copies(tt, k, False):
                        fn(cp)

    for_each_block(lambda cp: cp.start())

    q2_ref[...] = q_ref[...] * (hd ** -0.5)
    q2 = q2_ref[...].astype(BF16)
    slope = _row_slopes(slope_ref, g, n_rows, t)
    tq = pos0 + lax.broadcasted_iota(jnp.int32, (n_rows, 1), 0) % t

    kw = jnp.concatenate([kwin_ref[...], kwt_ref[...]], axis=0).astype(BF16)
    vw = jnp.concatenate([vwin_ref[...], vwt_ref[...]], axis=0).astype(BF16)
    nk = kw.shape[0]
    s_pos = (pos0 - wb) + lax.broadcasted_iota(jnp.int32, (n_rows, nk), 1)
    dist = tq - s_pos
    mask = (dist >= 0) & (dist <= WINDOW) & (s_pos >= 0)
    s = lax.dot_general(q2, kw, _NT, preferred_element_type=F32)
    s = jnp.where(mask, s - slope * dist.astype(F32), NEG_INF)
    e = jnp.exp(s - jnp.max(s, axis=-1, keepdims=True))
    p = e / jnp.sum(e, axis=-1, keepdims=True)
    o_win = jnp.dot(p.astype(BF16), vw, preferred_element_type=F32)

    for_each_block(lambda cp: cp.wait())

    nkeys = n_sel * SEL_BLOCK
    key_lane = lax.broadcasted_iota(jnp.int32, (1, nkeys), 1)
    slope8 = _row_slopes(slope_ref, g, r8, 1)
    for tt in range(t):
        blk = jnp.zeros((1, nkeys), jnp.int32)
        ok = jnp.zeros((1, nkeys), jnp.int32)
        for k in range(n_sel):
            here = key_lane // SEL_BLOCK == k
            blk = jnp.where(here, idx_ref[sel_entry(tt, k)], blk)
            ok = jnp.where(here, valid_ref[sel_entry(tt, k)], ok)
        dist = (pos0 + tt) - (blk * SEL_BLOCK + key_lane % SEL_BLOCK)
        mask = (dist >= 0) & (ok > 0)
        q_t = q2_ref[pl.ds(tt, r8, stride=t), :].astype(BF16)
        k_t = kbuf[tt].reshape(nkeys, hd).astype(BF16)
        v_t = vbuf[tt].reshape(nkeys, hd).astype(BF16)
        s = lax.dot_general(q_t, k_t, _NT, preferred_element_type=F32)
        s = jnp.where(mask, s - slope8 * dist.astype(F32), NEG_INF)
        e = jnp.exp(s - jnp.max(s, axis=-1, keepdims=True))
        p = e / jnp.sum(e, axis=-1, keepdims=True)
        osel_ref[pl.ds(tt, r8, stride=t), :] = jnp.dot(p.astype(BF16), v_t, preferred_element_type=F32)

    gates = jax.nn.sigmoid(gl_ref[...] + gb_ref[...])
    o = gates[:, 0:1] * ocmp_ref[...] + gates[:, 1:2] * osel_ref[...] + gates[:, 2:3] * o_win
    o_ref[...] = o.astype(BF16)


def nsa_sample_attention(q, o_cmp, gl, gb, idx, valid, page_table, cache_sel, tail_sel, cache_win, tail_win,
                         nb, t, pos0):
    g4, hd = NSA_KV_HEADS, NSA_HEAD_DIM
    n_rows = NSA_Q_PER_KV * t
    n_pages = page_table.shape[1]
    n_sel = idx.shape[-1]
    wb = cache_win.shape[1]
    slopes = jnp.exp2(-8.0 * (jnp.arange(NSA_HEADS, dtype=F32) + 1.0) / NSA_HEADS)
    row_spec = lambda w: pl.BlockSpec((None, None, n_rows, w), lambda b, g, *_: (g, b, 0, 0))
    win_spec = lambda rows, c: pl.BlockSpec((None, rows, hd), lambda b, g, *_: (b, 0, c * g4 + g))
    return pl.pallas_call(
        functools.partial(_nsa_sample_attn_kernel, nb, t, pos0, n_pages, wb),
        grid_spec=pltpu.PrefetchScalarGridSpec(
            num_scalar_prefetch=3,
            grid=(nb, g4),
            in_specs=[
                pl.BlockSpec(memory_space=pltpu.SMEM),
                row_spec(hd), row_spec(hd), row_spec(3),
                pl.BlockSpec((None, n_rows, 3), lambda b, g, *_: (g, 0, 0)),
                win_spec(wb, 0), win_spec(wb, 1), win_spec(LANE, 0), win_spec(LANE, 1),
                pl.BlockSpec(memory_space=pl.ANY), pl.BlockSpec(memory_space=pl.ANY),
            ],
            out_specs=row_spec(hd),
            scratch_shapes=[pltpu.VMEM((t, n_sel, SEL_BLOCK, hd), F32), pltpu.VMEM((t, n_sel, SEL_BLOCK, hd), F32),
                            pltpu.VMEM((n_rows, hd), F32), pltpu.VMEM((n_rows, hd), F32),
                            pltpu.SemaphoreType.DMA(())],
        ),
        out_shape=jax.ShapeDtypeStruct((g4, nb, n_rows, hd), BF16),
        compiler_params=_params("arbitrary", "arbitrary"),
        name="nsa_sample_attn",
    )(idx.reshape(-1), valid.reshape(-1), page_table.reshape(-1), slopes, q, o_cmp, gl, gb,
      cache_win, cache_win, tail_win, tail_win, cache_sel, tail_sel)


def nsa_sample_mix(z, nb, t, cache_cmp, cache_sel, cache_win, page_table, gate_bias, w_cmp, pe_cmp):
    g4, r8, hd = NSA_KV_HEADS, NSA_Q_PER_KV, NSA_HEAD_DIM
    n_pages = page_table.shape[1]
    past = n_pages * PAGE_SIZE
    wb = cache_win.shape[1]
    t_all = past + t
    tp = -(-t_all // SEL_BLOCK) * SEL_BLOCK
    n_slc = tp // SEL_BLOCK
    n_cmp = tp // CMP_STRIDE - 1
    gather_rows = (n_pages + GATHER_PAGES) * PAGE_SIZE
    nc = gather_rows // CMP_STRIDE
    kv = z[:, NSA_QD:NSA_QD + 6 * NSA_KVD].reshape(nb, t, 3, 2 * NSA_KVD)

    tail_cmp = jnp.pad(kv[:, :, 0], ((0, 0), (0, gather_rows - past - t), (0, 0)))
    rows = page_gather(cache_cmp, page_table, tail_cmp)
    ckv = compress(rows, nb, nc, 0, w_cmp, pe_cmp)

    q = z[:, :NSA_QD].reshape(nb, t, g4, r8, hd).transpose(2, 0, 3, 1, 4).reshape(g4, nb, r8 * t, hd)
    o_cmp, idx, valid = nsa_sample_select(q, ckv, nb, t, past, n_cmp, n_slc)
    n_sel = min(N_SEL, n_slc)
    idx = idx[:, :, :t, :n_sel]
    valid = valid[:, :, :t, :n_sel]

    gl = z[:, NSA_QD + 6 * NSA_KVD:NSA_QD + 6 * NSA_KVD + 3 * NSA_HEADS].reshape(nb, t, 3, g4, r8)
    gl = gl.transpose(3, 0, 4, 1, 2).reshape(g4, nb, r8 * t, 3)
    gb = jnp.broadcast_to(gate_bias.reshape(3, g4, r8, 1), (3, g4, r8, t)).transpose(1, 2, 3, 0).reshape(g4, r8 * t, 3)
    tail_sel = jnp.pad(kv[:, :, 1], ((0, 0), (0, SEL_BLOCK - t), (0, 0)))
    tail_win = jnp.pad(kv[:, :, 2], ((0, 0), (0, LANE - t), (0, 0)))
    o = nsa_sample_attention(q, o_cmp, gl, gb, idx, valid, page_table,
                             cache_sel, tail_sel,
                             cache_win.reshape(nb, wb, 2 * NSA_KVD), tail_win, nb, t, past)
    o = o.reshape(g4, nb, r8, t, hd).transpose(1, 3, 0, 2, 4).reshape(nb * t, NSA_QD)
    kv6 = kv.reshape(nb, t, 3, 2, g4, hd)
    win_new = jnp.concatenate([cache_win, kv6[:, :, 2]], axis=1)[:, t:]
    return o, kv6[:, :, 0], kv6[:, :, 1], win_new


GLA_QK = GLA_HEADS * GLA_DK
GLA_VD = GLA_HEADS * GLA_DV
GLA_SUPER = 256


def _chunk_scan(la, c):
    n = la.shape[0]
    ri = lax.broadcasted_iota(jnp.int32, la.shape, 0) % c
    b = la
    s = 1
    while s < c:
        b = b + jnp.where(ri >= s, pltpu.roll(b, s, 0), 0.0)
        s *= 2
    tot = jnp.where(ri == c - 1, b, 0.0)
    s = 1
    while s < c:
        tot = tot + jnp.where(ri + s < c, pltpu.roll(tot, n - s, 0), 0.0)
        s *= 2
    return b, tot


def _gla_gates(a_rows, wa_ref, ba_ref):
    x = jnp.dot(a_rows.astype(BF16), wa_ref[...], preferred_element_type=F32) + ba_ref[...]
    return jax.nn.log_sigmoid(x) / GLA_TAU


def _gla_finish(o, r, g_ref):
    return (_norm_rows(o, g_ref[...]) * (r * jax.nn.sigmoid(r))).astype(BF16)


def _gla_prompt_kernel(t, q_ref, k_ref, v_ref, r_ref, a_ref, wa_ref, ba_ref, g_ref, o_ref, s_ref, st_ref, oc_ref):
    c = GLA_CHUNK
    sc = GLA_SUPER
    st_ref[...] = jnp.zeros(st_ref.shape, F32)
    ti = lax.broadcasted_iota(jnp.int32, (sc, sc), 0)
    si = lax.broadcasted_iota(jnp.int32, (sc, sc), 1)
    causal = (ti // c == si // c) & (si <= ti)

    def body(i, carry):
        r0 = pl.multiple_of(i * sc, sc)
        la = _gla_gates(a_ref[pl.ds(r0, sc), :], wa_ref, ba_ref)
        b, b_last = _chunk_scan(la, c)
        q = q_ref[pl.ds(r0, sc), :] * (GLA_DK ** -0.5)
        k = k_ref[pl.ds(r0, sc), :]
        qe = (q * jnp.exp(b)).astype(BF16)
        ke = (k * jnp.exp(-b)).astype(BF16)
        kd = (k * jnp.exp(b_last - b)).astype(BF16)
        vb = v_ref[pl.ds(r0, sc), :].astype(BF16)
        att = jnp.where(causal, lax.dot_general(qe, ke, _NT, preferred_element_type=F32), 0.0)
        o_intra = jnp.dot(att.astype(BF16), vb, preferred_element_type=F32)
        for j in range(sc // c):
            rows = slice(j * c, (j + 1) * c)
            st = st_ref[...]
            o_inter = lax.dot_general(qe[rows], st.astype(BF16), _NT, preferred_element_type=F32)
            oc_ref[rows, :] = o_intra[rows] + o_inter
            dec = jnp.exp(b_last[j * c:j * c + 1, :])
            st_ref[...] = dec * st + lax.dot_general(vb[rows], kd[rows], _TN, preferred_element_type=F32)
        o_ref[pl.ds(r0, sc), :] = _gla_finish(oc_ref[...], r_ref[pl.ds(r0, sc), :], g_ref)
        return carry

    lax.fori_loop(0, t // sc, body, 0)
    s_ref[...] = st_ref[...].T


def gla_prompt(z, b, t, w_alpha, b_alpha, norm_g):
    dk, dv, h = GLA_DK, GLA_DV, GLA_HEADS
    wa = jnp.pad(w_alpha, ((0, LANE - GLA_RANK), (0, 0))).astype(BF16)
    return pl.pallas_call(
        functools.partial(_gla_prompt_kernel, t),
        grid=(b, h),
        in_specs=[
            pl.BlockSpec((t, dk), lambda bi, hi: (bi, hi)),
            pl.BlockSpec((t, dk), lambda bi, hi: (bi, h + hi)),
            pl.BlockSpec((t, dv), lambda bi, hi: (bi, 2 * GLA_QK // dv + hi)),
            pl.BlockSpec((t, dv), lambda bi, hi: (bi, (2 * GLA_QK + GLA_VD) // dv + hi)),
            pl.BlockSpec((t, LANE), lambda bi, hi: (bi, (2 * GLA_QK + 2 * GLA_VD) // LANE)),
            pl.BlockSpec((LANE, dk), lambda bi, hi: (0, hi)),
            pl.BlockSpec((1, dk), lambda bi, hi: (0, hi)),
            pl.BlockSpec((1, dv), lambda bi, hi: (0, 0)),
        ],
        out_specs=[
            pl.BlockSpec((t, dv), lambda bi, hi: (bi, hi)),
            pl.BlockSpec((None, None, dk, dv), lambda bi, hi: (bi, hi, 0, 0)),
        ],
        out_shape=[jax.ShapeDtypeStruct((b * t, GLA_VD), BF16), jax.ShapeDtypeStruct((b, h, dk, dv), F32)],
        scratch_shapes=[pltpu.VMEM((dv, dk), F32), pltpu.VMEM((GLA_SUPER, dv), F32)],
        compiler_params=_params("parallel", "arbitrary"),
        name="gla_prompt",
    )(z, z, z, z, z, wa, b_alpha.reshape(1, -1), norm_g.reshape(1, -1))


def _gla_sample_kernel(nb, t, q_ref, k_ref, v_ref, r_ref, a_ref, wa_ref, ba_ref, g_ref, s0_ref, o_ref, s_ref):
    n = nb * t
    la = _gla_gates(a_ref[...], wa_ref, ba_ref)
    b, b_last = _chunk_scan(la, t)
    q = q_ref[...] * (GLA_DK ** -0.5)
    k = k_ref[...]
    qe = q * jnp.exp(b)
    ke = (k * jnp.exp(-b)).astype(BF16)
    kd = k * jnp.exp(b_last - b)
    vb = v_ref[...].astype(BF16)
    ti = lax.broadcasted_iota(jnp.int32, (n, n), 0)
    si = lax.broadcasted_iota(jnp.int32, (n, n), 1)
    causal = (ti // t == si // t) & (si <= ti)
    att = jnp.where(causal, lax.dot_general(qe.astype(BF16), ke, _NT, preferred_element_type=F32), 0.0)
    o = jnp.dot(att.astype(BF16), vb, preferred_element_type=F32)
    row = lax.broadcasted_iota(jnp.int32, (n, 1), 0) // t
    for i in range(nb):
        mine = row == i
        st = s0_ref[i].T
        qe_i = jnp.where(mine, qe, 0.0).astype(BF16)
        kd_i = jnp.where(mine, kd, 0.0).astype(BF16)
        o = o + lax.dot_general(qe_i, st.astype(BF16), _NT, preferred_element_type=F32)
        dec = jnp.exp(b_last[i * t:i * t + 1, :])
        s_ref[i] = (dec * st + lax.dot_general(vb, kd_i, _TN, preferred_element_type=F32)).T
    o_ref[...] = _gla_finish(o, r_ref[...], g_ref)


def gla_sample(z, nb, t, s0, w_alpha, b_alpha, norm_g):
    dk, dv, h = GLA_DK, GLA_DV, GLA_HEADS
    n = nb * t
    wa = jnp.pad(w_alpha, ((0, LANE - GLA_RANK), (0, 0))).astype(BF16)
    return pl.pallas_call(
        functools.partial(_gla_sample_kernel, nb, t),
        grid=(h,),
        in_specs=[
            pl.BlockSpec((n, dk), lambda hi: (0, hi)),
            pl.BlockSpec((n, dk), lambda hi: (0, h + hi)),
            pl.BlockSpec((n, dv), lambda hi: (0, 2 * GLA_QK // dv + hi)),
            pl.BlockSpec((n, dv), lambda hi: (0, (2 * GLA_QK + GLA_VD) // dv + hi)),
            pl.BlockSpec((n, LANE), lambda hi: (0, (2 * GLA_QK + 2 * GLA_VD) // LANE)),
            pl.BlockSpec((LANE, dk), lambda hi: (0, hi)),
            pl.BlockSpec((1, dk), lambda hi: (0, hi)),
            pl.BlockSpec((1, dv), lambda hi: (0, 0)),
            pl.BlockSpec((nb, None, dk, dv), lambda hi: (0, hi, 0, 0)),
        ],
        out_specs=[
            pl.BlockSpec((n, dv), lambda hi: (0, hi)),
            pl.BlockSpec((nb, None, dk, dv), lambda hi: (0, hi, 0, 0)),
        ],
        out_shape=[jax.ShapeDtypeStruct((n, GLA_VD), BF16), jax.ShapeDtypeStruct((nb, h, dk, dv), F32)],
        compiler_params=_params("parallel"),
        name="gla_sample",
    )(z, z, z, z, z, wa, b_alpha.reshape(1, -1), norm_g.reshape(1, -1), s0)


def nsa_prompt_mix(z, kv, b, t, gate_bias, w_cmp, pe_cmp):
    ckv = compress(z, b, t // CMP_STRIDE, NSA_QD // NSA_HEAD_DIM, w_cmp, pe_cmp)
    o = nsa_prompt_attention(z, ckv, gate_bias, b, t)
    kv = kv.reshape(3, b, t, 2, NSA_KV_HEADS, NSA_HEAD_DIM)
    return o, kv[0], kv[1], kv[2, :, t - min(WINDOW, t):]


def _pad_cols(w, mult):
    n = w.shape[-1]
    return jnp.pad(w, ((0, 0), (0, -(-n // mult) * mult - n)))


def kernel(x_prompt, x_sample, p_prompt, p_sample, cache_cmp_kv, cache_sel_kv, cache_win_kv, state_gla, page_table, ffn1_norm, ffn1_w_gu, ffn1_w_down, mix_norm, nsa_w_in, nsa_gate_bias, nsa_w_cmp, nsa_pe_cmp, nsa_w_out, gla_w_in, gla_w_alpha, gla_b_alpha, gla_norm, gla_w_out, ffn2_norm, ffn2_w_gu, ffn2_w_down, ple_norm, ple_w_gate, ple_w_proj, final_norm):
    bp, tp, d = x_prompt.shape
    bs, ts, _ = x_sample.shape
    mp, ms = bp * tp, bs * ts
    xp = x_prompt.reshape(mp, d)
    xs = x_sample.reshape(ms, d)
    tm_p = _row_tile(mp, 512)
    tm_p2 = _row_tile(mp, 512)

    cmp_p, sel_p, win_p, gla_p = [], [], [], []
    cmp_s, sel_s, win_s, gla_s = [], [], [], []
    for i in range(DEPTH):
        xs, wa, wu, wd = ffn_cast(xs, ffn1_norm[i], ffn1_w_gu, ffn1_w_down, i)
        xp = ffn(xp, ffn1_norm[i], wa, wu, wd, tm_p)
        j = i // 2
        if i % 2 == 0:
            w_in = _pad_cols(nsa_w_in[j], 512).astype(BF16)
            w_out = nsa_w_out[j]
            zp, kvp = nsa_in_proj(xp, mix_norm[i], w_in, tm_p2)
            zs = norm_matmul(xs, mix_norm[i], w_in, ms)
            op, kc, kl, kw = nsa_prompt_mix(zp, kvp, bp, tp, nsa_gate_bias[j], nsa_w_cmp[j], nsa_pe_cmp[j])
            os_, kc2, kl2, kw2 = nsa_sample_mix(zs, bs, ts, cache_cmp_kv[j], cache_sel_kv[j], cache_win_kv[j],
                                                page_table, nsa_gate_bias[j], nsa_w_cmp[j], nsa_pe_cmp[j])
            cmp_p.append(kc)
            sel_p.append(kl)
            win_p.append(kw)
            cmp_s.append(kc2)
            sel_s.append(kl2)
            win_s.append(kw2)
        else:
            w_in = _pad_cols(gla_w_in[j], 512).astype(BF16)
            w_out = gla_w_out[j]
            zp = norm_matmul(xp, mix_norm[i], w_in, tm_p2)
            zs = norm_matmul(xs, mix_norm[i], w_in, ms)
            op, sp = gla_prompt(zp, bp, tp, gla_w_alpha[j], gla_b_alpha[j], gla_norm[j])
            os_, ss = gla_sample(zs, bs, ts, state_gla[j], gla_w_alpha[j], gla_b_alpha[j], gla_norm[j])
            gla_p.append(sp)
            gla_s.append(ss)
        xs, w_out = matmul_residual_cast(os_, w_out, xs)
        xp = matmul_residual(op, w_out, xp, tm_p)
        xs, wa, wu, wd = ffn_cast(xs, ffn2_norm[i], ffn2_w_gu, ffn2_w_down, i)
        xp = ffn(xp, ffn2_norm[i], wa, wu, wd, tm_p)
        xs, w_pg, w_pp = ple_cast(xs, p_sample[i].reshape(ms, -1), ple_norm[i], ple_w_gate, ple_w_proj, i)
        xp = ple(xp, p_prompt[i].reshape(mp, -1), ple_norm[i], w_pg, w_pp, tm_p2)
    y_prompt = rmsnorm(xp, final_norm, tm_p2).reshape(bp, tp, d)
    y_sample = rmsnorm(xs, final_norm, ms).reshape(bs, ts, d)
    return (y_prompt, y_sample, jnp.stack(cmp_p), jnp.stack(sel_p), jnp.stack(win_p), jnp.stack(gla_p),
            jnp.stack(cmp_s), jnp.stack(sel_s), jnp.stack(win_s), jnp.stack(gla_s))
```

```python
import functools
import math

import jax
import jax.numpy as jnp
from jax import lax
from jax.experimental import pallas as pl
from jax.experimental.pallas import tpu as pltpu

F32 = jnp.float32
BF16 = jnp.bfloat16

D_MODEL = 4096
DEPTH = 2
PAST_LEN = 16384
PAGE_SIZE = 128
NSA_HEADS = 32
NSA_HEAD_DIM = 128
NSA_KV_HEADS = 4
NSA_Q_PER_KV = 8
CMP_STRIDE = 16
CMP_LEN = 32
SEL_BLOCK = 64
N_SEL = 16
WINDOW = 512
NSA_QBLOCK = 32
NSA_QD = NSA_HEADS * NSA_HEAD_DIM
NSA_KVD = NSA_KV_HEADS * NSA_HEAD_DIM
GLA_HEADS = 8
GLA_DK = 256
GLA_DV = 512
GLA_RANK = 16
GLA_TAU = 16.0
GLA_CHUNK = 32
D_FF = 11008
EPS = 1e-6
NEG_INF = -1e30
FORCE_SCORE = 1e30

V7X_VMEM_LIMIT_BYTES = 56 * 1024 * 1024
LANE = 128


def _params(*sem):
    return pltpu.CompilerParams(dimension_semantics=sem, vmem_limit_bytes=V7X_VMEM_LIMIT_BYTES)


def _norm_rows(x, g):
    ms = jnp.mean(x * x, axis=-1, keepdims=True)
    return (x * lax.rsqrt(ms + EPS)) * g


def _row_tile(m, want):
    return want if m % want == 0 else m


def _ffn_kernel(x_ref, g_ref, wa_ref, wu_ref, wd_ref, o_ref, h_ref):
    @pl.when(pl.program_id(1) == 0)
    def _():
        x = x_ref[...]
        h_ref[...] = _norm_rows(x, g_ref[...]).astype(BF16)
        o_ref[...] = x

    h = h_ref[...]
    a = jnp.dot(h, wa_ref[...], preferred_element_type=F32)
    u = jnp.dot(h, wu_ref[...], preferred_element_type=F32)
    act = (0.5 * (a * jax.nn.sigmoid(a)) * u).astype(BF16)
    o_ref[...] += jnp.dot(act, wd_ref[...], preferred_element_type=F32)


def ffn(x, g, wa, wu, wd, tm, tf=256):
    m, d = x.shape
    return pl.pallas_call(
        _ffn_kernel,
        grid=(m // tm, D_FF // tf),
        in_specs=[
            pl.BlockSpec((tm, d), lambda i, j: (i, 0), pipeline_mode=pl.Buffered(1)),
            pl.BlockSpec((1, d), lambda i, j: (0, 0)),
            pl.BlockSpec((d, tf), lambda i, j: (0, j)),
            pl.BlockSpec((d, tf), lambda i, j: (0, j)),
            pl.BlockSpec((tf, d), lambda i, j: (j, 0)),
        ],
        out_specs=pl.BlockSpec((tm, d), lambda i, j: (i, 0)),
        out_shape=jax.ShapeDtypeStruct((m, d), F32),
        scratch_shapes=[pltpu.VMEM((tm, d), BF16)],
        compiler_params=_params("parallel", "arbitrary"),
        name="ffn",
    )(x, g.reshape(1, d), wa, wu, wd)


def _ffn_cast_kernel(x_ref, g_ref, wa_ref, wu_ref, wd_ref, o_ref, wa_out, wu_out, wd_out, h_ref):
    wa_out[...] = wa_ref[...].astype(BF16)
    wu_out[...] = wu_ref[...].astype(BF16)
    wd_out[...] = wd_ref[...].astype(BF16)
    _ffn_kernel(x_ref, g_ref, wa_out, wu_out, wd_out, o_ref, h_ref)


def ffn_cast(x, g, w_gu, w_down, layer, tf=256):
    m, d = x.shape
    nf = D_FF // tf
    return pl.pallas_call(
        _ffn_cast_kernel,
        grid=(1, nf),
        in_specs=[
            pl.BlockSpec((m, d), lambda i, j: (0, 0)),
            pl.BlockSpec((1, d), lambda i, j: (0, 0)),
            pl.BlockSpec((None, d, tf), lambda i, j: (layer, 0, j)),
            pl.BlockSpec((None, d, tf), lambda i, j: (layer, 0, j + nf)),
            pl.BlockSpec((None, tf, d), lambda i, j: (layer, j, 0)),
        ],
        out_specs=[
            pl.BlockSpec((m, d), lambda i, j: (0, 0)),
            pl.BlockSpec((d, tf), lambda i, j: (0, j)),
            pl.BlockSpec((d, tf), lambda i, j: (0, j)),
            pl.BlockSpec((tf, d), lambda i, j: (j, 0)),
        ],
        out_shape=[jax.ShapeDtypeStruct((m, d), F32), jax.ShapeDtypeStruct((d, D_FF), BF16),
                   jax.ShapeDtypeStruct((d, D_FF), BF16), jax.ShapeDtypeStruct((D_FF, d), BF16)],
        scratch_shapes=[pltpu.VMEM((m, d), BF16)],
        compiler_params=_params("arbitrary", "arbitrary"),
        name="ffn_cast",
    )(x, g.reshape(1, d), w_gu, w_gu, w_down)


def _norm_matmul_kernel(x_ref, g_ref, w_ref, o_ref, h_ref):
    @pl.when(pl.program_id(1) == 0)
    def _():
        h_ref[...] = _norm_rows(x_ref[...], g_ref[...]).astype(BF16)

    o_ref[...] = jnp.dot(h_ref[...], w_ref[...], preferred_element_type=F32)


def norm_matmul(x, g, w, tm, tn=512):
    m, d = x.shape
    n = w.shape[1]
    return pl.pallas_call(
        _norm_matmul_kernel,
        grid=(m // tm, n // tn),
        in_specs=[
            pl.BlockSpec((tm, d), lambda i, j: (i, 0)),
            pl.BlockSpec((1, d), lambda i, j: (0, 0)),
            pl.BlockSpec((d, tn), lambda i, j: (0, j)),
        ],
        out_specs=pl.BlockSpec((tm, tn), lambda i, j: (i, j)),
        out_shape=jax.ShapeDtypeStruct((m, n), F32),
        scratch_shapes=[pltpu.VMEM((tm, d), BF16)],
        compiler_params=_params("parallel", "arbitrary"),
        name="in_proj",
    )(x, g.reshape(1, d), w)


def _norm_matmul_cast_kernel(n_main_tiles, x_ref, g_ref, w_ref, tail_ref, o_ref, w_out, h_ref):
    j = pl.program_id(1)

    @pl.when(j < n_main_tiles)
    def _():
        w_out[...] = w_ref[...].astype(BF16)

    @pl.when(j == n_main_tiles)
    def _():
        w_out[...] = tail_ref[...]

    _norm_matmul_kernel(x_ref, g_ref, w_out, o_ref, h_ref)


def norm_matmul_cast(x, g, w, tn=512):
    m, d = x.shape
    n = w.shape[1]
    k = n // tn
    tail = jnp.pad(w[:, k * tn:], ((0, 0), (0, (k + 1) * tn - n))).astype(BF16)
    return pl.pallas_call(
        functools.partial(_norm_matmul_cast_kernel, k),
        grid=(1, k + 1),
        in_specs=[
            pl.BlockSpec((m, d), lambda i, j: (0, 0)),
            pl.BlockSpec((1, d), lambda i, j: (0, 0)),
            pl.BlockSpec((d, tn), lambda i, j: (0, jnp.minimum(j, k - 1))),
            pl.BlockSpec((d, tn), lambda i, j: (0, 0)),
        ],
        out_specs=[pl.BlockSpec((m, tn), lambda i, j: (0, j)), pl.BlockSpec((d, tn), lambda i, j: (0, j))],
        out_shape=[jax.ShapeDtypeStruct((m, (k + 1) * tn), F32), jax.ShapeDtypeStruct((d, (k + 1) * tn), BF16)],
        scratch_shapes=[pltpu.VMEM((m, d), BF16)],
        compiler_params=_params("arbitrary", "arbitrary"),
        name="in_proj_cast",
    )(x, g.reshape(1, d), w, tail)


def _nsa_in_proj_kernel(tn, x_ref, g_ref, w_ref, o_ref, kv_ref, h_ref):
    j = pl.program_id(1)

    @pl.when(j == 0)
    def _():
        h_ref[...] = _norm_rows(x_ref[...], g_ref[...]).astype(BF16)

    res = jnp.dot(h_ref[...], w_ref[...], preferred_element_type=F32)
    o_ref[...] = res

    @pl.when((j >= NSA_QD // tn) & (j < (NSA_QD + 6 * NSA_KVD) // tn))
    def _():
        for g in range(NSA_KV_HEADS):
            kv_ref[:, g, :] = res[:, g * NSA_HEAD_DIM:(g + 1) * NSA_HEAD_DIM]


def nsa_in_proj(x, g, w, tm):
    m, d = x.shape
    n = w.shape[1]
    tn = NSA_KVD
    q_tiles = NSA_QD // tn

    def kv_index(i, j):
        t = jnp.clip(j - q_tiles, 0, 5)
        return (t // 2, i, t % 2, 0, 0)

    return pl.pallas_call(
        functools.partial(_nsa_in_proj_kernel, tn),
        grid=(m // tm, n // tn),
        in_specs=[
            pl.BlockSpec((tm, d), lambda i, j: (i, 0)),
            pl.BlockSpec((1, d), lambda i, j: (0, 0)),
            pl.BlockSpec((d, tn), lambda i, j: (0, j)),
        ],
        out_specs=[
            pl.BlockSpec((tm, tn), lambda i, j: (i, j)),
            pl.BlockSpec((None, tm, None, NSA_KV_HEADS, NSA_HEAD_DIM), kv_index),
        ],
        out_shape=[jax.ShapeDtypeStruct((m, n), F32),
                   jax.ShapeDtypeStruct((3, m, 2, NSA_KV_HEADS, NSA_HEAD_DIM), F32)],
        scratch_shapes=[pltpu.VMEM((tm, d), BF16)],
        compiler_params=_params("parallel", "arbitrary"),
        name="nsa_in_proj",
    )(x, g.reshape(1, d), w)


def _matmul_residual_kernel(a_ref, w_ref, x_ref, o_ref):
    o_ref[...] = x_ref[...] + jnp.dot(a_ref[...], w_ref[...], preferred_element_type=F32)


def matmul_residual(a, w, x, tm, tn=512):
    m, k = a.shape
    n = w.shape[1]
    return pl.pallas_call(
        _matmul_residual_kernel,
        grid=(m // tm, n // tn),
        in_specs=[
            pl.BlockSpec((tm, k), lambda i, j: (i, 0)),
            pl.BlockSpec((k, tn), lambda i, j: (0, j)),
            pl.BlockSpec((tm, tn), lambda i, j: (i, j)),
        ],
        out_specs=pl.BlockSpec((tm, tn), lambda i, j: (i, j)),
        out_shape=jax.ShapeDtypeStruct((m, n), F32),
        compiler_params=_params("parallel", "arbitrary"),
        name="out_proj",
    )(a, w, x)


def _matmul_residual_cast_kernel(a_ref, w_ref, x_ref, o_ref, w_out):
    w_out[...] = w_ref[...].astype(BF16)
    _matmul_residual_kernel(a_ref, w_out, x_ref, o_ref)


def matmul_residual_cast(a, w, x, tn=512):
    m, k = a.shape
    n = w.shape[1]
    return pl.pallas_call(
        _matmul_residual_cast_kernel,
        grid=(n // tn,),
        in_specs=[
            pl.BlockSpec((m, k), lambda j: (0, 0)),
            pl.BlockSpec((k, tn), lambda j: (0, j)),
            pl.BlockSpec((m, tn), lambda j: (0, j)),
        ],
        out_specs=[pl.BlockSpec((m, tn), lambda j: (0, j)), pl.BlockSpec((k, tn), lambda j: (0, j))],
        out_shape=[jax.ShapeDtypeStruct((m, n), F32), jax.ShapeDtypeStruct((k, n), BF16)],
        compiler_params=_params("arbitrary"),
        name="out_proj_cast",
    )(a, w, x)


def _ple_kernel(tn, x_ref, g_ref, wg_ref, p_ref, wp_ref, o_ref, h_ref):
    j = pl.program_id(1)

    @pl.when(j == 0)
    def _():
        h_ref[...] = _norm_rows(x_ref[...], g_ref[...]).astype(BF16)

    gate = jax.nn.sigmoid(jnp.dot(h_ref[...], wg_ref[...], preferred_element_type=F32))
    proj = jnp.dot(p_ref[...].astype(BF16), wp_ref[...], preferred_element_type=F32)
    col = pl.multiple_of(j * tn, tn)
    o_ref[...] = x_ref[:, pl.ds(col, tn)] + gate * proj


def ple(x, p, g, w_gate, w_proj, tm, tn=512):
    m, d = x.shape
    pd = p.shape[1]
    return pl.pallas_call(
        functools.partial(_ple_kernel, tn),
        grid=(m // tm, d // tn),
        in_specs=[
            pl.BlockSpec((tm, d), lambda i, j: (i, 0)),
            pl.BlockSpec((1, d), lambda i, j: (0, 0)),
            pl.BlockSpec((d, tn), lambda i, j: (0, j)),
            pl.BlockSpec((tm, pd), lambda i, j: (i, 0)),
            pl.BlockSpec((pd, tn), lambda i, j: (0, j)),
        ],
        out_specs=pl.BlockSpec((tm, tn), lambda i, j: (i, j)),
        out_shape=jax.ShapeDtypeStruct((m, d), F32),
        scratch_shapes=[pltpu.VMEM((tm, d), BF16)],
        compiler_params=_params("parallel", "arbitrary"),
        name="ple",
    )(x, g.reshape(1, d), w_gate, p, w_proj)


def _ple_cast_kernel(tn, x_ref, g_ref, wg_ref, p_ref, wp_ref, o_ref, wg_out, wp_out, h_ref):
    wg_out[...] = wg_ref[...].astype(BF16)
    wp_out[...] = wp_ref[...].astype(BF16)
    _ple_kernel(tn, x_ref, g_ref, wg_out, p_ref, wp_out, o_ref, h_ref)


def ple_cast(x, p, g, w_gate, w_proj, layer, tn=512):
    m, d = x.shape
    pd = p.shape[1]
    return pl.pallas_call(
        functools.partial(_ple_cast_kernel, tn),
        grid=(1, d // tn),
        in_specs=[
            pl.BlockSpec((m, d), lambda i, j: (0, 0)),
            pl.BlockSpec((1, d), lambda i, j: (0, 0)),
            pl.BlockSpec((None, d, tn), lambda i, j: (layer, 0, j)),
            pl.BlockSpec((m, pd), lambda i, j: (0, 0)),
            pl.BlockSpec((None, pd, tn), lambda i, j: (layer, 0, j)),
        ],
        out_specs=[
            pl.BlockSpec((m, tn), lambda i, j: (0, j)),
            pl.BlockSpec((d, tn), lambda i, j: (0, j)),
            pl.BlockSpec((pd, tn), lambda i, j: (0, j)),
        ],
        out_shape=[jax.ShapeDtypeStruct((m, d), F32), jax.ShapeDtypeStruct((d, d), BF16),
                   jax.ShapeDtypeStruct((pd, d), BF16)],
        scratch_shapes=[pltpu.VMEM((m, d), BF16)],
        compiler_params=_params("arbitrary", "arbitrary"),
        name="ple_cast",
    )(x, g.reshape(1, d), w_gate, p, w_proj)


def _rmsnorm_kernel(x_ref, g_ref, o_ref):
    o_ref[...] = _norm_rows(x_ref[...], g_ref[...])


def rmsnorm(x, g, tm):
    m, d = x.shape
    return pl.pallas_call(
        _rmsnorm_kernel,
        grid=(m // tm,),
        in_specs=[pl.BlockSpec((tm, d), lambda i: (i, 0)), pl.BlockSpec((1, d), lambda i: (0, 0))],
        out_specs=pl.BlockSpec((tm, d), lambda i: (i, 0)),
        out_shape=jax.ShapeDtypeStruct((m, d), F32),
        compiler_params=_params("parallel"),
    )(x, g.reshape(1, d))


Z_GATE_COL = (NSA_QD + 6 * NSA_KVD) // LANE


def _compress_kernel(nc, x_ref, w2_ref, wf_ref, pe_ref, o_ref):
    acc = jnp.zeros((nc, 2 * NSA_HEAD_DIM), F32)
    for l in range(CMP_STRIDE):
        a = x_ref[pl.ds(l, nc, stride=CMP_STRIDE), :]
        acc += jnp.dot(a.astype(BF16), w2_ref[l], preferred_element_type=F32)
    n_idx = lax.broadcasted_iota(jnp.int32, (nc, NSA_HEAD_DIM), 0)
    first = acc[:, :NSA_HEAD_DIM]
    second = jnp.where(n_idx == nc - 1, 0.0, pltpu.roll(acc[:, NSA_HEAD_DIM:], nc - 1, 0))
    bias = jnp.dot(pe_ref[...], wf_ref[...], preferred_element_type=F32)[0:1]
    o_ref[...] = (first + second + bias).astype(BF16)


def compress(x, nb, nc, col0, w_cmp, pe_cmp):
    hd = NSA_HEAD_DIM
    w2 = w_cmp.reshape(2, 2, CMP_STRIDE, hd, hd).transpose(0, 2, 3, 1, 4).reshape(2, CMP_STRIDE, hd, 2 * hd)
    wf = w_cmp.reshape(2, CMP_LEN * hd, hd)
    pe = jnp.broadcast_to(pe_cmp.reshape(2, 1, CMP_LEN * hd), (2, 8, CMP_LEN * hd))
    return pl.pallas_call(
        functools.partial(_compress_kernel, nc),
        grid=(nb, 2, NSA_KV_HEADS),
        in_specs=[
            pl.BlockSpec((CMP_STRIDE * nc, hd), lambda i, c, g: (i, col0 + c * NSA_KV_HEADS + g)),
            pl.BlockSpec((None, CMP_STRIDE, hd, 2 * hd), lambda i, c, g: (c, 0, 0, 0)),
            pl.BlockSpec((None, CMP_LEN * hd, hd), lambda i, c, g: (c, 0, 0)),
            pl.BlockSpec((None, 8, CMP_LEN * hd), lambda i, c, g: (c, 0, 0)),
        ],
        out_specs=pl.BlockSpec((None, None, None, nc, hd), lambda i, c, g: (i, c, g, 0, 0)),
        out_shape=jax.ShapeDtypeStruct((nb, 2, NSA_KV_HEADS, nc, hd), BF16),
        compiler_params=_params("parallel", "parallel", "parallel"),
        name="nsa_compress",
    )(x, w2.astype(BF16), wf.astype(BF16), pe.astype(BF16))


def _nsa_prompt_kernel(t, qb, kb, nc, slope_ref, zq_ref, ks_ref, vs_ref, kw_ref, vw_ref, gz_ref, gb_ref,
                       ck_ref, cv_ref, ov_ref, o_ref, m_ref, l_ref, acc_ref, p_ref):
    g = pl.program_id(1)
    qi = pl.program_id(2)
    t0 = qi * qb
    r8 = NSA_Q_PER_KV
    hd = NSA_HEAD_DIM
    n_slc = t // SEL_BLOCK
    n_cmp = nc - 1

    zq = zq_ref[...] * (hd ** -0.5)
    q2 = jnp.concatenate([zq[:, r * hd:(r + 1) * hd] for r in range(r8)], axis=0).astype(BF16)

    slope = [slope_ref[g * r8 + r] for r in range(r8)]
    tq = lax.broadcasted_iota(jnp.int32, (qb, 1), 0) + t0
    rows = lambda r: slice(r * qb, (r + 1) * qb)

    n_i = lax.broadcasted_iota(jnp.int32, (qb, nc), 1)
    dist_ci = tq - (CMP_STRIDE * n_i + (CMP_LEN - 1))
    vis_c = (dist_ci >= 0) & (n_i < n_cmp)
    dist_c = dist_ci.astype(F32)
    s_all = lax.dot_general(q2, ck_ref[...], (((1,), (1,)), ((), ())), preferred_element_type=F32)
    psum = jnp.zeros((qb, nc), F32)
    for r in range(r8):
        s = jnp.where(vis_c, s_all[rows(r)] - slope[r] * dist_c, NEG_INF)
        e = jnp.exp(s - jnp.max(s, axis=-1, keepdims=True))
        p = jnp.where(vis_c, e / jnp.sum(e, axis=-1, keepdims=True), 0.0)
        psum = psum + p
        p_ref[rows(r), :nc] = p.astype(BF16)
    o_cmp = jnp.dot(p_ref[:, :nc], cv_ref[...], preferred_element_type=F32)

    p_hi = psum.astype(BF16)
    p_mid = (psum - p_hi.astype(F32)).astype(BF16)
    p_lo = (psum - p_hi.astype(F32) - p_mid.astype(F32)).astype(BF16)
    ov = ov_ref[...]
    imp = (jnp.dot(p_hi, ov, preferred_element_type=F32) + jnp.dot(p_mid, ov, preferred_element_type=F32)
           + jnp.dot(p_lo, ov, preferred_element_type=F32))
    nj = -(-n_slc // 8) * 8
    j_t = lax.broadcasted_iota(jnp.int32, (nj, qb), 0)
    cur = (t0 + lax.broadcasted_iota(jnp.int32, (nj, qb), 1)) // SEL_BLOCK
    vis_j = j_t <= cur
    forced = (j_t == 0) | (j_t == cur) | (j_t == cur - 1)
    score = jnp.where(forced, FORCE_SCORE, jnp.where(vis_j, imp.T[:nj], -FORCE_SCORE))
    score = jnp.where(j_t < n_slc, score, -jnp.inf)
    rank = jnp.zeros((nj, qb), jnp.int32)
    for jp in range(n_slc):
        row = score[jp:jp + 1, :]
        ahead = (row > score) | ((row == score) & (jp < j_t))
        rank = rank + ahead.astype(jnp.int32)
    selected = jnp.where((rank < min(N_SEL, n_slc)) & vis_j, 1.0, 0.0).astype(BF16)

    d0 = (lax.broadcasted_iota(jnp.int32, (qb, kb), 0) - lax.broadcasted_iota(jnp.int32, (qb, kb), 1))

    def flash_init():
        m_ref[...] = jnp.full(m_ref.shape, NEG_INF, F32)
        l_ref[...] = jnp.zeros(l_ref.shape, F32)
        acc_ref[...] = jnp.zeros(acc_ref.shape, F32)

    def flash_step(k_ref, v_ref, k0, mask, dist):
        k = k_ref[pl.ds(k0, kb), :].astype(BF16)
        v = v_ref[pl.ds(k0, kb), :].astype(BF16)
        s_all = lax.dot_general(q2, k, (((1,), (1,)), ((), ())), preferred_element_type=F32)
        alphas = []
        for r in range(r8):
            s = jnp.where(mask, s_all[rows(r)] - slope[r] * dist, NEG_INF)
            m_prev = m_ref[r]
            m_next = jnp.maximum(m_prev, jnp.max(s, axis=-1, keepdims=True))
            alpha = jnp.exp(m_prev - m_next)
            p = jnp.exp(s - jnp.concatenate([m_next] * (kb // LANE), axis=1))
            l_ref[r] = alpha * l_ref[r] + jnp.sum(p, axis=-1, keepdims=True)
            m_ref[r] = m_next
            p_ref[rows(r), :kb] = p.astype(BF16)
            alphas.append(alpha)
        pv = jnp.dot(p_ref[:, :kb], v, preferred_element_type=F32)
        for r in range(r8):
            acc_ref[r] = alphas[r] * acc_ref[r] + pv[rows(r)]

    flash_init()

    def sel_body(i, carry):
        k0 = pl.multiple_of(i * kb, kb)
        dist = d0 + (t0 - k0)
        blk_of_key = (lax.broadcasted_iota(jnp.int32, (nj, kb), 1) + k0) // SEL_BLOCK
        expand = jnp.where(blk_of_key == lax.broadcasted_iota(jnp.int32, (nj, kb), 0), 1.0, 0.0).astype(BF16)
        chosen = lax.dot_general(selected, expand, _TN, preferred_element_type=F32)
        flash_step(ks_ref, vs_ref, k0, (chosen > 0.5) & (dist >= 0), dist.astype(F32))
        return carry

    lax.fori_loop(0, (t0 + qb - 1) // kb + 1, sel_body, 0)
    o_sel = [acc_ref[r] / l_ref[r] for r in range(r8)]

    wk = WINDOW + qb
    kw0 = pl.multiple_of(jnp.maximum(t0 - WINDOW, 0), qb)
    k_w = kw_ref[pl.ds(kw0, wk), :].astype(BF16)
    v_w = vw_ref[pl.ds(kw0, wk), :].astype(BF16)
    dist_w = (lax.broadcasted_iota(jnp.int32, (qb, wk), 0) - lax.broadcasted_iota(jnp.int32, (qb, wk), 1)) + (t0 - kw0)
    mask_w = (dist_w >= 0) & (dist_w <= WINDOW)
    dist_wf = dist_w.astype(F32)
    s_all = lax.dot_general(q2, k_w, _NT, preferred_element_type=F32)
    l_win = []
    for r in range(r8):
        s = jnp.where(mask_w, s_all[rows(r)] - slope[r] * dist_wf, NEG_INF)
        p = jnp.exp(s - jnp.max(s, axis=-1, keepdims=True))
        l_win.append(jnp.sum(p, axis=-1, keepdims=True))
        p_ref[rows(r), :wk] = p.astype(BF16)
    pv = jnp.dot(p_ref[:, :wk], v_w, preferred_element_type=F32)
    o_win = [pv[rows(r)] / l_win[r] for r in range(r8)]

    gates = jax.nn.sigmoid(gz_ref[...] + gb_ref[...])
    src = lax.broadcasted_iota(jnp.int32, (LANE, LANE), 0)
    dst = lax.broadcasted_iota(jnp.int32, (LANE, LANE), 1)
    pick = jnp.where((dst < 3 * r8) & (src == (dst // r8) * NSA_HEADS + g * r8 + dst % r8), 1.0, 0.0).astype(BF16)
    g_hi = gates.astype(BF16)
    g_mid = (gates - g_hi.astype(F32)).astype(BF16)
    g_lo = (gates - g_hi.astype(F32) - g_mid.astype(F32)).astype(BF16)
    gsel = (jnp.dot(g_hi, pick, preferred_element_type=F32) + jnp.dot(g_mid, pick, preferred_element_type=F32)
            + jnp.dot(g_lo, pick, preferred_element_type=F32))

    def gate(branch, r):
        k = branch * r8 + r
        return gsel[:, k:k + 1]

    for r in range(r8):
        o = gate(0, r) * o_cmp[rows(r)] + gate(1, r) * o_sel[r] + gate(2, r) * o_win[r]
        o_ref[:, r * hd:(r + 1) * hd] = o.astype(BF16)


def nsa_prompt_attention(z, ckv, gate_bias, b, t, qb=128, kb=512):
    assert t % kb == 0 and t >= WINDOW + qb and WINDOW % qb == 0
    nc = t // CMP_STRIDE
    nq = t // qb
    hd = NSA_HEAD_DIM
    n_slc = t // SEL_BLOCK
    slopes = jnp.exp2(-8.0 * (jnp.arange(NSA_HEADS, dtype=F32) + 1.0) / NSA_HEADS)
    ci = jnp.arange(nc)[:, None]
    sj = jnp.arange(LANE)[None, :]
    ov = ((CMP_STRIDE * ci < SEL_BLOCK * (sj + 1)) & (CMP_STRIDE * ci + CMP_LEN > SEL_BLOCK * sj)
          & (sj < n_slc)).astype(BF16)
    gb = jnp.pad(gate_bias, (0, LANE - gate_bias.shape[0])).reshape(1, LANE)
    kv_spec = lambda off: pl.BlockSpec((t, hd), lambda bi, g, qi: (bi, off + g))
    kv_col = (NSA_QD + 2 * NSA_KVD) // hd
    return pl.pallas_call(
        functools.partial(_nsa_prompt_kernel, t, qb, kb, nc),
        grid=(b, NSA_KV_HEADS, nq),
        in_specs=[
            pl.BlockSpec(memory_space=pltpu.SMEM),
            pl.BlockSpec((qb, NSA_Q_PER_KV * hd), lambda bi, g, qi: (bi * nq + qi, g)),
            kv_spec(kv_col), kv_spec(kv_col + 4), kv_spec(kv_col + 8), kv_spec(kv_col + 12),
            pl.BlockSpec((qb, LANE), lambda bi, g, qi: (bi * nq + qi, Z_GATE_COL)),
            pl.BlockSpec((1, LANE), lambda bi, g, qi: (0, 0)),
            pl.BlockSpec((None, None, None, nc, hd), lambda bi, g, qi: (bi, 0, g, 0, 0)),
            pl.BlockSpec((None, None, None, nc, hd), lambda bi, g, qi: (bi, 1, g, 0, 0)),
            pl.BlockSpec((nc, LANE), lambda bi, g, qi: (0, 0)),
        ],
        out_specs=pl.BlockSpec((qb, NSA_Q_PER_KV * hd), lambda bi, g, qi: (bi * nq + qi, g)),
        out_shape=jax.ShapeDtypeStruct((b * t, NSA_QD), BF16),
        scratch_shapes=[pltpu.VMEM((NSA_Q_PER_KV, qb, LANE), F32), pltpu.VMEM((NSA_Q_PER_KV, qb, LANE), F32),
                        pltpu.VMEM((NSA_Q_PER_KV, qb, hd), F32),
                        pltpu.VMEM((NSA_Q_PER_KV * qb, max(kb, nc, WINDOW + qb)), BF16)],
        compiler_params=_params("parallel", "parallel", "arbitrary"),
        name="nsa_prompt_attn",
    )(slopes, z, z, z, z, z, z, gb, ckv, ckv, ov)


_NT = (((1,), (1,)), ((), ()))
_TN = (((0,), (0,)), ((), ()))


GATHER_PAGES = 4


def _page_gather_kernel(n_steps, pt_ref, *refs):
    page_refs, tail_ref, o_ref = refs[:GATHER_PAGES], refs[GATHER_PAGES], refs[GATHER_PAGES + 1]
    p = pl.program_id(1)

    @pl.when(p < n_steps)
    def _():
        for k, x_ref in enumerate(page_refs):
            for c in range(2):
                for g in range(NSA_KV_HEADS):
                    col = (c * NSA_KV_HEADS + g) * NSA_HEAD_DIM
                    o_ref[k * PAGE_SIZE:(k + 1) * PAGE_SIZE, col:col + NSA_HEAD_DIM] = x_ref[:, c, g, :]

    @pl.when(p == n_steps)
    def _():
        o_ref[...] = tail_ref[...]


def page_gather(cache, page_table, tail):
    nb, n_pages = page_table.shape
    step_rows = GATHER_PAGES * PAGE_SIZE
    n_steps = n_pages // GATHER_PAGES
    w = tail.shape[2]

    def page_spec(k):
        def index(b, p, pt):
            return (pt[b, jnp.minimum(p, n_steps - 1) * GATHER_PAGES + k], 0, 0, 0, 0)
        return pl.BlockSpec((None, PAGE_SIZE, 2, NSA_KV_HEADS, NSA_HEAD_DIM), index)

    return pl.pallas_call(
        functools.partial(_page_gather_kernel, n_steps),
        grid_spec=pltpu.PrefetchScalarGridSpec(
            num_scalar_prefetch=1,
            grid=(nb, n_steps + 1),
            in_specs=[page_spec(k) for k in range(GATHER_PAGES)]
            + [pl.BlockSpec((None, step_rows, w), lambda b, p, pt: (b, 0, 0))],
            out_specs=pl.BlockSpec((step_rows, w), lambda b, p, pt: (b * (n_steps + 1) + p, 0)),
        ),
        out_shape=jax.ShapeDtypeStruct((nb * (n_steps + 1) * step_rows, w), F32),
        compiler_params=_params("parallel", "arbitrary"),
        name="nsa_page_gather",
    )(page_table, *([cache] * GATHER_PAGES), tail)


def _row_slopes(slope_ref, g, n_rows, t):
    r_idx = lax.broadcasted_iota(jnp.int32, (n_rows, 1), 0) // t
    col = jnp.zeros((n_rows, 1), F32)
    for r in range(NSA_Q_PER_KV):
        col = jnp.where(r_idx == r, slope_ref[g * NSA_Q_PER_KV + r], col)
    return col


def _nsa_sample_select_kernel(t, pos0, nc, n_cmp, n_slc, slope_ref, q_ref, ck_ref, cv_ref, ov_ref,
                              ocmp_ref, idx_ref, valid_ref):
    g = pl.program_id(0)
    r8 = NSA_Q_PER_KV
    n_rows = r8 * t
    nsp = ov_ref.shape[1]
    q2 = (q_ref[...] * (NSA_HEAD_DIM ** -0.5)).astype(BF16)
    slope = _row_slopes(slope_ref, g, n_rows, t)
    tq = pos0 + lax.broadcasted_iota(jnp.int32, (n_rows, 1), 0) % t
    n_i = lax.broadcasted_iota(jnp.int32, (n_rows, nc), 1)
    dist_ci = tq - (CMP_STRIDE * n_i + (CMP_LEN - 1))
    vis_c = (dist_ci >= 0) & (n_i < n_cmp)
    s = lax.dot_general(q2, ck_ref[...], _NT, preferred_element_type=F32)
    s = jnp.where(vis_c, s - slope * dist_ci.astype(F32), NEG_INF)
    e = jnp.exp(s - jnp.max(s, axis=-1, keepdims=True))
    p = jnp.where(vis_c, e / jnp.sum(e, axis=-1, keepdims=True), 0.0)
    ocmp_ref[...] = jnp.dot(p.astype(BF16), cv_ref[...], preferred_element_type=F32)

    p_hi = p.astype(BF16)
    p_mid = (p - p_hi.astype(F32)).astype(BF16)
    p_lo = (p - p_hi.astype(F32) - p_mid.astype(F32)).astype(BF16)
    ov = ov_ref[...]
    imp = (jnp.dot(p_hi, ov, preferred_element_type=F32) + jnp.dot(p_mid, ov, preferred_element_type=F32)
           + jnp.dot(p_lo, ov, preferred_element_type=F32))
    sh = t
    while sh < n_rows:
        imp = imp + pltpu.roll(imp, sh, 0)
        sh *= 2
    imp = imp[0:8]
    j_i = lax.broadcasted_iota(jnp.int32, (8, nsp), 1)
    cur = (pos0 + lax.broadcasted_iota(jnp.int32, (8, 1), 0) % t) // SEL_BLOCK
    vis_j = j_i <= cur
    forced = (j_i == 0) | (j_i == cur) | (j_i == cur - 1)
    score = jnp.where(forced, FORCE_SCORE, jnp.where(vis_j, imp, -FORCE_SCORE))
    score = jnp.where(j_i < n_slc, score, -jnp.inf)
    rank = jnp.zeros((8, nsp), jnp.int32)
    for jp in range(n_slc):
        col = score[:, jp:jp + 1]
        ahead = (col > score) | ((col == score) & (jp < j_i))
        rank = rank + ahead.astype(jnp.int32)
    lane = lax.broadcasted_iota(jnp.int32, (8, LANE), 1)
    idx = jnp.zeros((8, LANE), F32)
    valid = jnp.zeros((8, LANE), F32)
    j_f = j_i.astype(F32)
    for k in range(min(N_SEL, n_slc)):
        hit = (rank == k) & (j_i < n_slc)
        idx_k = jnp.sum(jnp.where(hit, j_f, 0.0), axis=-1, keepdims=True)
        valid_k = jnp.sum(jnp.where(hit & vis_j, 1.0, 0.0), axis=-1, keepdims=True)
        idx = jnp.where(lane == k, idx_k, idx)
        valid = jnp.where(lane == k, valid_k, valid)
    idx_ref[...] = idx.astype(jnp.int32)
    valid_ref[...] = valid.astype(jnp.int32)


def nsa_sample_select(q, ckv, nb, t, pos0, n_cmp, n_slc):
    g4, hd = NSA_KV_HEADS, NSA_HEAD_DIM
    nc = ckv.shape[3]
    n_rows = NSA_Q_PER_KV * t
    nsp = -(-n_slc // LANE) * LANE
    slopes = jnp.exp2(-8.0 * (jnp.arange(NSA_HEADS, dtype=F32) + 1.0) / NSA_HEADS)
    ci = jnp.arange(nc)[:, None]
    sj = jnp.arange(nsp)[None, :]
    ov = ((CMP_STRIDE * ci < SEL_BLOCK * (sj + 1)) & (CMP_STRIDE * ci + CMP_LEN > SEL_BLOCK * sj)
          & (sj < n_slc) & (ci < n_cmp)).astype(BF16)
    return pl.pallas_call(
        functools.partial(_nsa_sample_select_kernel, t, pos0, nc, n_cmp, n_slc),
        grid=(g4, nb),
        in_specs=[
            pl.BlockSpec(memory_space=pltpu.SMEM),
            pl.BlockSpec((None, None, n_rows, hd), lambda g, b: (g, b, 0, 0)),
            pl.BlockSpec((None, None, None, nc, hd), lambda g, b: (b, 0, g, 0, 0)),
            pl.BlockSpec((None, None, None, nc, hd), lambda g, b: (b, 1, g, 0, 0)),
            pl.BlockSpec((nc, nsp), lambda g, b: (0, 0)),
        ],
        out_specs=[
            pl.BlockSpec((None, None, n_rows, hd), lambda g, b: (g, b, 0, 0)),
            pl.BlockSpec((None, None, 8, LANE), lambda g, b: (g, b, 0, 0)),
            pl.BlockSpec((None, None, 8, LANE), lambda g, b: (g, b, 0, 0)),
        ],
        out_shape=[jax.ShapeDtypeStruct((g4, nb, n_rows, hd), F32),
                   jax.ShapeDtypeStruct((g4, nb, 8, LANE), jnp.int32),
                   jax.ShapeDtypeStruct((g4, nb, 8, LANE), jnp.int32)],
        compiler_params=_params("parallel", "parallel"),
        name="nsa_sample_select",
    )(slopes, q, ckv, ckv, ov)


def _nsa_sample_attn_kernel(nb, t, pos0, n_pages, wb, idx_ref, valid_ref, pt_ref, slope_ref, q_ref, ocmp_ref,
                            gl_ref, gb_ref, kwin_ref, vwin_ref, kwt_ref, vwt_ref, cache_ref, tail_ref, o_ref,
                            kbuf, vbuf, q2_ref, osel_ref, sem):
    b = pl.program_id(0)
    g = pl.program_id(1)
    r8 = NSA_Q_PER_KV
    hd = NSA_HEAD_DIM
    n_rows = r8 * t
    n_sel = kbuf.shape[1]
    past_blocks = n_pages * (PAGE_SIZE // SEL_BLOCK)
    per_page = PAGE_SIZE // SEL_BLOCK

    def sel_entry(tt, k):
        return ((g * nb + b) * t + tt) * n_sel + k

    def block_copies(tt, k, in_tail):
        i = idx_ref[sel_entry(tt, k)]
        out = []
        for c, buf in ((0, kbuf), (1, vbuf)):
            col = c * NSA_KV_HEADS + g
            if in_tail:
                src = tail_ref.at[b, :, pl.ds(col * hd, hd)]
            else:
                ic = jnp.minimum(i, past_blocks - 1)
                page = pt_ref[b * n_pages + ic // per_page]
                src = cache_ref.at[page, pl.ds((ic % per_page) * SEL_BLOCK, SEL_BLOCK), c, g, :]
            out.append(pltpu.make_async_copy(src, buf.at[tt, k], sem))
        return out

    def for_each_block(fn):
        for tt in range(t):
            for k in range(n_sel):
                in_tail = idx_ref[sel_entry(tt, k)] >= past_blocks

                @pl.when(in_tail)
                def _():
                    for cp in block_copies(tt, k, True):
                        fn(cp)

                @pl.when(jnp.logical_not(in_tail))
                def _():
                    for cp in block_copies(tt, k, False):
                        fn(cp)

    for_each_block(lambda cp: cp.start())

    q2_ref[...] = q_ref[...] * (hd ** -0.5)
    q2 = q2_ref[...].astype(BF16)
    slope = _row_slopes(slope_ref, g, n_rows, t)
    tq = pos0 + lax.broadcasted_iota(jnp.int32, (n_rows, 1), 0) % t

    kw = jnp.concatenate([kwin_ref[...], kwt_ref[...]], axis=0).astype(BF16)
    vw = jnp.concatenate([vwin_ref[...], vwt_ref[...]], axis=0).astype(BF16)
    nk = kw.shape[0]
    s_pos = (pos0 - wb) + lax.broadcasted_iota(jnp.int32, (n_rows, nk), 1)
    dist = tq - s_pos
    mask = (dist >= 0) & (dist <= WINDOW) & (s_pos >= 0)
    s = lax.dot_general(q2, kw, _NT, preferred_element_type=F32)
    s = jnp.where(mask, s - slope * dist.astype(F32), NEG_INF)
    e = jnp.exp(s - jnp.max(s, axis=-1, keepdims=True))
    p = e / jnp.sum(e, axis=-1, keepdims=True)
    o_win = jnp.dot(p.astype(BF16), vw, preferred_element_type=F32)

    for_each_block(lambda cp: cp.wait())

    nkeys = n_sel * SEL_BLOCK
    key_lane = lax.broadcasted_iota(jnp.int32, (1, nkeys), 1)
    slope8 = _row_slopes(slope_ref, g, r8, 1)
    for tt in range(t):
        blk = jnp.zeros((1, nkeys), jnp.int32)
        ok = jnp.zeros((1, nkeys), jnp.int32)
        for k in range(n_sel):
            here = key_lane // SEL_BLOCK == k
            blk = jnp.where(here, idx_ref[sel_entry(tt, k)], blk)
            ok = jnp.where(here, valid_ref[sel_entry(tt, k)], ok)
        dist = (pos0 + tt) - (blk * SEL_BLOCK + key_lane % SEL_BLOCK)
        mask = (dist >= 0) & (ok > 0)
        q_t = q2_ref[pl.ds(tt, r8, stride=t), :].astype(BF16)
        k_t = kbuf[tt].reshape(nkeys, hd).astype(BF16)
        v_t = vbuf[tt].reshape(nkeys, hd).astype(BF16)
        s = lax.dot_general(q_t, k_t, _NT, preferred_element_type=F32)
        s = jnp.where(mask, s - slope8 * dist.astype(F32), NEG_INF)
        e = jnp.exp(s - jnp.max(s, axis=-1, keepdims=True))
        p = e / jnp.sum(e, axis=-1, keepdims=True)
        osel_ref[pl.ds(tt, r8, stride=t), :] = jnp.dot(p.astype(BF16), v_t, preferred_element_type=F32)

    gates = jax.nn.sigmoid(gl_ref[...] + gb_ref[...])
    o = gates[:, 0:1] * ocmp_ref[...] + gates[:, 1:2] * osel_ref[...] + gates[:, 2:3] * o_win
    o_ref[...] = o.astype(BF16)


def nsa_sample_attention(q, o_cmp, gl, gb, idx, valid, page_table, cache_sel, tail_sel, cache_win, tail_win,
                         nb, t, pos0):
    g4, hd = NSA_KV_HEADS, NSA_HEAD_DIM
    n_rows = NSA_Q_PER_KV * t
    n_pages = page_table.shape[1]
    n_sel = idx.shape[-1]
    wb = cache_win.shape[1]
    slopes = jnp.exp2(-8.0 * (jnp.arange(NSA_HEADS, dtype=F32) + 1.0) / NSA_HEADS)
    row_spec = lambda w: pl.BlockSpec((None, None, n_rows, w), lambda b, g, *_: (g, b, 0, 0))
    win_spec = lambda rows, c: pl.BlockSpec((None, rows, hd), lambda b, g, *_: (b, 0, c * g4 + g))
    return pl.pallas_call(
        functools.partial(_nsa_sample_attn_kernel, nb, t, pos0, n_pages, wb),
        grid_spec=pltpu.PrefetchScalarGridSpec(
            num_scalar_prefetch=3,
            grid=(nb, g4),
            in_specs=[
                pl.BlockSpec(memory_space=pltpu.SMEM),
                row_spec(hd), row_spec(hd), row_spec(3),
                pl.BlockSpec((None, n_rows, 3), lambda b, g, *_: (g, 0, 0)),
                win_spec(wb, 0), win_spec(wb, 1), win_spec(LANE, 0), win_spec(LANE, 1),
                pl.BlockSpec(memory_space=pl.ANY), pl.BlockSpec(memory_space=pl.ANY),
            ],
            out_specs=row_spec(hd),
            scratch_shapes=[pltpu.VMEM((t, n_sel, SEL_BLOCK, hd), F32), pltpu.VMEM((t, n_sel, SEL_BLOCK, hd), F32),
                            pltpu.VMEM((n_rows, hd), F32), pltpu.VMEM((n_rows, hd), F32),
                            pltpu.SemaphoreType.DMA(())],
        ),
        out_shape=jax.ShapeDtypeStruct((g4, nb, n_rows, hd), BF16),
        compiler_params=_params("arbitrary", "arbitrary"),
        name="nsa_sample_attn",
    )(idx.reshape(-1), valid.reshape(-1), page_table.reshape(-1), slopes, q, o_cmp, gl, gb,
      cache_win, cache_win, tail_win, tail_win, cache_sel, tail_sel)


def nsa_sample_mix(z, nb, t, cache_cmp, cache_sel, cache_win, page_table, gate_bias, w_cmp, pe_cmp):
    g4, r8, hd = NSA_KV_HEADS, NSA_Q_PER_KV, NSA_HEAD_DIM
    n_pages = page_table.shape[1]
    past = n_pages * PAGE_SIZE
    wb = cache_win.shape[1]
    t_all = past + t
    tp = -(-t_all // SEL_BLOCK) * SEL_BLOCK
    n_slc = tp // SEL_BLOCK
    n_cmp = tp // CMP_STRIDE - 1
    gather_rows = (n_pages + GATHER_PAGES) * PAGE_SIZE
    nc = gather_rows // CMP_STRIDE
    kv = z[:, NSA_QD:NSA_QD + 6 * NSA_KVD].reshape(nb, t, 3, 2 * NSA_KVD)

    tail_cmp = jnp.pad(kv[:, :, 0], ((0, 0), (0, gather_rows - past - t), (0, 0)))
    rows = page_gather(cache_cmp, page_table, tail_cmp)
    ckv = compress(rows, nb, nc, 0, w_cmp, pe_cmp)

    q = z[:, :NSA_QD].reshape(nb, t, g4, r8, hd).transpose(2, 0, 3, 1, 4).reshape(g4, nb, r8 * t, hd)
    o_cmp, idx, valid = nsa_sample_select(q, ckv, nb, t, past, n_cmp, n_slc)
    n_sel = min(N_SEL, n_slc)
    idx = idx[:, :, :t, :n_sel]
    valid = valid[:, :, :t, :n_sel]

    gl = z[:, NSA_QD + 6 * NSA_KVD:NSA_QD + 6 * NSA_KVD + 3 * NSA_HEADS].reshape(nb, t, 3, g4, r8)
    gl = gl.transpose(3, 0, 4, 1, 2).reshape(g4, nb, r8 * t, 3)
    gb = jnp.broadcast_to(gate_bias.reshape(3, g4, r8, 1), (3, g4, r8, t)).transpose(1, 2, 3, 0).reshape(g4, r8 * t, 3)
    tail_sel = jnp.pad(kv[:, :, 1], ((0, 0), (0, SEL_BLOCK - t), (0, 0)))
    tail_win = jnp.pad(kv[:, :, 2], ((0, 0), (0, LANE - t), (0, 0)))
    o = nsa_sample_attention(q, o_cmp, gl, gb, idx, valid, page_table,
                             cache_sel, tail_sel,
                             cache_win.reshape(nb, wb, 2 * NSA_KVD), tail_win, nb, t, past)
    o = o.reshape(g4, nb, r8, t, hd).transpose(1, 3, 0, 2, 4).reshape(nb * t, NSA_QD)
    kv6 = kv.reshape(nb, t, 3, 2, g4, hd)
    win_new = jnp.concatenate([cache_win, kv6[:, :, 2]], axis=1)[:, t:]
    return o, kv6[:, :, 0], kv6[:, :, 1], win_new


GLA_QK = GLA_HEADS * GLA_DK
GLA_VD = GLA_HEADS * GLA_DV
GLA_SUPER = 256
GLA_HEAD_PAIR = 2


def _chunk_scan(la, c):
    n = la.shape[0]
    ri = lax.broadcasted_iota(jnp.int32, la.shape, 0) % c
    b = la
    s = 1
    while s < c:
        b = b + jnp.where(ri >= s, pltpu.roll(b, s, 0), 0.0)
        s *= 2
    tot = jnp.where(ri == c - 1, b, 0.0)
    s = 1
    while s < c:
        tot = tot + jnp.where(ri + s < c, pltpu.roll(tot, n - s, 0), 0.0)
        s *= 2
    return b, tot


def _gla_gates(a_rows, wa_ref, ba_ref):
    x = jnp.dot(a_rows.astype(BF16), wa_ref[...], preferred_element_type=F32) + ba_ref[...]
    return jax.nn.log_sigmoid(x) / GLA_TAU


def _gla_finish(o, r, g_ref):
    return (_norm_rows(o, g_ref[...]) * (r * jax.nn.sigmoid(r))).astype(BF16)


def _gla_prompt_kernel(q_ref, k_ref, v_ref, r_ref, a_ref, wa_ref, ba_ref, g_ref, o_ref, s_ref, st_ref, oc_ref):
    c, sc, hp, dk, dv = GLA_CHUNK, GLA_SUPER, GLA_HEAD_PAIR, GLA_DK, GLA_DV
    step = pl.program_id(2)

    @pl.when(step == 0)
    def _():
        st_ref[...] = jnp.zeros(st_ref.shape, F32)

    ti = lax.broadcasted_iota(jnp.int32, (sc, sc), 0)
    si = lax.broadcasted_iota(jnp.int32, (sc, sc), 1)
    causal = (ti // c == si // c) & (si <= ti)
    la = _gla_gates(a_ref[...], wa_ref, ba_ref)
    b, b_last = _chunk_scan(la, c)
    q = q_ref[...] * (dk ** -0.5)
    k = k_ref[...]
    qe = (q * jnp.exp(b)).astype(BF16)
    ke = (k * jnp.exp(-b)).astype(BF16)
    kd = (k * jnp.exp(b_last - b)).astype(BF16)
    vb = v_ref[...].astype(BF16)
    kc = lambda hh: slice(hh * dk, (hh + 1) * dk)
    vc = lambda hh: slice(hh * dv, (hh + 1) * dv)
    o_intra = []
    for hh in range(hp):
        att = jnp.where(causal, lax.dot_general(qe[:, kc(hh)], ke[:, kc(hh)], _NT, preferred_element_type=F32), 0.0)
        o_intra.append(jnp.dot(att.astype(BF16), vb[:, vc(hh)], preferred_element_type=F32))
    for j in range(sc // c):
        rows = slice(j * c, (j + 1) * c)
        for hh in range(hp):
            st = st_ref[hh]
            o_inter = lax.dot_general(qe[rows, kc(hh)], st.astype(BF16), _NT, preferred_element_type=F32)
            oc_ref[rows, vc(hh)] = o_intra[hh][rows] + o_inter
            dec = jnp.exp(b_last[j * c:j * c + 1, kc(hh)])
            st_ref[hh] = dec * st + lax.dot_general(vb[rows, vc(hh)], kd[rows, kc(hh)], _TN,
                                                    preferred_element_type=F32)
    for hh in range(hp):
        o_ref[:, vc(hh)] = _gla_finish(oc_ref[:, vc(hh)], r_ref[:, vc(hh)], g_ref)

    @pl.when(step == pl.num_programs(2) - 1)
    def _():
        for hh in range(hp):
            s_ref[hh] = st_ref[hh].T


def gla_prompt(z, b, t, w_alpha, b_alpha, norm_g):
    dk, dv, h, hp, sc = GLA_DK, GLA_DV, GLA_HEADS, GLA_HEAD_PAIR, GLA_SUPER
    nt = t // sc
    wa = jnp.pad(w_alpha, ((0, LANE - GLA_RANK), (0, 0))).astype(BF16)
    rows = lambda bi, hi, ti: bi * nt + ti
    return pl.pallas_call(
        _gla_prompt_kernel,
        grid=(b, h // hp, nt),
        in_specs=[
            pl.BlockSpec((sc, hp * dk), lambda bi, hi, ti: (rows(bi, hi, ti), hi)),
            pl.BlockSpec((sc, hp * dk), lambda bi, hi, ti: (rows(bi, hi, ti), h // hp + hi)),
            pl.BlockSpec((sc, hp * dv), lambda bi, hi, ti: (rows(bi, hi, ti), 2 * GLA_QK // (hp * dv) + hi)),
            pl.BlockSpec((sc, hp * dv), lambda bi, hi, ti: (rows(bi, hi, ti), (2 * GLA_QK + GLA_VD) // (hp * dv) + hi)),
            pl.BlockSpec((sc, LANE), lambda bi, hi, ti: (rows(bi, hi, ti), (2 * GLA_QK + 2 * GLA_VD) // LANE)),
            pl.BlockSpec((LANE, hp * dk), lambda bi, hi, ti: (0, hi)),
            pl.BlockSpec((1, hp * dk), lambda bi, hi, ti: (0, hi)),
            pl.BlockSpec((1, dv), lambda bi, hi, ti: (0, 0)),
        ],
        out_specs=[
            pl.BlockSpec((sc, hp * dv), lambda bi, hi, ti: (rows(bi, hi, ti), hi)),
            pl.BlockSpec((None, hp, dk, dv), lambda bi, hi, ti: (bi, hi, 0, 0)),
        ],
        out_shape=[jax.ShapeDtypeStruct((b * t, GLA_VD), BF16), jax.ShapeDtypeStruct((b, h, dk, dv), F32)],
        scratch_shapes=[pltpu.VMEM((hp, dv, dk), F32), pltpu.VMEM((sc, hp * dv), F32)],
        compiler_params=_params("parallel", "parallel", "arbitrary"),
        name="gla_prompt",
    )(z, z, z, z, z, wa, b_alpha.reshape(1, -1), norm_g.reshape(1, -1))


def _gla_sample_kernel(nb, t, q_ref, k_ref, v_ref, r_ref, a_ref, wa_ref, ba_ref, g_ref, s0_ref, o_ref, s_ref):
    n = nb * t
    la = _gla_gates(a_ref[...], wa_ref, ba_ref)
    b, b_last = _chunk_scan(la, t)
    q = q_ref[...] * (GLA_DK ** -0.5)
    k = k_ref[...]
    qe = q * jnp.exp(b)
    ke = (k * jnp.exp(-b)).astype(BF16)
    kd = k * jnp.exp(b_last - b)
    vb = v_ref[...].astype(BF16)
    ti = lax.broadcasted_iota(jnp.int32, (n, n), 0)
    si = lax.broadcasted_iota(jnp.int32, (n, n), 1)
    causal = (ti // t == si // t) & (si <= ti)
    att = jnp.where(causal, lax.dot_general(qe.astype(BF16), ke, _NT, preferred_element_type=F32), 0.0)
    o = jnp.dot(att.astype(BF16), vb, preferred_element_type=F32)
    row = lax.broadcasted_iota(jnp.int32, (n, 1), 0) // t
    for i in range(nb):
        mine = row == i
        st = s0_ref[i].T
        qe_i = jnp.where(mine, qe, 0.0).astype(BF16)
        kd_i = jnp.where(mine, kd, 0.0).astype(BF16)
        o = o + lax.dot_general(qe_i, st.astype(BF16), _NT, preferred_element_type=F32)
        dec = jnp.exp(b_last[i * t:i * t + 1, :])
        s_ref[i] = (dec * st + lax.dot_general(vb, kd_i, _TN, preferred_element_type=F32)).T
    o_ref[...] = _gla_finish(o, r_ref[...], g_ref)


def gla_sample(z, nb, t, s0, w_alpha, b_alpha, norm_g):
    dk, dv, h = GLA_DK, GLA_DV, GLA_HEADS
    n = nb * t
    wa = jnp.pad(w_alpha, ((0, LANE - GLA_RANK), (0, 0))).astype(BF16)
    return pl.pallas_call(
        functools.partial(_gla_sample_kernel, nb, t),
        grid=(h,),
        in_specs=[
            pl.BlockSpec((n, dk), lambda hi: (0, hi)),
            pl.BlockSpec((n, dk), lambda hi: (0, h + hi)),
            pl.BlockSpec((n, dv), lambda hi: (0, 2 * GLA_QK // dv + hi)),
            pl.BlockSpec((n, dv), lambda hi: (0, (2 * GLA_QK + GLA_VD) // dv + hi)),
            pl.BlockSpec((n, LANE), lambda hi: (0, (2 * GLA_QK + 2 * GLA_VD) // LANE)),
            pl.BlockSpec((LANE, dk), lambda hi: (0, hi)),
            pl.BlockSpec((1, dk), lambda hi: (0, hi)),
            pl.BlockSpec((1, dv), lambda hi: (0, 0)),
            pl.BlockSpec((nb, None, dk, dv), lambda hi: (0, hi, 0, 0)),
        ],
        out_specs=[
            pl.BlockSpec((n, dv), lambda hi: (0, hi)),
            pl.BlockSpec((nb, None, dk, dv), lambda hi: (0, hi, 0, 0)),
        ],
        out_shape=[jax.ShapeDtypeStruct((n, GLA_VD), BF16), jax.ShapeDtypeStruct((nb, h, dk, dv), F32)],
        compiler_params=_params("parallel"),
        name="gla_sample",
    )(z, z, z, z, z, wa, b_alpha.reshape(1, -1), norm_g.reshape(1, -1), s0)


def nsa_prompt_mix(z, kv, b, t, gate_bias, w_cmp, pe_cmp):
    ckv = compress(z, b, t // CMP_STRIDE, NSA_QD // NSA_HEAD_DIM, w_cmp, pe_cmp)
    o = nsa_prompt_attention(z, ckv, gate_bias, b, t)
    kv = kv.reshape(3, b, t, 2, NSA_KV_HEADS, NSA_HEAD_DIM)
    return o, kv[0], kv[1], kv[2, :, t - min(WINDOW, t):]


def kernel(x_prompt, x_sample, p_prompt, p_sample, cache_cmp_kv, cache_sel_kv, cache_win_kv, state_gla, page_table, ffn1_norm, ffn1_w_gu, ffn1_w_down, mix_norm, nsa_w_in, nsa_gate_bias, nsa_w_cmp, nsa_pe_cmp, nsa_w_out, gla_w_in, gla_w_alpha, gla_b_alpha, gla_norm, gla_w_out, ffn2_norm, ffn2_w_gu, ffn2_w_down, ple_norm, ple_w_gate, ple_w_proj, final_norm):
    bp, tp, d = x_prompt.shape
    bs, ts, _ = x_sample.shape
    mp, ms = bp * tp, bs * ts
    xp = x_prompt.reshape(mp, d)
    xs = x_sample.reshape(ms, d)
    tm_p = _row_tile(mp, 512)
    tm_p2 = _row_tile(mp, 512)

    cmp_p, sel_p, win_p, gla_p = [], [], [], []
    cmp_s, sel_s, win_s, gla_s = [], [], [], []
    for i in range(DEPTH):
        xs, wa, wu, wd = ffn_cast(xs, ffn1_norm[i], ffn1_w_gu, ffn1_w_down, i)
        xp = ffn(xp, ffn1_norm[i], wa, wu, wd, tm_p)
        j = i // 2
        if i % 2 == 0:
            w_out = nsa_w_out[j]
            zs, w_in = norm_matmul_cast(xs, mix_norm[i], nsa_w_in[j])
            zp, kvp = nsa_in_proj(xp, mix_norm[i], w_in, tm_p2)
            op, kc, kl, kw = nsa_prompt_mix(zp, kvp, bp, tp, nsa_gate_bias[j], nsa_w_cmp[j], nsa_pe_cmp[j])
            os_, kc2, kl2, kw2 = nsa_sample_mix(zs, bs, ts, cache_cmp_kv[j], cache_sel_kv[j], cache_win_kv[j],
                                                page_table, nsa_gate_bias[j], nsa_w_cmp[j], nsa_pe_cmp[j])
            cmp_p.append(kc)
            sel_p.append(kl)
            win_p.append(kw)
            cmp_s.append(kc2)
            sel_s.append(kl2)
            win_s.append(kw2)
        else:
            w_out = gla_w_out[j]
            zs, w_in = norm_matmul_cast(xs, mix_norm[i], gla_w_in[j])
            zp = norm_matmul(xp, mix_norm[i], w_in, tm_p2)
            op, sp = gla_prompt(zp, bp, tp, gla_w_alpha[j], gla_b_alpha[j], gla_norm[j])
            os_, ss = gla_sample(zs, bs, ts, state_gla[j], gla_w_alpha[j], gla_b_alpha[j], gla_norm[j])
            gla_p.append(sp)
            gla_s.append(ss)
        xs, w_out = matmul_residual_cast(os_, w_out, xs)
        xp = matmul_residual(op, w_out, xp, tm_p)
        xs, wa, wu, wd = ffn_cast(xs, ffn2_norm[i], ffn2_w_gu, ffn2_w_down, i)
        xp = ffn(xp, ffn2_norm[i], wa, wu, wd, tm_p)
        xs, w_pg, w_pp = ple_cast(xs, p_sample[i].reshape(ms, -1), ple_norm[i], ple_w_gate, ple_w_proj, i)
        xp = ple(xp, p_prompt[i].reshape(mp, -1), ple_norm[i], w_pg, w_pp, tm_p2)
    y_prompt = rmsnorm(xp, final_norm, tm_p2).reshape(bp, tp, d)
    y_sample = rmsnorm(xs, final_norm, ms).reshape(bs, ts, d)
    return (y_prompt, y_sample, jnp.stack(cmp_p), jnp.stack(sel_p), jnp.stack(win_p), jnp.stack(gla_p),
            jnp.stack(cmp_s), jnp.stack(sel_s), jnp.stack(win_s), jnp.stack(gla_s))
```

```python
import functools
import math

import jax
import jax.numpy as jnp
from jax import lax
from jax.experimental import pallas as pl
from jax.experimental.pallas import tpu as pltpu

F32 = jnp.float32
BF16 = jnp.bfloat16

D_MODEL = 4096
DEPTH = 2
PAST_LEN = 16384
PAGE_SIZE = 128
NSA_HEADS = 32
NSA_HEAD_DIM = 128
NSA_KV_HEADS = 4
NSA_Q_PER_KV = 8
CMP_STRIDE = 16
CMP_LEN = 32
SEL_BLOCK = 64
N_SEL = 16
WINDOW = 512
NSA_QBLOCK = 32
NSA_QD = NSA_HEADS * NSA_HEAD_DIM
NSA_KVD = NSA_KV_HEADS * NSA_HEAD_DIM
GLA_HEADS = 8
GLA_DK = 256
GLA_DV = 512
GLA_RANK = 16
GLA_TAU = 16.0
GLA_CHUNK = 32
D_FF = 11008
EPS = 1e-6
NEG_INF = -1e30
FORCE_SCORE = 1e30

V7X_VMEM_LIMIT_BYTES = 56 * 1024 * 1024
LANE = 128


def _params(*sem):
    return pltpu.CompilerParams(dimension_semantics=sem, vmem_limit_bytes=V7X_VMEM_LIMIT_BYTES)


def _norm_rows(x, g):
    ms = jnp.mean(x * x, axis=-1, keepdims=True)
    return (x * lax.rsqrt(ms + EPS)) * g


def _row_tile(m, want):
    return want if m % want == 0 else m


def _ffn_tile(h, wa_ref, wu_ref, wd_ref):
    a = jnp.dot(h, wa_ref[...], preferred_element_type=F32)
    u = jnp.dot(h, wu_ref[...], preferred_element_type=F32)
    act = (0.5 * (a * jax.nn.sigmoid(a)) * u).astype(BF16)
    return jnp.dot(act, wd_ref[...], preferred_element_type=F32)


def _ffn_kernel(x_ref, g_ref, wa_ref, wu_ref, wd_ref, o_ref, h_ref):
    @pl.when(pl.program_id(1) == 0)
    def _():
        x = x_ref[...]
        h_ref[...] = _norm_rows(x, g_ref[...]).astype(BF16)
        o_ref[...] = x

    o_ref[...] += _ffn_tile(h_ref[...], wa_ref, wu_ref, wd_ref)


def _ffn_pair_kernel(nf, x_ref, g_ref, wa0, wu0, wd0, wa1, wu1, wd1, o_ref, h_ref):
    j = pl.program_id(1)

    @pl.when(j == 0)
    def _():
        x = x_ref[...]
        h_ref[...] = _norm_rows(x, g_ref[...]).astype(BF16)
        o_ref[...] = x

    o_ref[...] += _ffn_tile(h_ref[...], wa0, wu0, wd0)

    @pl.when(2 * j + 1 < nf)
    def _():
        o_ref[...] += _ffn_tile(h_ref[...], wa1, wu1, wd1)


def ffn(x, g, wa, wu, wd, tm, tf=256):
    m, d = x.shape
    nf = D_FF // tf
    second = lambda j: jnp.minimum(2 * j + 1, nf - 1)
    return pl.pallas_call(
        functools.partial(_ffn_pair_kernel, nf),
        grid=(m // tm, -(-nf // 2)),
        in_specs=[
            pl.BlockSpec((tm, d), lambda i, j: (i, 0), pipeline_mode=pl.Buffered(1)),
            pl.BlockSpec((1, d), lambda i, j: (0, 0)),
            pl.BlockSpec((d, tf), lambda i, j: (0, 2 * j)),
            pl.BlockSpec((d, tf), lambda i, j: (0, 2 * j)),
            pl.BlockSpec((tf, d), lambda i, j: (2 * j, 0)),
            pl.BlockSpec((d, tf), lambda i, j: (0, second(j))),
            pl.BlockSpec((d, tf), lambda i, j: (0, second(j))),
            pl.BlockSpec((tf, d), lambda i, j: (second(j), 0)),
        ],
        out_specs=pl.BlockSpec((tm, d), lambda i, j: (i, 0), pipeline_mode=pl.Buffered(1)),
        out_shape=jax.ShapeDtypeStruct((m, d), F32),
        scratch_shapes=[pltpu.VMEM((tm, d), BF16)],
        compiler_params=_params("parallel", "arbitrary"),
        name="ffn",
    )(x, g.reshape(1, d), wa, wu, wd, wa, wu, wd)


def _ffn_cast_kernel(x_ref, g_ref, wa_ref, wu_ref, wd_ref, o_ref, wa_out, wu_out, wd_out, h_ref):
    wa_out[...] = wa_ref[...].astype(BF16)
    wu_out[...] = wu_ref[...].astype(BF16)
    wd_out[...] = wd_ref[...].astype(BF16)
    _ffn_kernel(x_ref, g_ref, wa_out, wu_out, wd_out, o_ref, h_ref)


def ffn_cast(x, g, w_gu, w_down, layer, tf=256):
    m, d = x.shape
    nf = D_FF // tf
    return pl.pallas_call(
        _ffn_cast_kernel,
        grid=(1, nf),
        in_specs=[
            pl.BlockSpec((m, d), lambda i, j: (0, 0)),
            pl.BlockSpec((1, d), lambda i, j: (0, 0)),
            pl.BlockSpec((None, d, tf), lambda i, j: (layer, 0, j)),
            pl.BlockSpec((None, d, tf), lambda i, j: (layer, 0, j + nf)),
            pl.BlockSpec((None, tf, d), lambda i, j: (layer, j, 0)),
        ],
        out_specs=[
            pl.BlockSpec((m, d), lambda i, j: (0, 0)),
            pl.BlockSpec((d, tf), lambda i, j: (0, j)),
            pl.BlockSpec((d, tf), lambda i, j: (0, j)),
            pl.BlockSpec((tf, d), lambda i, j: (j, 0)),
        ],
        out_shape=[jax.ShapeDtypeStruct((m, d), F32), jax.ShapeDtypeStruct((d, D_FF), BF16),
                   jax.ShapeDtypeStruct((d, D_FF), BF16), jax.ShapeDtypeStruct((D_FF, d), BF16)],
        scratch_shapes=[pltpu.VMEM((m, d), BF16)],
        compiler_params=_params("arbitrary", "arbitrary"),
        name="ffn_cast",
    )(x, g.reshape(1, d), w_gu, w_gu, w_down)


def _norm_matmul_kernel(x_ref, g_ref, w_ref, o_ref, h_ref):
    @pl.when(pl.program_id(1) == 0)
    def _():
        h_ref[...] = _norm_rows(x_ref[...], g_ref[...]).astype(BF16)

    o_ref[...] = jnp.dot(h_ref[...], w_ref[...], preferred_element_type=F32)


def norm_matmul(x, g, w, tm, tn=512):
    m, d = x.shape
    n = w.shape[1]
    return pl.pallas_call(
        _norm_matmul_kernel,
        grid=(m // tm, n // tn),
        in_specs=[
            pl.BlockSpec((tm, d), lambda i, j: (i, 0)),
            pl.BlockSpec((1, d), lambda i, j: (0, 0)),
            pl.BlockSpec((d, tn), lambda i, j: (0, j)),
        ],
        out_specs=pl.BlockSpec((tm, tn), lambda i, j: (i, j)),
        out_shape=jax.ShapeDtypeStruct((m, n), F32),
        scratch_shapes=[pltpu.VMEM((tm, d), BF16)],
        compiler_params=_params("parallel", "arbitrary"),
        name="in_proj",
    )(x, g.reshape(1, d), w)


def _nsa_in_proj_kernel(tn, x_ref, g_ref, w_ref, o_ref, kv_ref, h_ref):
    j = pl.program_id(1)

    @pl.when(j == 0)
    def _():
        h_ref[...] = _norm_rows(x_ref[...], g_ref[...]).astype(BF16)

    res = jnp.dot(h_ref[...], w_ref[...], preferred_element_type=F32)
    o_ref[...] = res

    @pl.when((j >= NSA_QD // tn) & (j < (NSA_QD + 6 * NSA_KVD) // tn))
    def _():
        for g in range(NSA_KV_HEADS):
            kv_ref[:, g, :] = res[:, g * NSA_HEAD_DIM:(g + 1) * NSA_HEAD_DIM]


def nsa_in_proj(x, g, w, tm):
    m, d = x.shape
    n = w.shape[1]
    tn = NSA_KVD
    q_tiles = NSA_QD // tn

    def kv_index(i, j):
        t = jnp.clip(j - q_tiles, 0, 5)
        return (t // 2, i, t % 2, 0, 0)

    return pl.pallas_call(
        functools.partial(_nsa_in_proj_kernel, tn),
        grid=(m // tm, n // tn),
        in_specs=[
            pl.BlockSpec((tm, d), lambda i, j: (i, 0)),
            pl.BlockSpec((1, d), lambda i, j: (0, 0)),
            pl.BlockSpec((d, tn), lambda i, j: (0, j)),
        ],
        out_specs=[
            pl.BlockSpec((tm, tn), lambda i, j: (i, j)),
            pl.BlockSpec((None, tm, None, NSA_KV_HEADS, NSA_HEAD_DIM), kv_index),
        ],
        out_shape=[jax.ShapeDtypeStruct((m, n), F32),
                   jax.ShapeDtypeStruct((3, m, 2, NSA_KV_HEADS, NSA_HEAD_DIM), F32)],
        scratch_shapes=[pltpu.VMEM((tm, d), BF16)],
        compiler_params=_params("parallel", "arbitrary"),
        name="nsa_in_proj",
    )(x, g.reshape(1, d), w)


def _matmul_residual_kernel(a_ref, w_ref, x_ref, o_ref):
    o_ref[...] = x_ref[...] + jnp.dot(a_ref[...], w_ref[...], preferred_element_type=F32)


def matmul_residual(a, w, x, tm, tn=512):
    m, k = a.shape
    n = w.shape[1]
    return pl.pallas_call(
        _matmul_residual_kernel,
        grid=(m // tm, n // tn),
        in_specs=[
            pl.BlockSpec((tm, k), lambda i, j: (i, 0)),
            pl.BlockSpec((k, tn), lambda i, j: (0, j)),
            pl.BlockSpec((tm, tn), lambda i, j: (i, j)),
        ],
        out_specs=pl.BlockSpec((tm, tn), lambda i, j: (i, j)),
        out_shape=jax.ShapeDtypeStruct((m, n), F32),
        compiler_params=_params("parallel", "arbitrary"),
        name="out_proj",
    )(a, w, x)


def _matmul_residual_cast_kernel(a_ref, w_ref, x_ref, o_ref, w_out):
    w_out[...] = w_ref[...].astype(BF16)
    _matmul_residual_kernel(a_ref, w_out, x_ref, o_ref)


def matmul_residual_cast(a, w, x, tn=512):
    m, k = a.shape
    n = w.shape[1]
    return pl.pallas_call(
        _matmul_residual_cast_kernel,
        grid=(n // tn,),
        in_specs=[
            pl.BlockSpec((m, k), lambda j: (0, 0)),
            pl.BlockSpec((k, tn), lambda j: (0, j)),
            pl.BlockSpec((m, tn), lambda j: (0, j)),
        ],
        out_specs=[pl.BlockSpec((m, tn), lambda j: (0, j)), pl.BlockSpec((k, tn), lambda j: (0, j))],
        out_shape=[jax.ShapeDtypeStruct((m, n), F32), jax.ShapeDtypeStruct((k, n), BF16)],
        compiler_params=_params("arbitrary"),
        name="out_proj_cast",
    )(a, w, x)


def _ple_kernel(tn, x_ref, g_ref, wg_ref, p_ref, wp_ref, o_ref, h_ref):
    j = pl.program_id(1)

    @pl.when(j == 0)
    def _():
        h_ref[...] = _norm_rows(x_ref[...], g_ref[...]).astype(BF16)

    gate = jax.nn.sigmoid(jnp.dot(h_ref[...], wg_ref[...], preferred_element_type=F32))
    proj = jnp.dot(p_ref[...].astype(BF16), wp_ref[...], preferred_element_type=F32)
    col = pl.multiple_of(j * tn, tn)
    o_ref[...] = x_ref[:, pl.ds(col, tn)] + gate * proj


def ple(x, p, g, w_gate, w_proj, tm, tn=512):
    m, d = x.shape
    pd = p.shape[1]
    return pl.pallas_call(
        functools.partial(_ple_kernel, tn),
        grid=(m // tm, d // tn),
        in_specs=[
            pl.BlockSpec((tm, d), lambda i, j: (i, 0)),
            pl.BlockSpec((1, d), lambda i, j: (0, 0)),
            pl.BlockSpec((d, tn), lambda i, j: (0, j)),
            pl.BlockSpec((tm, pd), lambda i, j: (i, 0)),
            pl.BlockSpec((pd, tn), lambda i, j: (0, j)),
        ],
        out_specs=pl.BlockSpec((tm, tn), lambda i, j: (i, j)),
        out_shape=jax.ShapeDtypeStruct((m, d), F32),
        scratch_shapes=[pltpu.VMEM((tm, d), BF16)],
        compiler_params=_params("parallel", "arbitrary"),
        name="ple",
    )(x, g.reshape(1, d), w_gate, p, w_proj)


def _ple_cast_kernel(tn, x_ref, g_ref, wg_ref, p_ref, wp_ref, o_ref, wg_out, wp_out, h_ref):
    wg_out[...] = wg_ref[...].astype(BF16)
    wp_out[...] = wp_ref[...].astype(BF16)
    _ple_kernel(tn, x_ref, g_ref, wg_out, p_ref, wp_out, o_ref, h_ref)


def ple_cast(x, p, g, w_gate, w_proj, layer, tn=512):
    m, d = x.shape
    pd = p.shape[1]
    return pl.pallas_call(
        functools.partial(_ple_cast_kernel, tn),
        grid=(1, d // tn),
        in_specs=[
            pl.BlockSpec((m, d), lambda i, j: (0, 0)),
            pl.BlockSpec((1, d), lambda i, j: (0, 0)),
            pl.BlockSpec((None, d, tn), lambda i, j: (layer, 0, j)),
            pl.BlockSpec((m, pd), lambda i, j: (0, 0)),
            pl.BlockSpec((None, pd, tn), lambda i, j: (layer, 0, j)),
        ],
        out_specs=[
            pl.BlockSpec((m, tn), lambda i, j: (0, j)),
            pl.BlockSpec((d, tn), lambda i, j: (0, j)),
            pl.BlockSpec((pd, tn), lambda i, j: (0, j)),
        ],
        out_shape=[jax.ShapeDtypeStruct((m, d), F32), jax.ShapeDtypeStruct((d, d), BF16),
                   jax.ShapeDtypeStruct((pd, d), BF16)],
        scratch_shapes=[pltpu.VMEM((m, d), BF16)],
        compiler_params=_params("arbitrary", "arbitrary"),
        name="ple_cast",
    )(x, g.reshape(1, d), w_gate, p, w_proj)


def _rmsnorm_kernel(x_ref, g_ref, o_ref):
    o_ref[...] = _norm_rows(x_ref[...], g_ref[...])


def rmsnorm(x, g, tm):
    m, d = x.shape
    return pl.pallas_call(
        _rmsnorm_kernel,
        grid=(m // tm,),
        in_specs=[pl.BlockSpec((tm, d), lambda i: (i, 0)), pl.BlockSpec((1, d), lambda i: (0, 0))],
        out_specs=pl.BlockSpec((tm, d), lambda i: (i, 0)),
        out_shape=jax.ShapeDtypeStruct((m, d), F32),
        compiler_params=_params("parallel"),
    )(x, g.reshape(1, d))


Z_GATE_COL = (NSA_QD + 6 * NSA_KVD) // LANE


def _compress_kernel(nc, x_ref, w2_ref, wf_ref, pe_ref, o_ref):
    acc = jnp.zeros((nc, 2 * NSA_HEAD_DIM), F32)
    for l in range(CMP_STRIDE):
        a = x_ref[pl.ds(l, nc, stride=CMP_STRIDE), :]
        acc += jnp.dot(a.astype(BF16), w2_ref[l], preferred_element_type=F32)
    n_idx = lax.broadcasted_iota(jnp.int32, (nc, NSA_HEAD_DIM), 0)
    first = acc[:, :NSA_HEAD_DIM]
    second = jnp.where(n_idx == nc - 1, 0.0, pltpu.roll(acc[:, NSA_HEAD_DIM:], nc - 1, 0))
    bias = jnp.dot(pe_ref[...], wf_ref[...], preferred_element_type=F32)[0:1]
    o_ref[...] = (first + second + bias).astype(BF16)


def compress(x, nb, nc, col0, w_cmp, pe_cmp):
    hd = NSA_HEAD_DIM
    w2 = w_cmp.reshape(2, 2, CMP_STRIDE, hd, hd).transpose(0, 2, 3, 1, 4).reshape(2, CMP_STRIDE, hd, 2 * hd)
    wf = w_cmp.reshape(2, CMP_LEN * hd, hd)
    pe = jnp.broadcast_to(pe_cmp.reshape(2, 1, CMP_LEN * hd), (2, 8, CMP_LEN * hd))
    return pl.pallas_call(
        functools.partial(_compress_kernel, nc),
        grid=(nb, 2, NSA_KV_HEADS),
        in_specs=[
            pl.BlockSpec((CMP_STRIDE * nc, hd), lambda i, c, g: (i, col0 + c * NSA_KV_HEADS + g)),
            pl.BlockSpec((None, CMP_STRIDE, hd, 2 * hd), lambda i, c, g: (c, 0, 0, 0)),
            pl.BlockSpec((None, CMP_LEN * hd, hd), lambda i, c, g: (c, 0, 0)),
            pl.BlockSpec((None, 8, CMP_LEN * hd), lambda i, c, g: (c, 0, 0)),
        ],
        out_specs=pl.BlockSpec((None, None, None, nc, hd), lambda i, c, g: (i, c, g, 0, 0)),
        out_shape=jax.ShapeDtypeStruct((nb, 2, NSA_KV_HEADS, nc, hd), BF16),
        compiler_params=_params("parallel", "parallel", "parallel"),
        name="nsa_compress",
    )(x, w2.astype(BF16), wf.astype(BF16), pe.astype(BF16))


def _nsa_prompt_kernel(t, qb, kb, nc, slope_ref, zq_ref, ks_ref, vs_ref, kw_ref, vw_ref, gz_ref, gb_ref,
                       ck_ref, cv_ref, ov_ref, o_ref, m_ref, l_ref, acc_ref, p_ref):
    g = pl.program_id(1)
    qi = pl.program_id(2)
    t0 = qi * qb
    r8 = NSA_Q_PER_KV
    hd = NSA_HEAD_DIM
    n_slc = t // SEL_BLOCK
    n_cmp = nc - 1

    zq = zq_ref[...] * (hd ** -0.5)
    q2 = jnp.concatenate([zq[:, r * hd:(r + 1) * hd] for r in range(r8)], axis=0).astype(BF16)

    slope = [slope_ref[g * r8 + r] for r in range(r8)]
    tq = lax.broadcasted_iota(jnp.int32, (qb, 1), 0) + t0
    rows = lambda r: slice(r * qb, (r + 1) * qb)

    n_i = lax.broadcasted_iota(jnp.int32, (qb, nc), 1)
    dist_ci = tq - (CMP_STRIDE * n_i + (CMP_LEN - 1))
    vis_c = (dist_ci >= 0) & (n_i < n_cmp)
    dist_c = dist_ci.astype(F32)
    s_all = lax.dot_general(q2, ck_ref[...], (((1,), (1,)), ((), ())), preferred_element_type=F32)
    psum = jnp.zeros((qb, nc), F32)
    for r in range(r8):
        s = jnp.where(vis_c, s_all[rows(r)] - slope[r] * dist_c, NEG_INF)
        e = jnp.exp(s - jnp.max(s, axis=-1, keepdims=True))
        p = jnp.where(vis_c, e / jnp.sum(e, axis=-1, keepdims=True), 0.0)
        psum = psum + p
        p_ref[rows(r), :nc] = p.astype(BF16)
    o_cmp = jnp.dot(p_ref[:, :nc], cv_ref[...], preferred_element_type=F32)

    p_hi = psum.astype(BF16)
    p_mid = (psum - p_hi.astype(F32)).astype(BF16)
    p_lo = (psum - p_hi.astype(F32) - p_mid.astype(F32)).astype(BF16)
    ov = ov_ref[...]
    imp = (jnp.dot(p_hi, ov, preferred_element_type=F32) + jnp.dot(p_mid, ov, preferred_element_type=F32)
           + jnp.dot(p_lo, ov, preferred_element_type=F32))
    nj = -(-n_slc // 8) * 8
    j_t = lax.broadcasted_iota(jnp.int32, (nj, qb), 0)
    cur = (t0 + lax.broadcasted_iota(jnp.int32, (nj, qb), 1)) // SEL_BLOCK
    vis_j = j_t <= cur
    forced = (j_t == 0) | (j_t == cur) | (j_t == cur - 1)
    score = jnp.where(forced, FORCE_SCORE, jnp.where(vis_j, imp.T[:nj], -FORCE_SCORE))
    score = jnp.where(j_t < n_slc, score, -jnp.inf)
    rank = jnp.zeros((nj, qb), jnp.int32)
    for jp in range(n_slc):
        row = score[jp:jp + 1, :]
        ahead = (row > score) | ((row == score) & (jp < j_t))
        rank = rank + ahead.astype(jnp.int32)
    selected = jnp.where((rank < min(N_SEL, n_slc)) & vis_j, 1.0, 0.0).astype(BF16)

    d0 = (lax.broadcasted_iota(jnp.int32, (qb, kb), 0) - lax.broadcasted_iota(jnp.int32, (qb, kb), 1))

    def flash_init():
        m_ref[...] = jnp.full(m_ref.shape, NEG_INF, F32)
        l_ref[...] = jnp.zeros(l_ref.shape, F32)
        acc_ref[...] = jnp.zeros(acc_ref.shape, F32)

    def flash_step(k_ref, v_ref, k0, mask, dist):
        k = k_ref[pl.ds(k0, kb), :].astype(BF16)
        v = v_ref[pl.ds(k0, kb), :].astype(BF16)
        s_all = lax.dot_general(q2, k, (((1,), (1,)), ((), ())), preferred_element_type=F32)
        alphas = []
        for r in range(r8):
            s = jnp.where(mask, s_all[rows(r)] - slope[r] * dist, NEG_INF)
            m_prev = m_ref[r]
            m_next = jnp.maximum(m_prev, jnp.max(s, axis=-1, keepdims=True))
            alpha = jnp.exp(m_prev - m_next)
            p = jnp.exp(s - jnp.concatenate([m_next] * (kb // LANE), axis=1))
            l_ref[r] = alpha * l_ref[r] + jnp.sum(p, axis=-1, keepdims=True)
            m_ref[r] = m_next
            p_ref[rows(r), :kb] = p.astype(BF16)
            alphas.append(alpha)
        pv = jnp.dot(p_ref[:, :kb], v, preferred_element_type=F32)
        for r in range(r8):
            acc_ref[r] = alphas[r] * acc_ref[r] + pv[rows(r)]

    flash_init()

    def sel_body(i, carry):
        k0 = pl.multiple_of(i * kb, kb)
        dist = d0 + (t0 - k0)
        blk_of_key = (lax.broadcasted_iota(jnp.int32, (nj, kb), 1) + k0) // SEL_BLOCK
        expand = jnp.where(blk_of_key == lax.broadcasted_iota(jnp.int32, (nj, kb), 0), 1.0, 0.0).astype(BF16)
        chosen = lax.dot_general(selected, expand, _TN, preferred_element_type=F32)
        flash_step(ks_ref, vs_ref, k0, (chosen > 0.5) & (dist >= 0), dist.astype(F32))
        return carry

    lax.fori_loop(0, (t0 + qb - 1) // kb + 1, sel_body, 0)
    o_sel = [acc_ref[r] / l_ref[r] for r in range(r8)]

    wk = WINDOW + qb
    kw0 = pl.multiple_of(jnp.maximum(t0 - WINDOW, 0), qb)
    k_w = kw_ref[pl.ds(kw0, wk), :].astype(BF16)
    v_w = vw_ref[pl.ds(kw0, wk), :].astype(BF16)
    dist_w = (lax.broadcasted_iota(jnp.int32, (qb, wk), 0) - lax.broadcasted_iota(jnp.int32, (qb, wk), 1)) + (t0 - kw0)
    mask_w = (dist_w >= 0) & (dist_w <= WINDOW)
    dist_wf = dist_w.astype(F32)
    s_all = lax.dot_general(q2, k_w, _NT, preferred_element_type=F32)
    l_win = []
    for r in range(r8):
        s = jnp.where(mask_w, s_all[rows(r)] - slope[r] * dist_wf, NEG_INF)
        p = jnp.exp(s - jnp.max(s, axis=-1, keepdims=True))
        l_win.append(jnp.sum(p, axis=-1, keepdims=True))
        p_ref[rows(r), :wk] = p.astype(BF16)
    pv = jnp.dot(p_ref[:, :wk], v_w, preferred_element_type=F32)
    o_win = [pv[rows(r)] / l_win[r] for r in range(r8)]

    gates = jax.nn.sigmoid(gz_ref[...] + gb_ref[...])
    src = lax.broadcasted_iota(jnp.int32, (LANE, LANE), 0)
    dst = lax.broadcasted_iota(jnp.int32, (LANE, LANE), 1)
    pick = jnp.where((dst < 3 * r8) & (src == (dst // r8) * NSA_HEADS + g * r8 + dst % r8), 1.0, 0.0).astype(BF16)
    g_hi = gates.astype(BF16)
    g_mid = (gates - g_hi.astype(F32)).astype(BF16)
    g_lo = (gates - g_hi.astype(F32) - g_mid.astype(F32)).astype(BF16)
    gsel = (jnp.dot(g_hi, pick, preferred_element_type=F32) + jnp.dot(g_mid, pick, preferred_element_type=F32)
            + jnp.dot(g_lo, pick, preferred_element_type=F32))

    def gate(branch, r):
        k = branch * r8 + r
        return gsel[:, k:k + 1]

    for r in range(r8):
        o = gate(0, r) * o_cmp[rows(r)] + gate(1, r) * o_sel[r] + gate(2, r) * o_win[r]
        o_ref[:, r * hd:(r + 1) * hd] = o.astype(BF16)


def nsa_prompt_attention(z, ckv, gate_bias, b, t, qb=128, kb=512):
    assert t % kb == 0 and t >= WINDOW + qb and WINDOW % qb == 0
    nc = t // CMP_STRIDE
    nq = t // qb
    hd = NSA_HEAD_DIM
    n_slc = t // SEL_BLOCK
    slopes = jnp.exp2(-8.0 * (jnp.arange(NSA_HEADS, dtype=F32) + 1.0) / NSA_HEADS)
    ci = jnp.arange(nc)[:, None]
    sj = jnp.arange(LANE)[None, :]
    ov = ((CMP_STRIDE * ci < SEL_BLOCK * (sj + 1)) & (CMP_STRIDE * ci + CMP_LEN > SEL_BLOCK * sj)
          & (sj < n_slc)).astype(BF16)
    gb = jnp.pad(gate_bias, (0, LANE - gate_bias.shape[0])).reshape(1, LANE)
    kv_spec = lambda off: pl.BlockSpec((t, hd), lambda bi, g, qi: (bi, off + g))
    kv_col = (NSA_QD + 2 * NSA_KVD) // hd
    return pl.pallas_call(
        functools.partial(_nsa_prompt_kernel, t, qb, kb, nc),
        grid=(b, NSA_KV_HEADS, nq),
        in_specs=[
            pl.BlockSpec(memory_space=pltpu.SMEM),
            pl.BlockSpec((qb, NSA_Q_PER_KV * hd), lambda bi, g, qi: (bi * nq + qi, g)),
            kv_spec(kv_col), kv_spec(kv_col + 4), kv_spec(kv_col + 8), kv_spec(kv_col + 12),
            pl.BlockSpec((qb, LANE), lambda bi, g, qi: (bi * nq + qi, Z_GATE_COL)),
            pl.BlockSpec((1, LANE), lambda bi, g, qi: (0, 0)),
            pl.BlockSpec((None, None, None, nc, hd), lambda bi, g, qi: (bi, 0, g, 0, 0)),
            pl.BlockSpec((None, None, None, nc, hd), lambda bi, g, qi: (bi, 1, g, 0, 0)),
            pl.BlockSpec((nc, LANE), lambda bi, g, qi: (0, 0)),
        ],
        out_specs=pl.BlockSpec((qb, NSA_Q_PER_KV * hd), lambda bi, g, qi: (bi * nq + qi, g)),
        out_shape=jax.ShapeDtypeStruct((b * t, NSA_QD), BF16),
        scratch_shapes=[pltpu.VMEM((NSA_Q_PER_KV, qb, LANE), F32), pltpu.VMEM((NSA_Q_PER_KV, qb, LANE), F32),
                        pltpu.VMEM((NSA_Q_PER_KV, qb, hd), F32),
                        pltpu.VMEM((NSA_Q_PER_KV * qb, max(kb, nc, WINDOW + qb)), BF16)],
        compiler_params=_params("parallel", "parallel", "arbitrary"),
        name="nsa_prompt_attn",
    )(slopes, z, z, z, z, z, z, gb, ckv, ckv, ov)


_NT = (((1,), (1,)), ((), ()))
_TN = (((0,), (0,)), ((), ()))


GATHER_PAGES = 4


def _page_gather_kernel(n_steps, pt_ref, *refs):
    page_refs, tail_ref, o_ref = refs[:GATHER_PAGES], refs[GATHER_PAGES], refs[GATHER_PAGES + 1]
    p = pl.program_id(1)

    @pl.when(p < n_steps)
    def _():
        for k, x_ref in enumerate(page_refs):
            for c in range(2):
                for g in range(NSA_KV_HEADS):
                    col = (c * NSA_KV_HEADS + g) * NSA_HEAD_DIM
                    o_ref[k * PAGE_SIZE:(k + 1) * PAGE_SIZE, col:col + NSA_HEAD_DIM] = x_ref[:, c, g, :]

    @pl.when(p == n_steps)
    def _():
        o_ref[...] = tail_ref[...]


def page_gather(cache, page_table, tail):
    nb, n_pages = page_table.shape
    step_rows = GATHER_PAGES * PAGE_SIZE
    n_steps = n_pages // GATHER_PAGES
    w = tail.shape[2]

    def page_spec(k):
        def index(b, p, pt):
            return (pt[b, jnp.minimum(p, n_steps - 1) * GATHER_PAGES + k], 0, 0, 0, 0)
        return pl.BlockSpec((None, PAGE_SIZE, 2, NSA_KV_HEADS, NSA_HEAD_DIM), index)

    return pl.pallas_call(
        functools.partial(_page_gather_kernel, n_steps),
        grid_spec=pltpu.PrefetchScalarGridSpec(
            num_scalar_prefetch=1,
            grid=(nb, n_steps + 1),
            in_specs=[page_spec(k) for k in range(GATHER_PAGES)]
            + [pl.BlockSpec((None, step_rows, w), lambda b, p, pt: (b, 0, 0))],
            out_specs=pl.BlockSpec((step_rows, w), lambda b, p, pt: (b * (n_steps + 1) + p, 0)),
        ),
        out_shape=jax.ShapeDtypeStruct((nb * (n_steps + 1) * step_rows, w), F32),
        compiler_params=_params("parallel", "arbitrary"),
        name="nsa_page_gather",
    )(page_table, *([cache] * GATHER_PAGES), tail)


def _row_slopes(slope_ref, g, n_rows, t):
    r_idx = lax.broadcasted_iota(jnp.int32, (n_rows, 1), 0) // t
    col = jnp.zeros((n_rows, 1), F32)
    for r in range(NSA_Q_PER_KV):
        col = jnp.where(r_idx == r, slope_ref[g * NSA_Q_PER_KV + r], col)
    return col


def _nsa_sample_select_kernel(t, pos0, nc, n_cmp, n_slc, slope_ref, q_ref, ck_ref, cv_ref, ov_ref,
                              ocmp_ref, idx_ref, valid_ref):
    g = pl.program_id(0)
    r8 = NSA_Q_PER_KV
    n_rows = r8 * t
    nsp = ov_ref.shape[1]
    q2 = (q_ref[...] * (NSA_HEAD_DIM ** -0.5)).astype(BF16)
    slope = _row_slopes(slope_ref, g, n_rows, t)
    tq = pos0 + lax.broadcasted_iota(jnp.int32, (n_rows, 1), 0) % t
    n_i = lax.broadcasted_iota(jnp.int32, (n_rows, nc), 1)
    dist_ci = tq - (CMP_STRIDE * n_i + (CMP_LEN - 1))
    vis_c = (dist_ci >= 0) & (n_i < n_cmp)
    s = lax.dot_general(q2, ck_ref[...], _NT, preferred_element_type=F32)
    s = jnp.where(vis_c, s - slope * dist_ci.astype(F32), NEG_INF)
    e = jnp.exp(s - jnp.max(s, axis=-1, keepdims=True))
    p = jnp.where(vis_c, e / jnp.sum(e, axis=-1, keepdims=True), 0.0)
    ocmp_ref[...] = jnp.dot(p.astype(BF16), cv_ref[...], preferred_element_type=F32)

    p_hi = p.astype(BF16)
    p_mid = (p - p_hi.astype(F32)).astype(BF16)
    p_lo = (p - p_hi.astype(F32) - p_mid.astype(F32)).astype(BF16)
    ov = ov_ref[...]
    imp = (jnp.dot(p_hi, ov, preferred_element_type=F32) + jnp.dot(p_mid, ov, preferred_element_type=F32)
           + jnp.dot(p_lo, ov, preferred_element_type=F32))
    sh = t
    while sh < n_rows:
        imp = imp + pltpu.roll(imp, sh, 0)
        sh *= 2
    imp = imp[0:8]
    j_i = lax.broadcasted_iota(jnp.int32, (8, nsp), 1)
    cur = (pos0 + lax.broadcasted_iota(jnp.int32, (8, 1), 0) % t) // SEL_BLOCK
    vis_j = j_i <= cur
    forced = (j_i == 0) | (j_i == cur) | (j_i == cur - 1)
    score = jnp.where(forced, FORCE_SCORE, jnp.where(vis_j, imp, -FORCE_SCORE))
    score = jnp.where(j_i < n_slc, score, -jnp.inf)
    rank = jnp.zeros((8, nsp), jnp.int32)
    for jp in range(n_slc):
        col = score[:, jp:jp + 1]
        ahead = (col > score) | ((col == score) & (jp < j_i))
        rank = rank + ahead.astype(jnp.int32)
    lane = lax.broadcasted_iota(jnp.int32, (8, LANE), 1)
    idx = jnp.zeros((8, LANE), F32)
    valid = jnp.zeros((8, LANE), F32)
    j_f = j_i.astype(F32)
    for k in range(min(N_SEL, n_slc)):
        hit = (rank == k) & (j_i < n_slc)
        idx_k = jnp.sum(jnp.where(hit, j_f, 0.0), axis=-1, keepdims=True)
        valid_k = jnp.sum(jnp.where(hit & vis_j, 1.0, 0.0), axis=-1, keepdims=True)
        idx = jnp.where(lane == k, idx_k, idx)
        valid = jnp.where(lane == k, valid_k, valid)
    idx_ref[...] = idx.astype(jnp.int32)
    valid_ref[...] = valid.astype(jnp.int32)


def nsa_sample_select(q, ckv, nb, t, pos0, n_cmp, n_slc):
    g4, hd = NSA_KV_HEADS, NSA_HEAD_DIM
    nc = ckv.shape[3]
    n_rows = NSA_Q_PER_KV * t
    nsp = -(-n_slc // LANE) * LANE
    slopes = jnp.exp2(-8.0 * (jnp.arange(NSA_HEADS, dtype=F32) + 1.0) / NSA_HEADS)
    ci = jnp.arange(nc)[:, None]
    sj = jnp.arange(nsp)[None, :]
    ov = ((CMP_STRIDE * ci < SEL_BLOCK * (sj + 1)) & (CMP_STRIDE * ci + CMP_LEN > SEL_BLOCK * sj)
          & (sj < n_slc) & (ci < n_cmp)).astype(BF16)
    return pl.pallas_call(
        functools.partial(_nsa_sample_select_kernel, t, pos0, nc, n_cmp, n_slc),
        grid=(g4, nb),
        in_specs=[
            pl.BlockSpec(memory_space=pltpu.SMEM),
            pl.BlockSpec((None, None, n_rows, hd), lambda g, b: (g, b, 0, 0)),
            pl.BlockSpec((None, None, None, nc, hd), lambda g, b: (b, 0, g, 0, 0)),
            pl.BlockSpec((None, None, None, nc, hd), lambda g, b: (b, 1, g, 0, 0)),
            pl.BlockSpec((nc, nsp), lambda g, b: (0, 0)),
        ],
        out_specs=[
            pl.BlockSpec((None, None, n_rows, hd), lambda g, b: (g, b, 0, 0)),
            pl.BlockSpec((None, None, 8, LANE), lambda g, b: (g, b, 0, 0)),
            pl.BlockSpec((None, None, 8, LANE), lambda g, b: (g, b, 0, 0)),
        ],
        out_shape=[jax.ShapeDtypeStruct((g4, nb, n_rows, hd), F32),
                   jax.ShapeDtypeStruct((g4, nb, 8, LANE), jnp.int32),
                   jax.ShapeDtypeStruct((g4, nb, 8, LANE), jnp.int32)],
        compiler_params=_params("parallel", "parallel"),
        name="nsa_sample_select",
    )(slopes, q, ckv, ckv, ov)


def _nsa_sample_attn_kernel(nb, t, pos0, n_pages, wb, idx_ref, valid_ref, pt_ref, slope_ref, q_ref, ocmp_ref,
                            gl_ref, gb_ref, kwin_ref, vwin_ref, kwt_ref, vwt_ref, cache_ref, tail_ref, o_ref,
                            kbuf, vbuf, q2_ref, osel_ref, sem):
    b = pl.program_id(0)
    g = pl.program_id(1)
    r8 = NSA_Q_PER_KV
    hd = NSA_HEAD_DIM
    n_rows = r8 * t
    n_sel = kbuf.shape[1]
    past_blocks = n_pages * (PAGE_SIZE // SEL_BLOCK)
    per_page = PAGE_SIZE // SEL_BLOCK

    def sel_entry(tt, k):
        return ((g * nb + b) * t + tt) * n_sel + k

    def block_copies(tt, k, in_tail):
        i = idx_ref[sel_entry(tt, k)]
        out = []
        for c, buf in ((0, kbuf), (1, vbuf)):
            col = c * NSA_KV_HEADS + g
            if in_tail:
                src = tail_ref.at[b, :, pl.ds(col * hd, hd)]
            else:
                ic = jnp.minimum(i, past_blocks - 1)
                page = pt_ref[b * n_pages + ic // per_page]
                src = cache_ref.at[page, pl.ds((ic % per_page) * SEL_BLOCK, SEL_BLOCK), c, g, :]
            out.append(pltpu.make_async_copy(src, buf.at[tt, k], sem))
        return out

    def for_each_block(fn):
        for tt in range(t):
            for k in range(n_sel):
                in_tail = idx_ref[sel_entry(tt, k)] >= past_blocks

                @pl.when(in_tail)
                def _():
                    for cp in block_copies(tt, k, True):
                        fn(cp)

                @pl.when(jnp.logical_not(in_tail))
                def _():
                    for cp in block_copies(tt, k, False):
                        fn(cp)

    for_each_block(lambda cp: cp.start())

    q2_ref[...] = q_ref[...] * (hd ** -0.5)
    q2 = q2_ref[...].astype(BF16)
    slope = _row_slopes(slope_ref, g, n_rows, t)
    tq = pos0 + lax.broadcasted_iota(jnp.int32, (n_rows, 1), 0) % t

    kw = jnp.concatenate([kwin_ref[...], kwt_ref[...]], axis=0).astype(BF16)
    vw = jnp.concatenate([vwin_ref[...], vwt_ref[...]], axis=0).astype(BF16)
    nk = kw.shape[0]
    s_pos = (pos0 - wb) + lax.broadcasted_iota(jnp.int32, (n_rows, nk), 1)
    dist = tq - s_pos
    mask = (dist >= 0) & (dist <= WINDOW) & (s_pos >= 0)
    s = lax.dot_general(q2, kw, _NT, preferred_element_type=F32)
    s = jnp.where(mask, s - slope * dist.astype(F32), NEG_INF)
    e = jnp.exp(s - jnp.max(s, axis=-1, keepdims=True))
    p = e / jnp.sum(e, axis=-1, keepdims=True)
    o_win = jnp.dot(p.astype(BF16), vw, preferred_element_type=F32)

    for_each_block(lambda cp: cp.wait())

    nkeys = n_sel * SEL_BLOCK
    key_lane = lax.broadcasted_iota(jnp.int32, (1, nkeys), 1)
    slope8 = _row_slopes(slope_ref, g, r8, 1)
    for tt in range(t):
        blk = jnp.zeros((1, nkeys), jnp.int32)
        ok = jnp.zeros((1, nkeys), jnp.int32)
        for k in range(n_sel):
            here = key_lane // SEL_BLOCK == k
            blk = jnp.where(here, idx_ref[sel_entry(tt, k)], blk)
            ok = jnp.where(here, valid_ref[sel_entry(tt, k)], ok)
        dist = (pos0 + tt) - (blk * SEL_BLOCK + key_lane % SEL_BLOCK)
        mask = (dist >= 0) & (ok > 0)
        q_t = q2_ref[pl.ds(tt, r8, stride=t), :].astype(BF16)
        k_t = kbuf[tt].reshape(nkeys, hd).astype(BF16)
        v_t = vbuf[tt].reshape(nkeys, hd).astype(BF16)
        s = lax.dot_general(q_t, k_t, _NT, preferred_element_type=F32)
        s = jnp.where(mask, s - slope8 * dist.astype(F32), NEG_INF)
        e = jnp.exp(s - jnp.max(s, axis=-1, keepdims=True))
        p = e / jnp.sum(e, axis=-1, keepdims=True)
        osel_ref[pl.ds(tt, r8, stride=t), :] = jnp.dot(p.astype(BF16), v_t, preferred_element_type=F32)

    gates = jax.nn.sigmoid(gl_ref[...] + gb_ref[...])
    o = gates[:, 0:1] * ocmp_ref[...] + gates[:, 1:2] * osel_ref[...] + gates[:, 2:3] * o_win
    o_ref[...] = o.astype(BF16)


def nsa_sample_attention(q, o_cmp, gl, gb, idx, valid, page_table, cache_sel, tail_sel, cache_win, tail_win,
                         nb, t, pos0):
    g4, hd = NSA_KV_HEADS, NSA_HEAD_DIM
    n_rows = NSA_Q_PER_KV * t
    n_pages = page_table.shape[1]
    n_sel = idx.shape[-1]
    wb = cache_win.shape[1]
    slopes = jnp.exp2(-8.0 * (jnp.arange(NSA_HEADS, dtype=F32) + 1.0) / NSA_HEADS)
    row_spec = lambda w: pl.BlockSpec((None, None, n_rows, w), lambda b, g, *_: (g, b, 0, 0))
    win_spec = lambda rows, c: pl.BlockSpec((None, rows, hd), lambda b, g, *_: (b, 0, c * g4 + g))
    return pl.pallas_call(
        functools.partial(_nsa_sample_attn_kernel, nb, t, pos0, n_pages, wb),
        grid_spec=pltpu.PrefetchScalarGridSpec(
            num_scalar_prefetch=3,
            grid=(nb, g4),
            in_specs=[
                pl.BlockSpec(memory_space=pltpu.SMEM),
                row_spec(hd), row_spec(hd), row_spec(3),
                pl.BlockSpec((None, n_rows, 3), lambda b, g, *_: (g, 0, 0)),
                win_spec(wb, 0), win_spec(wb, 1), win_spec(LANE, 0), win_spec(LANE, 1),
                pl.BlockSpec(memory_space=pl.ANY), pl.BlockSpec(memory_space=pl.ANY),
            ],
            out_specs=row_spec(hd),
            scratch_shapes=[pltpu.VMEM((t, n_sel, SEL_BLOCK, hd), F32), pltpu.VMEM((t, n_sel, SEL_BLOCK, hd), F32),
                            pltpu.VMEM((n_rows, hd), F32), pltpu.VMEM((n_rows, hd), F32),
                            pltpu.SemaphoreType.DMA(())],
        ),
        out_shape=jax.ShapeDtypeStruct((g4, nb, n_rows, hd), BF16),
        compiler_params=_params("arbitrary", "arbitrary"),
        name="nsa_sample_attn",
    )(idx.reshape(-1), valid.reshape(-1), page_table.reshape(-1), slopes, q, o_cmp, gl, gb,
      cache_win, cache_win, tail_win, tail_win, cache_sel, tail_sel)


def nsa_sample_mix(z, nb, t, cache_cmp, cache_sel, cache_win, page_table, gate_bias, w_cmp, pe_cmp):
    g4, r8, hd = NSA_KV_HEADS, NSA_Q_PER_KV, NSA_HEAD_DIM
    n_pages = page_table.shape[1]
    past = n_pages * PAGE_SIZE
    wb = cache_win.shape[1]
    t_all = past + t
    tp = -(-t_all // SEL_BLOCK) * SEL_BLOCK
    n_slc = tp // SEL_BLOCK
    n_cmp = tp // CMP_STRIDE - 1
    gather_rows = (n_pages + GATHER_PAGES) * PAGE_SIZE
    nc = gather_rows // CMP_STRIDE
    kv = z[:, NSA_QD:NSA_QD + 6 * NSA_KVD].reshape(nb, t, 3, 2 * NSA_KVD)

    tail_cmp = jnp.pad(kv[:, :, 0], ((0, 0), (0, gather_rows - past - t), (0, 0)))
    rows = page_gather(cache_cmp, page_table, tail_cmp)
    ckv = compress(rows, nb, nc, 0, w_cmp, pe_cmp)

    q = z[:, :NSA_QD].reshape(nb, t, g4, r8, hd).transpose(2, 0, 3, 1, 4).reshape(g4, nb, r8 * t, hd)
    o_cmp, idx, valid = nsa_sample_select(q, ckv, nb, t, past, n_cmp, n_slc)
    n_sel = min(N_SEL, n_slc)
    idx = idx[:, :, :t, :n_sel]
    valid = valid[:, :, :t, :n_sel]

    gl = z[:, NSA_QD + 6 * NSA_KVD:NSA_QD + 6 * NSA_KVD + 3 * NSA_HEADS].reshape(nb, t, 3, g4, r8)
    gl = gl.transpose(3, 0, 4, 1, 2).reshape(g4, nb, r8 * t, 3)
    gb = jnp.broadcast_to(gate_bias.reshape(3, g4, r8, 1), (3, g4, r8, t)).transpose(1, 2, 3, 0).reshape(g4, r8 * t, 3)
    tail_sel = jnp.pad(kv[:, :, 1], ((0, 0), (0, SEL_BLOCK - t), (0, 0)))
    tail_win = jnp.pad(kv[:, :, 2], ((0, 0), (0, LANE - t), (0, 0)))
    o = nsa_sample_attention(q, o_cmp, gl, gb, idx, valid, page_table,
                             cache_sel, tail_sel,
                             cache_win.reshape(nb, wb, 2 * NSA_KVD), tail_win, nb, t, past)
    o = o.reshape(g4, nb, r8, t, hd).transpose(1, 3, 0, 2, 4).reshape(nb * t, NSA_QD)
    kv6 = kv.reshape(nb, t, 3, 2, g4, hd)
    win_new = jnp.concatenate([cache_win, kv6[:, :, 2]], axis=1)[:, t:]
    return o, kv6[:, :, 0], kv6[:, :, 1], win_new


GLA_QK = GLA_HEADS * GLA_DK
GLA_VD = GLA_HEADS * GLA_DV
GLA_SUPER = 256
GLA_HEAD_PAIR = 2


def _chunk_scan(la, c):
    n = la.shape[0]
    ri = lax.broadcasted_iota(jnp.int32, la.shape, 0) % c
    b = la
    s = 1
    while s < c:
        b = b + jnp.where(ri >= s, pltpu.roll(b, s, 0), 0.0)
        s *= 2
    tot = jnp.where(ri == c - 1, b, 0.0)
    s = 1
    while s < c:
        tot = tot + jnp.where(ri + s < c, pltpu.roll(tot, n - s, 0), 0.0)
        s *= 2
    return b, tot


def _gla_gates(a_rows, wa_ref, ba_ref):
    x = jnp.dot(a_rows.astype(BF16), wa_ref[...], preferred_element_type=F32) + ba_ref[...]
    return jax.nn.log_sigmoid(x) / GLA_TAU


def _gla_finish(o, r, g_ref):
    return (_norm_rows(o, g_ref[...]) * (r * jax.nn.sigmoid(r))).astype(BF16)


def _gla_prompt_kernel(q_ref, k_ref, v_ref, r_ref, a_ref, wa_ref, ba_ref, g_ref, o_ref, s_ref, st_ref, oc_ref):
    c, sc, hp, dk, dv = GLA_CHUNK, GLA_SUPER, GLA_HEAD_PAIR, GLA_DK, GLA_DV
    step = pl.program_id(2)

    @pl.when(step == 0)
    def _():
        st_ref[...] = jnp.zeros(st_ref.shape, F32)

    ti = lax.broadcasted_iota(jnp.int32, (sc, sc), 0)
    si = lax.broadcasted_iota(jnp.int32, (sc, sc), 1)
    causal = (ti // c == si // c) & (si <= ti)
    la = _gla_gates(a_ref[...], wa_ref, ba_ref)
    b, b_last = _chunk_scan(la, c)
    q = q_ref[...] * (dk ** -0.5)
    k = k_ref[...]
    qe = (q * jnp.exp(b)).astype(BF16)
    ke = (k * jnp.exp(-b)).astype(BF16)
    kd = (k * jnp.exp(b_last - b)).astype(BF16)
    vb = v_ref[...].astype(BF16)
    kc = lambda hh: slice(hh * dk, (hh + 1) * dk)
    vc = lambda hh: slice(hh * dv, (hh + 1) * dv)
    o_intra = []
    for hh in range(hp):
        att = jnp.where(causal, lax.dot_general(qe[:, kc(hh)], ke[:, kc(hh)], _NT, preferred_element_type=F32), 0.0)
        o_intra.append(jnp.dot(att.astype(BF16), vb[:, vc(hh)], preferred_element_type=F32))
    for j in range(sc // c):
        rows = slice(j * c, (j + 1) * c)
        for hh in range(hp):
            st = st_ref[hh]
            o_inter = lax.dot_general(qe[rows, kc(hh)], st.astype(BF16), _NT, preferred_element_type=F32)
            oc_ref[rows, vc(hh)] = o_intra[hh][rows] + o_inter
            dec = jnp.exp(b_last[j * c:j * c + 1, kc(hh)])
            st_ref[hh] = dec * st + lax.dot_general(vb[rows, vc(hh)], kd[rows, kc(hh)], _TN,
                                                    preferred_element_type=F32)
    for hh in range(hp):
        o_ref[:, vc(hh)] = _gla_finish(oc_ref[:, vc(hh)], r_ref[:, vc(hh)], g_ref)

    @pl.when(step == pl.num_programs(2) - 1)
    def _():
        for hh in range(hp):
            s_ref[hh] = st_ref[hh].T


def gla_prompt(z, b, t, w_alpha, b_alpha, norm_g):
    dk, dv, h, hp, sc = GLA_DK, GLA_DV, GLA_HEADS, GLA_HEAD_PAIR, GLA_SUPER
    nt = t // sc
    wa = jnp.pad(w_alpha, ((0, LANE - GLA_RANK), (0, 0))).astype(BF16)
    rows = lambda bi, hi, ti: bi * nt + ti
    return pl.pallas_call(
        _gla_prompt_kernel,
        grid=(b, h // hp, nt),
        in_specs=[
            pl.BlockSpec((sc, hp * dk), lambda bi, hi, ti: (rows(bi, hi, ti), hi)),
            pl.BlockSpec((sc, hp * dk), lambda bi, hi, ti: (rows(bi, hi, ti), h // hp + hi)),
            pl.BlockSpec((sc, hp * dv), lambda bi, hi, ti: (rows(bi, hi, ti), 2 * GLA_QK // (hp * dv) + hi)),
            pl.BlockSpec((sc, hp * dv), lambda bi, hi, ti: (rows(bi, hi, ti), (2 * GLA_QK + GLA_VD) // (hp * dv) + hi)),
            pl.BlockSpec((sc, LANE), lambda bi, hi, ti: (rows(bi, hi, ti), (2 * GLA_QK + 2 * GLA_VD) // LANE)),
            pl.BlockSpec((LANE, hp * dk), lambda bi, hi, ti: (0, hi)),
            pl.BlockSpec((1, hp * dk), lambda bi, hi, ti: (0, hi)),
            pl.BlockSpec((1, dv), lambda bi, hi, ti: (0, 0)),
        ],
        out_specs=[
            pl.BlockSpec((sc, hp * dv), lambda bi, hi, ti: (rows(bi, hi, ti), hi)),
            pl.BlockSpec((None, hp, dk, dv), lambda bi, hi, ti: (bi, hi, 0, 0)),
        ],
        out_shape=[jax.ShapeDtypeStruct((b * t, GLA_VD), BF16), jax.ShapeDtypeStruct((b, h, dk, dv), F32)],
        scratch_shapes=[pltpu.VMEM((hp, dv, dk), F32), pltpu.VMEM((sc, hp * dv), F32)],
        compiler_params=_params("parallel", "parallel", "arbitrary"),
        name="gla_prompt",
    )(z, z, z, z, z, wa, b_alpha.reshape(1, -1), norm_g.reshape(1, -1))


def _gla_sample_kernel(nb, t, q_ref, k_ref, v_ref, r_ref, a_ref, wa_ref, ba_ref, g_ref, s0_ref, o_ref, s_ref):
    n = nb * t
    la = _gla_gates(a_ref[...], wa_ref, ba_ref)
    b, b_last = _chunk_scan(la, t)
    q = q_ref[...] * (GLA_DK ** -0.5)
    k = k_ref[...]
    qe = q * jnp.exp(b)
    ke = (k * jnp.exp(-b)).astype(BF16)
    kd = k * jnp.exp(b_last - b)
    vb = v_ref[...].astype(BF16)
    ti = lax.broadcasted_iota(jnp.int32, (n, n), 0)
    si = lax.broadcasted_iota(jnp.int32, (n, n), 1)
    causal = (ti // t == si // t) & (si <= ti)
    att = jnp.where(causal, lax.dot_general(qe.astype(BF16), ke, _NT, preferred_element_type=F32), 0.0)
    o = jnp.dot(att.astype(BF16), vb, preferred_element_type=F32)
    row = lax.broadcasted_iota(jnp.int32, (n, 1), 0) // t
    for i in range(nb):
        mine = row == i
        st = s0_ref[i].T
        qe_i = jnp.where(mine, qe, 0.0).astype(BF16)
        kd_i = jnp.where(mine, kd, 0.0).astype(BF16)
        o = o + lax.dot_general(qe_i, st.astype(BF16), _NT, preferred_element_type=F32)
        dec = jnp.exp(b_last[i * t:i * t + 1, :])
        s_ref[i] = (dec * st + lax.dot_general(vb, kd_i, _TN, preferred_element_type=F32)).T
    o_ref[...] = _gla_finish(o, r_ref[...], g_ref)


def gla_sample(z, nb, t, s0, w_alpha, b_alpha, norm_g):
    dk, dv, h = GLA_DK, GLA_DV, GLA_HEADS
    n = nb * t
    wa = jnp.pad(w_alpha, ((0, LANE - GLA_RANK), (0, 0))).astype(BF16)
    return pl.pallas_call(
        functools.partial(_gla_sample_kernel, nb, t),
        grid=(h,),
        in_specs=[
            pl.BlockSpec((n, dk), lambda hi: (0, hi)),
            pl.BlockSpec((n, dk), lambda hi: (0, h + hi)),
            pl.BlockSpec((n, dv), lambda hi: (0, 2 * GLA_QK // dv + hi)),
            pl.BlockSpec((n, dv), lambda hi: (0, (2 * GLA_QK + GLA_VD) // dv + hi)),
            pl.BlockSpec((n, LANE), lambda hi: (0, (2 * GLA_QK + 2 * GLA_VD) // LANE)),
            pl.BlockSpec((LANE, dk), lambda hi: (0, hi)),
            pl.BlockSpec((1, dk), lambda hi: (0, hi)),
            pl.BlockSpec((1, dv), lambda hi: (0, 0)),
            pl.BlockSpec((nb, None, dk, dv), lambda hi: (0, hi, 0, 0)),
        ],
        out_specs=[
            pl.BlockSpec((n, dv), lambda hi: (0, hi)),
            pl.BlockSpec((nb, None, dk, dv), lambda hi: (0, hi, 0, 0)),
        ],
        out_shape=[jax.ShapeDtypeStruct((n, GLA_VD), BF16), jax.ShapeDtypeStruct((nb, h, dk, dv), F32)],
        compiler_params=_params("parallel"),
        name="gla_sample",
    )(z, z, z, z, z, wa, b_alpha.reshape(1, -1), norm_g.reshape(1, -1), s0)


def nsa_prompt_mix(z, kv, b, t, gate_bias, w_cmp, pe_cmp):
    ckv = compress(z, b, t // CMP_STRIDE, NSA_QD // NSA_HEAD_DIM, w_cmp, pe_cmp)
    o = nsa_prompt_attention(z, ckv, gate_bias, b, t)
    kv = kv.reshape(3, b, t, 2, NSA_KV_HEADS, NSA_HEAD_DIM)
    return o, kv[0], kv[1], kv[2, :, t - min(WINDOW, t):]


def _pad_cols(w, mult):
    n = w.shape[-1]
    return jnp.pad(w, ((0, 0), (0, -(-n // mult) * mult - n)))


def kernel(x_prompt, x_sample, p_prompt, p_sample, cache_cmp_kv, cache_sel_kv, cache_win_kv, state_gla, page_table, ffn1_norm, ffn1_w_gu, ffn1_w_down, mix_norm, nsa_w_in, nsa_gate_bias, nsa_w_cmp, nsa_pe_cmp, nsa_w_out, gla_w_in, gla_w_alpha, gla_b_alpha, gla_norm, gla_w_out, ffn2_norm, ffn2_w_gu, ffn2_w_down, ple_norm, ple_w_gate, ple_w_proj, final_norm):
    bp, tp, d = x_prompt.shape
    bs, ts, _ = x_sample.shape
    mp, ms = bp * tp, bs * ts
    xp = x_prompt.reshape(mp, d)
    xs = x_sample.reshape(ms, d)
    tm_p = _row_tile(mp, 512)
    tm_p2 = _row_tile(mp, 512)

    cmp_p, sel_p, win_p, gla_p = [], [], [], []
    cmp_s, sel_s, win_s, gla_s = [], [], [], []
    for i in range(DEPTH):
        xs, wa, wu, wd = ffn_cast(xs, ffn1_norm[i], ffn1_w_gu, ffn1_w_down, i)
        xp = ffn(xp, ffn1_norm[i], wa, wu, wd, tm_p)
        j = i // 2
        if i % 2 == 0:
            w_out = nsa_w_out[j]
            w_in = _pad_cols(nsa_w_in[j], 512).astype(BF16)
            zs = norm_matmul(xs, mix_norm[i], w_in, ms)
            zp, kvp = nsa_in_proj(xp, mix_norm[i], w_in, tm_p2)
            op, kc, kl, kw = nsa_prompt_mix(zp, kvp, bp, tp, nsa_gate_bias[j], nsa_w_cmp[j], nsa_pe_cmp[j])
            os_, kc2, kl2, kw2 = nsa_sample_mix(zs, bs, ts, cache_cmp_kv[j], cache_sel_kv[j], cache_win_kv[j],
                                                page_table, nsa_gate_bias[j], nsa_w_cmp[j], nsa_pe_cmp[j])
            cmp_p.append(kc)
            sel_p.append(kl)
            win_p.append(kw)
            cmp_s.append(kc2)
            sel_s.append(kl2)
            win_s.append(kw2)
        else:
            w_out = gla_w_out[j]
            w_in = _pad_cols(gla_w_in[j], 512).astype(BF16)
            zs = norm_matmul(xs, mix_norm[i], w_in, ms)
            zp = norm_matmul(xp, mix_norm[i], w_in, tm_p2)
            op, sp = gla_prompt(zp, bp, tp, gla_w_alpha[j], gla_b_alpha[j], gla_norm[j])
            os_, ss = gla_sample(zs, bs, ts, state_gla[j], gla_w_alpha[j], gla_b_alpha[j], gla_norm[j])
            gla_p.append(sp)
            gla_s.append(ss)
        xs, w_out = matmul_residual_cast(os_, w_out, xs)
        xp = matmul_residual(op, w_out, xp, tm_p, tn=1024)
        xs, wa, wu, wd = ffn_cast(xs, ffn2_norm[i], ffn2_w_gu, ffn2_w_down, i)
        xp = ffn(xp, ffn2_norm[i], wa, wu, wd, tm_p)
        xs, w_pg, w_pp = ple_cast(xs, p_sample[i].reshape(ms, -1), ple_norm[i], ple_w_gate, ple_w_proj, i)
        xp = ple(xp, p_prompt[i].reshape(mp, -1), ple_norm[i], w_pg, w_pp, tm_p2, tn=1024)
    y_prompt = rmsnorm(xp, final_norm, tm_p2).reshape(bp, tp, d)
    y_sample = rmsnorm(xs, final_norm, ms).reshape(bs, ts, d)
    return (y_prompt, y_sample, jnp.stack(cmp_p), jnp.stack(sel_p), jnp.stack(win_p), jnp.stack(gla_p),
            jnp.stack(cmp_s), jnp.stack(sel_s), jnp.stack(win_s), jnp.stack(gla_s))
```

```python
import functools

import jax
import jax.numpy as jnp
from jax import lax
from jax.experimental import pallas as pl
from jax.experimental.pallas import tpu as pltpu

F32 = jnp.float32
BF16 = jnp.bfloat16

DEPTH = 2
PAGE_SIZE = 128
NSA_HEADS = 32
NSA_HEAD_DIM = 128
NSA_KV_HEADS = 4
NSA_Q_PER_KV = 8
CMP_STRIDE = 16
CMP_LEN = 32
SEL_BLOCK = 64
N_SEL = 16
WINDOW = 512
NSA_QD = NSA_HEADS * NSA_HEAD_DIM
NSA_KVD = NSA_KV_HEADS * NSA_HEAD_DIM
GLA_HEADS = 8
GLA_DK = 256
GLA_DV = 512
GLA_RANK = 16
GLA_TAU = 16.0
GLA_CHUNK = 32
D_FF = 11008
EPS = 1e-6
NEG_INF = -1e30
FORCE_SCORE = 1e30

V7X_VMEM_LIMIT_BYTES = 56 * 1024 * 1024
LANE = 128


def _params(*sem):
    return pltpu.CompilerParams(dimension_semantics=sem, vmem_limit_bytes=V7X_VMEM_LIMIT_BYTES)


def _norm_rows(x, g):
    ms = jnp.mean(x * x, axis=-1, keepdims=True)
    return (x * lax.rsqrt(ms + EPS)) * g


def _row_tile(m, want):
    return want if m % want == 0 else m


def _ffn_tile(h, wa_ref, wu_ref, wd_ref):
    a = jnp.dot(h, wa_ref[...], preferred_element_type=F32)
    u = jnp.dot(h, wu_ref[...], preferred_element_type=F32)
    act = (0.5 * (a * jax.nn.sigmoid(a)) * u).astype(BF16)
    return jnp.dot(act, wd_ref[...], preferred_element_type=F32)


def _ffn_kernel(x_ref, g_ref, wa_ref, wu_ref, wd_ref, o_ref, h_ref):
    @pl.when(pl.program_id(1) == 0)
    def _():
        x = x_ref[...]
        h_ref[...] = _norm_rows(x, g_ref[...]).astype(BF16)
        o_ref[...] = x

    o_ref[...] += _ffn_tile(h_ref[...], wa_ref, wu_ref, wd_ref)


def ffn(x, g, wa, wu, wd, tm, tf=256):
    m, d = x.shape
    return pl.pallas_call(
        _ffn_kernel,
        grid=(m // tm, D_FF // tf),
        in_specs=[
            pl.BlockSpec((tm, d), lambda i, j: (i, 0), pipeline_mode=pl.Buffered(1)),
            pl.BlockSpec((1, d), lambda i, j: (0, 0)),
            pl.BlockSpec((d, tf), lambda i, j: (0, j)),
            pl.BlockSpec((d, tf), lambda i, j: (0, j)),
            pl.BlockSpec((tf, d), lambda i, j: (j, 0)),
        ],
        out_specs=pl.BlockSpec((tm, d), lambda i, j: (i, 0)),
        out_shape=jax.ShapeDtypeStruct((m, d), F32),
        scratch_shapes=[pltpu.VMEM((tm, d), BF16)],
        compiler_params=_params("parallel", "arbitrary"),
        name="ffn",
    )(x, g.reshape(1, d), wa, wu, wd)


def _ffn_cast_kernel(x_ref, g_ref, wa_ref, wu_ref, wd_ref, o_ref, wa_out, wu_out, wd_out, h_ref):
    wa_out[...] = wa_ref[...].astype(BF16)
    wu_out[...] = wu_ref[...].astype(BF16)
    wd_out[...] = wd_ref[...].astype(BF16)
    _ffn_kernel(x_ref, g_ref, wa_out, wu_out, wd_out, o_ref, h_ref)


def ffn_cast(x, g, w_gu, w_down, layer, tf=256):
    m, d = x.shape
    nf = D_FF // tf
    return pl.pallas_call(
        _ffn_cast_kernel,
        grid=(1, nf),
        in_specs=[
            pl.BlockSpec((m, d), lambda i, j: (0, 0)),
            pl.BlockSpec((1, d), lambda i, j: (0, 0)),
            pl.BlockSpec((None, d, tf), lambda i, j: (layer, 0, j)),
            pl.BlockSpec((None, d, tf), lambda i, j: (layer, 0, j + nf)),
            pl.BlockSpec((None, tf, d), lambda i, j: (layer, j, 0)),
        ],
        out_specs=[
            pl.BlockSpec((m, d), lambda i, j: (0, 0)),
            pl.BlockSpec((d, tf), lambda i, j: (0, j)),
            pl.BlockSpec((d, tf), lambda i, j: (0, j)),
            pl.BlockSpec((tf, d), lambda i, j: (j, 0)),
        ],
        out_shape=[jax.ShapeDtypeStruct((m, d), F32), jax.ShapeDtypeStruct((d, D_FF), BF16),
                   jax.ShapeDtypeStruct((d, D_FF), BF16), jax.ShapeDtypeStruct((D_FF, d), BF16)],
        scratch_shapes=[pltpu.VMEM((m, d), BF16)],
        compiler_params=_params("arbitrary", "arbitrary"),
        name="ffn_cast",
    )(x, g.reshape(1, d), w_gu, w_gu, w_down)


def _norm_matmul_kernel(x_ref, g_ref, w_ref, o_ref, h_ref):
    @pl.when(pl.program_id(1) == 0)
    def _():
        h_ref[...] = _norm_rows(x_ref[...], g_ref[...]).astype(BF16)

    o_ref[...] = jnp.dot(h_ref[...], w_ref[...], preferred_element_type=F32)


def norm_matmul(x, g, w, tm, tn=512):
    m, d = x.shape
    n = w.shape[1]
    return pl.pallas_call(
        _norm_matmul_kernel,
        grid=(m // tm, n // tn),
        in_specs=[
            pl.BlockSpec((tm, d), lambda i, j: (i, 0)),
            pl.BlockSpec((1, d), lambda i, j: (0, 0)),
            pl.BlockSpec((d, tn), lambda i, j: (0, j)),
        ],
        out_specs=pl.BlockSpec((tm, tn), lambda i, j: (i, j)),
        out_shape=jax.ShapeDtypeStruct((m, n), F32),
        scratch_shapes=[pltpu.VMEM((tm, d), BF16)],
        compiler_params=_params("parallel", "arbitrary"),
        name="in_proj",
    )(x, g.reshape(1, d), w)


def _nsa_in_proj_kernel(tn, x_ref, g_ref, w_ref, o_ref, kv_ref, h_ref):
    j = pl.program_id(1)

    @pl.when(j == 0)
    def _():
        h_ref[...] = _norm_rows(x_ref[...], g_ref[...]).astype(BF16)

    res = jnp.dot(h_ref[...], w_ref[...], preferred_element_type=F32)
    o_ref[...] = res

    @pl.when((j >= NSA_QD // tn) & (j < (NSA_QD + 6 * NSA_KVD) // tn))
    def _():
        for g in range(NSA_KV_HEADS):
            kv_ref[:, g, :] = res[:, g * NSA_HEAD_DIM:(g + 1) * NSA_HEAD_DIM]


def nsa_in_proj(x, g, w, tm):
    m, d = x.shape
    n = w.shape[1]
    tn = NSA_KVD
    q_tiles = NSA_QD // tn

    def kv_index(i, j):
        t = jnp.clip(j - q_tiles, 0, 5)
        return (t // 2, i, t % 2, 0, 0)

    return pl.pallas_call(
        functools.partial(_nsa_in_proj_kernel, tn),
        grid=(m // tm, n // tn),
        in_specs=[
            pl.BlockSpec((tm, d), lambda i, j: (i, 0)),
            pl.BlockSpec((1, d), lambda i, j: (0, 0)),
            pl.BlockSpec((d, tn), lambda i, j: (0, j)),
        ],
        out_specs=[
            pl.BlockSpec((tm, tn), lambda i, j: (i, j)),
            pl.BlockSpec((None, tm, None, NSA_KV_HEADS, NSA_HEAD_DIM), kv_index),
        ],
        out_shape=[jax.ShapeDtypeStruct((m, n), F32),
                   jax.ShapeDtypeStruct((3, m, 2, NSA_KV_HEADS, NSA_HEAD_DIM), F32)],
        scratch_shapes=[pltpu.VMEM((tm, d), BF16)],
        compiler_params=_params("parallel", "arbitrary"),
        name="nsa_in_proj",
    )(x, g.reshape(1, d), w)


def _matmul_residual_kernel(a_ref, w_ref, x_ref, o_ref):
    o_ref[...] = x_ref[...] + jnp.dot(a_ref[...], w_ref[...], preferred_element_type=F32)


def matmul_residual(a, w, x, tm, tn=512):
    m, k = a.shape
    n = w.shape[1]
    return pl.pallas_call(
        _matmul_residual_kernel,
        grid=(m // tm, n // tn),
        in_specs=[
            pl.BlockSpec((tm, k), lambda i, j: (i, 0)),
            pl.BlockSpec((k, tn), lambda i, j: (0, j)),
            pl.BlockSpec((tm, tn), lambda i, j: (i, j)),
        ],
        out_specs=pl.BlockSpec((tm, tn), lambda i, j: (i, j)),
        out_shape=jax.ShapeDtypeStruct((m, n), F32),
        compiler_params=_params("parallel", "arbitrary"),
        name="out_proj",
    )(a, w, x)


def _matmul_residual_cast_kernel(a_ref, w_ref, x_ref, o_ref, w_out):
    w_out[...] = w_ref[...].astype(BF16)
    _matmul_residual_kernel(a_ref, w_out, x_ref, o_ref)


def matmul_residual_cast(a, w, x, tn=512):
    m, k = a.shape
    n = w.shape[1]
    return pl.pallas_call(
        _matmul_residual_cast_kernel,
        grid=(n // tn,),
        in_specs=[
            pl.BlockSpec((m, k), lambda j: (0, 0)),
            pl.BlockSpec((k, tn), lambda j: (0, j)),
            pl.BlockSpec((m, tn), lambda j: (0, j)),
        ],
        out_specs=[pl.BlockSpec((m, tn), lambda j: (0, j)), pl.BlockSpec((k, tn), lambda j: (0, j))],
        out_shape=[jax.ShapeDtypeStruct((m, n), F32), jax.ShapeDtypeStruct((k, n), BF16)],
        compiler_params=_params("arbitrary"),
        name="out_proj_cast",
    )(a, w, x)


def _ple_kernel(tn, x_ref, g_ref, wg_ref, p_ref, wp_ref, o_ref, h_ref):
    j = pl.program_id(1)

    @pl.when(j == 0)
    def _():
        h_ref[...] = _norm_rows(x_ref[...], g_ref[...]).astype(BF16)

    gate = jax.nn.sigmoid(jnp.dot(h_ref[...], wg_ref[...], preferred_element_type=F32))
    proj = jnp.dot(p_ref[...].astype(BF16), wp_ref[...], preferred_element_type=F32)
    col = pl.multiple_of(j * tn, tn)
    o_ref[...] = x_ref[:, pl.ds(col, tn)] + gate * proj


def ple(x, p, g, w_gate, w_proj, tm, tn=512):
    m, d = x.shape
    pd = p.shape[1]
    return pl.pallas_call(
        functools.partial(_ple_kernel, tn),
        grid=(m // tm, d // tn),
        in_specs=[
            pl.BlockSpec((tm, d), lambda i, j: (i, 0)),
            pl.BlockSpec((1, d), lambda i, j: (0, 0)),
            pl.BlockSpec((d, tn), lambda i, j: (0, j)),
            pl.BlockSpec((tm, pd), lambda i, j: (i, 0)),
            pl.BlockSpec((pd, tn), lambda i, j: (0, j)),
        ],
        out_specs=pl.BlockSpec((tm, tn), lambda i, j: (i, j)),
        out_shape=jax.ShapeDtypeStruct((m, d), F32),
        scratch_shapes=[pltpu.VMEM((tm, d), BF16)],
        compiler_params=_params("parallel", "arbitrary"),
        name="ple",
    )(x, g.reshape(1, d), w_gate, p, w_proj)


def _ple_cast_kernel(tn, x_ref, g_ref, wg_ref, p_ref, wp_ref, o_ref, wg_out, wp_out, h_ref):
    wg_out[...] = wg_ref[...].astype(BF16)
    wp_out[...] = wp_ref[...].astype(BF16)
    _ple_kernel(tn, x_ref, g_ref, wg_out, p_ref, wp_out, o_ref, h_ref)


def ple_cast(x, p, g, w_gate, w_proj, layer, tn=512):
    m, d = x.shape
    pd = p.shape[1]
    return pl.pallas_call(
        functools.partial(_ple_cast_kernel, tn),
        grid=(1, d // tn),
        in_specs=[
            pl.BlockSpec((m, d), lambda i, j: (0, 0)),
            pl.BlockSpec((1, d), lambda i, j: (0, 0)),
            pl.BlockSpec((None, d, tn), lambda i, j: (layer, 0, j)),
            pl.BlockSpec((m, pd), lambda i, j: (0, 0)),
            pl.BlockSpec((None, pd, tn), lambda i, j: (layer, 0, j)),
        ],
        out_specs=[
            pl.BlockSpec((m, tn), lambda i, j: (0, j)),
            pl.BlockSpec((d, tn), lambda i, j: (0, j)),
            pl.BlockSpec((pd, tn), lambda i, j: (0, j)),
        ],
        out_shape=[jax.ShapeDtypeStruct((m, d), F32), jax.ShapeDtypeStruct((d, d), BF16),
                   jax.ShapeDtypeStruct((pd, d), BF16)],
        scratch_shapes=[pltpu.VMEM((m, d), BF16)],
        compiler_params=_params("arbitrary", "arbitrary"),
        name="ple_cast",
    )(x, g.reshape(1, d), w_gate, p, w_proj)


def _rmsnorm_kernel(x_ref, g_ref, o_ref):
    o_ref[...] = _norm_rows(x_ref[...], g_ref[...])


def rmsnorm(x, g, tm):
    m, d = x.shape
    return pl.pallas_call(
        _rmsnorm_kernel,
        grid=(m // tm,),
        in_specs=[pl.BlockSpec((tm, d), lambda i: (i, 0)), pl.BlockSpec((1, d), lambda i: (0, 0))],
        out_specs=pl.BlockSpec((tm, d), lambda i: (i, 0)),
        out_shape=jax.ShapeDtypeStruct((m, d), F32),
        compiler_params=_params("parallel"),
    )(x, g.reshape(1, d))


Z_GATE_COL = (NSA_QD + 6 * NSA_KVD) // LANE


def _compress_kernel(nc, x_ref, w2_ref, wf_ref, pe_ref, o_ref):
    acc = jnp.zeros((nc, 2 * NSA_HEAD_DIM), F32)
    for l in range(CMP_STRIDE):
        a = x_ref[pl.ds(l, nc, stride=CMP_STRIDE), :]
        acc += jnp.dot(a.astype(BF16), w2_ref[l], preferred_element_type=F32)
    n_idx = lax.broadcasted_iota(jnp.int32, (nc, NSA_HEAD_DIM), 0)
    first = acc[:, :NSA_HEAD_DIM]
    second = jnp.where(n_idx == nc - 1, 0.0, pltpu.roll(acc[:, NSA_HEAD_DIM:], nc - 1, 0))
    bias = jnp.dot(pe_ref[...], wf_ref[...], preferred_element_type=F32)[0:1]
    o_ref[...] = (first + second + bias).astype(BF16)


def compress(x, nb, nc, col0, w_cmp, pe_cmp):
    hd = NSA_HEAD_DIM
    w2 = w_cmp.reshape(2, 2, CMP_STRIDE, hd, hd).transpose(0, 2, 3, 1, 4).reshape(2, CMP_STRIDE, hd, 2 * hd)
    wf = w_cmp.reshape(2, CMP_LEN * hd, hd)
    pe = jnp.broadcast_to(pe_cmp.reshape(2, 1, CMP_LEN * hd), (2, 8, CMP_LEN * hd))
    return pl.pallas_call(
        functools.partial(_compress_kernel, nc),
        grid=(nb, 2, NSA_KV_HEADS),
        in_specs=[
            pl.BlockSpec((CMP_STRIDE * nc, hd), lambda i, c, g: (i, col0 + c * NSA_KV_HEADS + g)),
            pl.BlockSpec((None, CMP_STRIDE, hd, 2 * hd), lambda i, c, g: (c, 0, 0, 0)),
            pl.BlockSpec((None, CMP_LEN * hd, hd), lambda i, c, g: (c, 0, 0)),
            pl.BlockSpec((None, 8, CMP_LEN * hd), lambda i, c, g: (c, 0, 0)),
        ],
        out_specs=pl.BlockSpec((None, None, None, nc, hd), lambda i, c, g: (i, c, g, 0, 0)),
        out_shape=jax.ShapeDtypeStruct((nb, 2, NSA_KV_HEADS, nc, hd), BF16),
        compiler_params=_params("parallel", "parallel", "parallel"),
        name="nsa_compress",
    )(x, w2.astype(BF16), wf.astype(BF16), pe.astype(BF16))


def _nsa_prompt_kernel(t, qb, kb, nc, slope_ref, zq_ref, ks_ref, vs_ref, kw_ref, vw_ref, gz_ref, gb_ref,
                       ck_ref, cv_ref, ov_ref, o_ref, m_ref, l_ref, acc_ref, p_ref):
    g = pl.program_id(1)
    qi = pl.program_id(2)
    t0 = qi * qb
    r8 = NSA_Q_PER_KV
    hd = NSA_HEAD_DIM
    n_slc = t // SEL_BLOCK
    n_cmp = nc - 1

    zq = zq_ref[...] * (hd ** -0.5)
    q2 = jnp.concatenate([zq[:, r * hd:(r + 1) * hd] for r in range(r8)], axis=0).astype(BF16)

    slope = [slope_ref[g * r8 + r] for r in range(r8)]
    tq = lax.broadcasted_iota(jnp.int32, (qb, 1), 0) + t0
    rows = lambda r: slice(r * qb, (r + 1) * qb)

    n_i = lax.broadcasted_iota(jnp.int32, (qb, nc), 1)
    dist_ci = tq - (CMP_STRIDE * n_i + (CMP_LEN - 1))
    vis_c = (dist_ci >= 0) & (n_i < n_cmp)
    dist_c = dist_ci.astype(F32)
    s_all = lax.dot_general(q2, ck_ref[...], (((1,), (1,)), ((), ())), preferred_element_type=F32)
    psum = jnp.zeros((qb, nc), F32)
    for r in range(r8):
        s = jnp.where(vis_c, s_all[rows(r)] - slope[r] * dist_c, NEG_INF)
        e = jnp.exp(s - jnp.max(s, axis=-1, keepdims=True))
        p = jnp.where(vis_c, e / jnp.sum(e, axis=-1, keepdims=True), 0.0)
        psum = psum + p
        p_ref[rows(r), :nc] = p.astype(BF16)
    o_cmp = jnp.dot(p_ref[:, :nc], cv_ref[...], preferred_element_type=F32)

    p_hi = psum.astype(BF16)
    p_mid = (psum - p_hi.astype(F32)).astype(BF16)
    p_lo = (psum - p_hi.astype(F32) - p_mid.astype(F32)).astype(BF16)
    ov = ov_ref[...]
    imp = (jnp.dot(p_hi, ov, preferred_element_type=F32) + jnp.dot(p_mid, ov, preferred_element_type=F32)
           + jnp.dot(p_lo, ov, preferred_element_type=F32))
    nj = -(-n_slc // 8) * 8
    j_t = lax.broadcasted_iota(jnp.int32, (nj, qb), 0)
    cur = (t0 + lax.broadcasted_iota(jnp.int32, (nj, qb), 1)) // SEL_BLOCK
    vis_j = j_t <= cur
    forced = (j_t == 0) | (j_t == cur) | (j_t == cur - 1)
    score = jnp.where(forced, FORCE_SCORE, jnp.where(vis_j, imp.T[:nj], -FORCE_SCORE))
    score = jnp.where(j_t < n_slc, score, -jnp.inf)
    rank = jnp.zeros((nj, qb), jnp.int32)
    for jp in range(n_slc):
        row = score[jp:jp + 1, :]
        ahead = (row > score) | ((row == score) & (jp < j_t))
        rank = rank + ahead.astype(jnp.int32)
    selected = jnp.where((rank < min(N_SEL, n_slc)) & vis_j, 1.0, 0.0).astype(BF16)

    d0 = (lax.broadcasted_iota(jnp.int32, (qb, kb), 0) - lax.broadcasted_iota(jnp.int32, (qb, kb), 1))

    def flash_init():
        m_ref[...] = jnp.full(m_ref.shape, NEG_INF, F32)
        l_ref[...] = jnp.zeros(l_ref.shape, F32)
        acc_ref[...] = jnp.zeros(acc_ref.shape, F32)

    def flash_step(k_ref, v_ref, k0, mask, dist):
        k = k_ref[pl.ds(k0, kb), :].astype(BF16)
        v = v_ref[pl.ds(k0, kb), :].astype(BF16)
        s_all = lax.dot_general(q2, k, (((1,), (1,)), ((), ())), preferred_element_type=F32)
        alphas = []
        for r in range(r8):
            s = jnp.where(mask, s_all[rows(r)] - slope[r] * dist, NEG_INF)
            m_prev = m_ref[r]
            m_next = jnp.maximum(m_prev, jnp.max(s, axis=-1, keepdims=True))
            alpha = jnp.exp(m_prev - m_next)
            p = jnp.exp(s - jnp.concatenate([m_next] * (kb // LANE), axis=1))
            l_ref[r] = alpha * l_ref[r] + jnp.sum(p, axis=-1, keepdims=True)
            m_ref[r] = m_next
            p_ref[rows(r), :kb] = p.astype(BF16)
            alphas.append(alpha)
        pv = jnp.dot(p_ref[:, :kb], v, preferred_element_type=F32)
        for r in range(r8):
            acc_ref[r] = alphas[r] * acc_ref[r] + pv[rows(r)]

    flash_init()

    def sel_body(i, carry):
        k0 = pl.multiple_of(i * kb, kb)
        dist = d0 + (t0 - k0)
        blk_of_key = (lax.broadcasted_iota(jnp.int32, (nj, kb), 1) + k0) // SEL_BLOCK
        expand = jnp.where(blk_of_key == lax.broadcasted_iota(jnp.int32, (nj, kb), 0), 1.0, 0.0).astype(BF16)
        chosen = lax.dot_general(selected, expand, _TN, preferred_element_type=F32)
        flash_step(ks_ref, vs_ref, k0, (chosen > 0.5) & (dist >= 0), dist.astype(F32))
        return carry

    lax.fori_loop(0, (t0 + qb - 1) // kb + 1, sel_body, 0)
    o_sel = [acc_ref[r] / l_ref[r] for r in range(r8)]

    wk = WINDOW + qb
    kw0 = pl.multiple_of(jnp.maximum(t0 - WINDOW, 0), qb)
    k_w = kw_ref[pl.ds(kw0, wk), :].astype(BF16)
    v_w = vw_ref[pl.ds(kw0, wk), :].astype(BF16)
    dist_w = (lax.broadcasted_iota(jnp.int32, (qb, wk), 0) - lax.broadcasted_iota(jnp.int32, (qb, wk), 1)) + (t0 - kw0)
    mask_w = (dist_w >= 0) & (dist_w <= WINDOW)
    dist_wf = dist_w.astype(F32)
    s_all = lax.dot_general(q2, k_w, _NT, preferred_element_type=F32)
    l_win = []
    for r in range(r8):
        s = jnp.where(mask_w, s_all[rows(r)] - slope[r] * dist_wf, NEG_INF)
        p = jnp.exp(s - jnp.max(s, axis=-1, keepdims=True))
        l_win.append(jnp.sum(p, axis=-1, keepdims=True))
        p_ref[rows(r), :wk] = p.astype(BF16)
    pv = jnp.dot(p_ref[:, :wk], v_w, preferred_element_type=F32)
    o_win = [pv[rows(r)] / l_win[r] for r in range(r8)]

    gates = jax.nn.sigmoid(gz_ref[...] + gb_ref[...])
    src = lax.broadcasted_iota(jnp.int32, (LANE, LANE), 0)
    dst = lax.broadcasted_iota(jnp.int32, (LANE, LANE), 1)
    pick = jnp.where((dst < 3 * r8) & (src == (dst // r8) * NSA_HEADS + g * r8 + dst % r8), 1.0, 0.0).astype(BF16)
    g_hi = gates.astype(BF16)
    g_mid = (gates - g_hi.astype(F32)).astype(BF16)
    g_lo = (gates - g_hi.astype(F32) - g_mid.astype(F32)).astype(BF16)
    gsel = (jnp.dot(g_hi, pick, preferred_element_type=F32) + jnp.dot(g_mid, pick, preferred_element_type=F32)
            + jnp.dot(g_lo, pick, preferred_element_type=F32))

    def gate(branch, r):
        k = branch * r8 + r
        return gsel[:, k:k + 1]

    for r in range(r8):
        o = gate(0, r) * o_cmp[rows(r)] + gate(1, r) * o_sel[r] + gate(2, r) * o_win[r]
        o_ref[:, r * hd:(r + 1) * hd] = o.astype(BF16)


def nsa_prompt_attention(z, ckv, gate_bias, b, t, qb=128, kb=512):
    assert t % kb == 0 and t >= WINDOW + qb and WINDOW % qb == 0
    nc = t // CMP_STRIDE
    nq = t // qb
    hd = NSA_HEAD_DIM
    n_slc = t // SEL_BLOCK
    slopes = jnp.exp2(-8.0 * (jnp.arange(NSA_HEADS, dtype=F32) + 1.0) / NSA_HEADS)
    ci = jnp.arange(nc)[:, None]
    sj = jnp.arange(LANE)[None, :]
    ov = ((CMP_STRIDE * ci < SEL_BLOCK * (sj + 1)) & (CMP_STRIDE * ci + CMP_LEN > SEL_BLOCK * sj)
          & (sj < n_slc)).astype(BF16)
    gb = jnp.pad(gate_bias, (0, LANE - gate_bias.shape[0])).reshape(1, LANE)
    kv_spec = lambda off: pl.BlockSpec((t, hd), lambda bi, g, qi: (bi, off + g))
    kv_col = (NSA_QD + 2 * NSA_KVD) // hd
    return pl.pallas_call(
        functools.partial(_nsa_prompt_kernel, t, qb, kb, nc),
        grid=(b, NSA_KV_HEADS, nq),
        in_specs=[
            pl.BlockSpec(memory_space=pltpu.SMEM),
            pl.BlockSpec((qb, NSA_Q_PER_KV * hd), lambda bi, g, qi: (bi * nq + qi, g)),
            kv_spec(kv_col), kv_spec(kv_col + 4), kv_spec(kv_col + 8), kv_spec(kv_col + 12),
            pl.BlockSpec((qb, LANE), lambda bi, g, qi: (bi * nq + qi, Z_GATE_COL)),
            pl.BlockSpec((1, LANE), lambda bi, g, qi: (0, 0)),
            pl.BlockSpec((None, None, None, nc, hd), lambda bi, g, qi: (bi, 0, g, 0, 0)),
            pl.BlockSpec((None, None, None, nc, hd), lambda bi, g, qi: (bi, 1, g, 0, 0)),
            pl.BlockSpec((nc, LANE), lambda bi, g, qi: (0, 0)),
        ],
        out_specs=pl.BlockSpec((qb, NSA_Q_PER_KV * hd), lambda bi, g, qi: (bi * nq + qi, g)),
        out_shape=jax.ShapeDtypeStruct((b * t, NSA_QD), BF16),
        scratch_shapes=[pltpu.VMEM((NSA_Q_PER_KV, qb, LANE), F32), pltpu.VMEM((NSA_Q_PER_KV, qb, LANE), F32),
                        pltpu.VMEM((NSA_Q_PER_KV, qb, hd), F32),
                        pltpu.VMEM((NSA_Q_PER_KV * qb, max(kb, nc, WINDOW + qb)), BF16)],
        compiler_params=_params("parallel", "parallel", "arbitrary"),
        name="nsa_prompt_attn",
    )(slopes, z, z, z, z, z, z, gb, ckv, ckv, ov)


_NT = (((1,), (1,)), ((), ()))
_TN = (((0,), (0,)), ((), ()))


GATHER_PAGES = 4


def _page_gather_kernel(n_steps, pt_ref, *refs):
    page_refs, tail_ref, o_ref = refs[:GATHER_PAGES], refs[GATHER_PAGES], refs[GATHER_PAGES + 1]
    p = pl.program_id(1)

    @pl.when(p < n_steps)
    def _():
        for k, x_ref in enumerate(page_refs):
            for c in range(2):
                for g in range(NSA_KV_HEADS):
                    col = (c * NSA_KV_HEADS + g) * NSA_HEAD_DIM
                    o_ref[k * PAGE_SIZE:(k + 1) * PAGE_SIZE, col:col + NSA_HEAD_DIM] = x_ref[:, c, g, :]

    @pl.when(p == n_steps)
    def _():
        o_ref[...] = tail_ref[...]


def page_gather(cache, page_table, tail):
    nb, n_pages = page_table.shape
    step_rows = GATHER_PAGES * PAGE_SIZE
    n_steps = n_pages // GATHER_PAGES
    w = tail.shape[2]

    def page_spec(k):
        def index(b, p, pt):
            return (pt[b, jnp.minimum(p, n_steps - 1) * GATHER_PAGES + k], 0, 0, 0, 0)
        return pl.BlockSpec((None, PAGE_SIZE, 2, NSA_KV_HEADS, NSA_HEAD_DIM), index)

    return pl.pallas_call(
        functools.partial(_page_gather_kernel, n_steps),
        grid_spec=pltpu.PrefetchScalarGridSpec(
            num_scalar_prefetch=1,
            grid=(nb, n_steps + 1),
            in_specs=[page_spec(k) for k in range(GATHER_PAGES)]
            + [pl.BlockSpec((None, step_rows, w), lambda b, p, pt: (b, 0, 0))],
            out_specs=pl.BlockSpec((step_rows, w), lambda b, p, pt: (b * (n_steps + 1) + p, 0)),
        ),
        out_shape=jax.ShapeDtypeStruct((nb * (n_steps + 1) * step_rows, w), F32),
        compiler_params=_params("parallel", "arbitrary"),
        name="nsa_page_gather",
    )(page_table, *([cache] * GATHER_PAGES), tail)


def _row_slopes(slope_ref, g, n_rows, t):
    r_idx = lax.broadcasted_iota(jnp.int32, (n_rows, 1), 0) // t
    col = jnp.zeros((n_rows, 1), F32)
    for r in range(NSA_Q_PER_KV):
        col = jnp.where(r_idx == r, slope_ref[g * NSA_Q_PER_KV + r], col)
    return col


def _nsa_sample_select_kernel(t, pos0, nc, n_cmp, n_slc, slope_ref, q_ref, ck_ref, cv_ref, ov_ref,
                              ocmp_ref, idx_ref, valid_ref):
    g = pl.program_id(0)
    r8 = NSA_Q_PER_KV
    n_rows = r8 * t
    nsp = ov_ref.shape[1]
    q2 = (q_ref[...] * (NSA_HEAD_DIM ** -0.5)).astype(BF16)
    slope = _row_slopes(slope_ref, g, n_rows, t)
    tq = pos0 + lax.broadcasted_iota(jnp.int32, (n_rows, 1), 0) % t
    n_i = lax.broadcasted_iota(jnp.int32, (n_rows, nc), 1)
    dist_ci = tq - (CMP_STRIDE * n_i + (CMP_LEN - 1))
    vis_c = (dist_ci >= 0) & (n_i < n_cmp)
    s = lax.dot_general(q2, ck_ref[...], _NT, preferred_element_type=F32)
    s = jnp.where(vis_c, s - slope * dist_ci.astype(F32), NEG_INF)
    e = jnp.exp(s - jnp.max(s, axis=-1, keepdims=True))
    p = jnp.where(vis_c, e / jnp.sum(e, axis=-1, keepdims=True), 0.0)
    ocmp_ref[...] = jnp.dot(p.astype(BF16), cv_ref[...], preferred_element_type=F32)

    p_hi = p.astype(BF16)
    p_mid = (p - p_hi.astype(F32)).astype(BF16)
    p_lo = (p - p_hi.astype(F32) - p_mid.astype(F32)).astype(BF16)
    ov = ov_ref[...]
    imp = (jnp.dot(p_hi, ov, preferred_element_type=F32) + jnp.dot(p_mid, ov, preferred_element_type=F32)
           + jnp.dot(p_lo, ov, preferred_element_type=F32))
    sh = t
    while sh < n_rows:
        imp = imp + pltpu.roll(imp, sh, 0)
        sh *= 2
    imp = imp[0:8]
    j_i = lax.broadcasted_iota(jnp.int32, (8, nsp), 1)
    cur = (pos0 + lax.broadcasted_iota(jnp.int32, (8, 1), 0) % t) // SEL_BLOCK
    vis_j = j_i <= cur
    forced = (j_i == 0) | (j_i == cur) | (j_i == cur - 1)
    score = jnp.where(forced, FORCE_SCORE, jnp.where(vis_j, imp, -FORCE_SCORE))
    score = jnp.where(j_i < n_slc, score, -jnp.inf)
    rank = jnp.zeros((8, nsp), jnp.int32)
    for jp in range(n_slc):
        col = score[:, jp:jp + 1]
        ahead = (col > score) | ((col == score) & (jp < j_i))
        rank = rank + ahead.astype(jnp.int32)
    lane = lax.broadcasted_iota(jnp.int32, (8, LANE), 1)
    idx = jnp.zeros((8, LANE), F32)
    valid = jnp.zeros((8, LANE), F32)
    j_f = j_i.astype(F32)
    for k in range(min(N_SEL, n_slc)):
        hit = (rank == k) & (j_i < n_slc)
        idx_k = jnp.sum(jnp.where(hit, j_f, 0.0), axis=-1, keepdims=True)
        valid_k = jnp.sum(jnp.where(hit & vis_j, 1.0, 0.0), axis=-1, keepdims=True)
        idx = jnp.where(lane == k, idx_k, idx)
        valid = jnp.where(lane == k, valid_k, valid)
    idx_ref[...] = idx.astype(jnp.int32)
    valid_ref[...] = valid.astype(jnp.int32)


def nsa_sample_select(q, ckv, nb, t, pos0, n_cmp, n_slc):
    g4, hd = NSA_KV_HEADS, NSA_HEAD_DIM
    nc = ckv.shape[3]
    n_rows = NSA_Q_PER_KV * t
    nsp = -(-n_slc // LANE) * LANE
    slopes = jnp.exp2(-8.0 * (jnp.arange(NSA_HEADS, dtype=F32) + 1.0) / NSA_HEADS)
    ci = jnp.arange(nc)[:, None]
    sj = jnp.arange(nsp)[None, :]
    ov = ((CMP_STRIDE * ci < SEL_BLOCK * (sj + 1)) & (CMP_STRIDE * ci + CMP_LEN > SEL_BLOCK * sj)
          & (sj < n_slc) & (ci < n_cmp)).astype(BF16)
    return pl.pallas_call(
        functools.partial(_nsa_sample_select_kernel, t, pos0, nc, n_cmp, n_slc),
        grid=(g4, nb),
        in_specs=[
            pl.BlockSpec(memory_space=pltpu.SMEM),
            pl.BlockSpec((None, None, n_rows, hd), lambda g, b: (g, b, 0, 0)),
            pl.BlockSpec((None, None, None, nc, hd), lambda g, b: (b, 0, g, 0, 0)),
            pl.BlockSpec((None, None, None, nc, hd), lambda g, b: (b, 1, g, 0, 0)),
            pl.BlockSpec((nc, nsp), lambda g, b: (0, 0)),
        ],
        out_specs=[
            pl.BlockSpec((None, None, n_rows, hd), lambda g, b: (g, b, 0, 0)),
            pl.BlockSpec((None, None, 8, LANE), lambda g, b: (g, b, 0, 0)),
            pl.BlockSpec((None, None, 8, LANE), lambda g, b: (g, b, 0, 0)),
        ],
        out_shape=[jax.ShapeDtypeStruct((g4, nb, n_rows, hd), F32),
                   jax.ShapeDtypeStruct((g4, nb, 8, LANE), jnp.int32),
                   jax.ShapeDtypeStruct((g4, nb, 8, LANE), jnp.int32)],
        compiler_params=_params("parallel", "parallel"),
        name="nsa_sample_select",
    )(slopes, q, ckv, ckv, ov)


def _nsa_sample_attn_kernel(nb, t, pos0, n_pages, wb, idx_ref, valid_ref, pt_ref, slope_ref, q_ref, ocmp_ref,
                            gl_ref, gb_ref, kwin_ref, vwin_ref, kwt_ref, vwt_ref, cache_ref, tail_ref, o_ref,
                            kbuf, vbuf, q2_ref, osel_ref, sem):
    b = pl.program_id(0)
    g = pl.program_id(1)
    r8 = NSA_Q_PER_KV
    hd = NSA_HEAD_DIM
    n_rows = r8 * t
    n_sel = kbuf.shape[1]
    past_blocks = n_pages * (PAGE_SIZE // SEL_BLOCK)
    per_page = PAGE_SIZE // SEL_BLOCK

    def sel_entry(tt, k):
        return ((g * nb + b) * t + tt) * n_sel + k

    def block_copies(tt, k, in_tail):
        i = idx_ref[sel_entry(tt, k)]
        out = []
        for c, buf in ((0, kbuf), (1, vbuf)):
            col = c * NSA_KV_HEADS + g
            if in_tail:
                src = tail_ref.at[b, :, pl.ds(col * hd, hd)]
            else:
                ic = jnp.minimum(i, past_blocks - 1)
                page = pt_ref[b * n_pages + ic // per_page]
                src = cache_ref.at[page, pl.ds((ic % per_page) * SEL_BLOCK, SEL_BLOCK), c, g, :]
            out.append(pltpu.make_async_copy(src, buf.at[tt, k], sem))
        return out

    def for_each_block(fn):
        for tt in range(t):
            for k in range(n_sel):
                in_tail = idx_ref[sel_entry(tt, k)] >= past_blocks

                @pl.when(in_tail)
                def _():
                    for cp in block_copies(tt, k, True):
                        fn(cp)

                @pl.when(jnp.logical_not(in_tail))
                def _():
                    for cp in block_copies(tt, k, False):
                        fn(cp)

    for_each_block(lambda cp: cp.start())

    q2_ref[...] = q_ref[...] * (hd ** -0.5)
    q2 = q2_ref[...].astype(BF16)
    slope = _row_slopes(slope_ref, g, n_rows, t)
    tq = pos0 + lax.broadcasted_iota(jnp.int32, (n_rows, 1), 0) % t

    kw = jnp.concatenate([kwin_ref[...], kwt_ref[...]], axis=0).astype(BF16)
    vw = jnp.concatenate([vwin_ref[...], vwt_ref[...]], axis=0).astype(BF16)
    nk = kw.shape[0]
    s_pos = (pos0 - wb) + lax.broadcasted_iota(jnp.int32, (n_rows, nk), 1)
    dist = tq - s_pos
    mask = (dist >= 0) & (dist <= WINDOW) & (s_pos >= 0)
    s = lax.dot_general(q2, kw, _NT, preferred_element_type=F32)
    s = jnp.where(mask, s - slope * dist.astype(F32), NEG_INF)
    e = jnp.exp(s - jnp.max(s, axis=-1, keepdims=True))
    p = e / jnp.sum(e, axis=-1, keepdims=True)
    o_win = jnp.dot(p.astype(BF16), vw, preferred_element_type=F32)

    for_each_block(lambda cp: cp.wait())

    nkeys = n_sel * SEL_BLOCK
    key_lane = lax.broadcasted_iota(jnp.int32, (1, nkeys), 1)
    slope8 = _row_slopes(slope_ref, g, r8, 1)
    for tt in range(t):
        blk = jnp.zeros((1, nkeys), jnp.int32)
        ok = jnp.zeros((1, nkeys), jnp.int32)
        for k in range(n_sel):
            here = key_lane // SEL_BLOCK == k
            blk = jnp.where(here, idx_ref[sel_entry(tt, k)], blk)
            ok = jnp.where(here, valid_ref[sel_entry(tt, k)], ok)
        dist = (pos0 + tt) - (blk * SEL_BLOCK + key_lane % SEL_BLOCK)
        mask = (dist >= 0) & (ok > 0)
        q_t = q2_ref[pl.ds(tt, r8, stride=t), :].astype(BF16)
        k_t = kbuf[tt].reshape(nkeys, hd).astype(BF16)
        v_t = vbuf[tt].reshape(nkeys, hd).astype(BF16)
        s = lax.dot_general(q_t, k_t, _NT, preferred_element_type=F32)
        s = jnp.where(mask, s - slope8 * dist.astype(F32), NEG_INF)
        e = jnp.exp(s - jnp.max(s, axis=-1, keepdims=True))
        p = e / jnp.sum(e, axis=-1, keepdims=True)
        osel_ref[pl.ds(tt, r8, stride=t), :] = jnp.dot(p.astype(BF16), v_t, preferred_element_type=F32)

    gates = jax.nn.sigmoid(gl_ref[...] + gb_ref[...])
    o = gates[:, 0:1] * ocmp_ref[...] + gates[:, 1:2] * osel_ref[...] + gates[:, 2:3] * o_win
    o_ref[...] = o.astype(BF16)


def nsa_sample_attention(q, o_cmp, gl, gb, idx, valid, page_table, cache_sel, tail_sel, cache_win, tail_win,
                         nb, t, pos0):
    g4, hd = NSA_KV_HEADS, NSA_HEAD_DIM
    n_rows = NSA_Q_PER_KV * t
    n_pages = page_table.shape[1]
    n_sel = idx.shape[-1]
    wb = cache_win.shape[1]
    slopes = jnp.exp2(-8.0 * (jnp.arange(NSA_HEADS, dtype=F32) + 1.0) / NSA_HEADS)
    row_spec = lambda w: pl.BlockSpec((None, None, n_rows, w), lambda b, g, *_: (g, b, 0, 0))
    win_spec = lambda rows, c: pl.BlockSpec((None, rows, hd), lambda b, g, *_: (b, 0, c * g4 + g))
    return pl.pallas_call(
        functools.partial(_nsa_sample_attn_kernel, nb, t, pos0, n_pages, wb),
        grid_spec=pltpu.PrefetchScalarGridSpec(
            num_scalar_prefetch=3,
            grid=(nb, g4),
            in_specs=[
                pl.BlockSpec(memory_space=pltpu.SMEM),
                row_spec(hd), row_spec(hd), row_spec(3),
                pl.BlockSpec((None, n_rows, 3), lambda b, g, *_: (g, 0, 0)),
                win_spec(wb, 0), win_spec(wb, 1), win_spec(LANE, 0), win_spec(LANE, 1),
                pl.BlockSpec(memory_space=pl.ANY), pl.BlockSpec(memory_space=pl.ANY),
            ],
            out_specs=row_spec(hd),
            scratch_shapes=[pltpu.VMEM((t, n_sel, SEL_BLOCK, hd), F32), pltpu.VMEM((t, n_sel, SEL_BLOCK, hd), F32),
                            pltpu.VMEM((n_rows, hd), F32), pltpu.VMEM((n_rows, hd), F32),
                            pltpu.SemaphoreType.DMA(())],
        ),
        out_shape=jax.ShapeDtypeStruct((g4, nb, n_rows, hd), BF16),
        compiler_params=_params("arbitrary", "arbitrary"),
        name="nsa_sample_attn",
    )(idx.reshape(-1), valid.reshape(-1), page_table.reshape(-1), slopes, q, o_cmp, gl, gb,
      cache_win, cache_win, tail_win, tail_win, cache_sel, tail_sel)


def nsa_sample_mix(z, nb, t, cache_cmp, cache_sel, cache_win, page_table, gate_bias, w_cmp, pe_cmp):
    g4, r8, hd = NSA_KV_HEADS, NSA_Q_PER_KV, NSA_HEAD_DIM
    n_pages = page_table.shape[1]
    past = n_pages * PAGE_SIZE
    wb = cache_win.shape[1]
    t_all = past + t
    tp = -(-t_all // SEL_BLOCK) * SEL_BLOCK
    n_slc = tp // SEL_BLOCK
    n_cmp = tp // CMP_STRIDE - 1
    gather_rows = (n_pages + GATHER_PAGES) * PAGE_SIZE
    nc = gather_rows // CMP_STRIDE
    kv = z[:, NSA_QD:NSA_QD + 6 * NSA_KVD].reshape(nb, t, 3, 2 * NSA_KVD)

    tail_cmp = jnp.pad(kv[:, :, 0], ((0, 0), (0, gather_rows - past - t), (0, 0)))
    rows = page_gather(cache_cmp, page_table, tail_cmp)
    ckv = compress(rows, nb, nc, 0, w_cmp, pe_cmp)

    q = z[:, :NSA_QD].reshape(nb, t, g4, r8, hd).transpose(2, 0, 3, 1, 4).reshape(g4, nb, r8 * t, hd)
    o_cmp, idx, valid = nsa_sample_select(q, ckv, nb, t, past, n_cmp, n_slc)
    n_sel = min(N_SEL, n_slc)
    idx = idx[:, :, :t, :n_sel]
    valid = valid[:, :, :t, :n_sel]

    gl = z[:, NSA_QD + 6 * NSA_KVD:NSA_QD + 6 * NSA_KVD + 3 * NSA_HEADS].reshape(nb, t, 3, g4, r8)
    gl = gl.transpose(3, 0, 4, 1, 2).reshape(g4, nb, r8 * t, 3)
    gb = jnp.broadcast_to(gate_bias.reshape(3, g4, r8, 1), (3, g4, r8, t)).transpose(1, 2, 3, 0).reshape(g4, r8 * t, 3)
    tail_sel = jnp.pad(kv[:, :, 1], ((0, 0), (0, SEL_BLOCK - t), (0, 0)))
    tail_win = jnp.pad(kv[:, :, 2], ((0, 0), (0, LANE - t), (0, 0)))
    o = nsa_sample_attention(q, o_cmp, gl, gb, idx, valid, page_table,
                             cache_sel, tail_sel,
                             cache_win.reshape(nb, wb, 2 * NSA_KVD), tail_win, nb, t, past)
    o = o.reshape(g4, nb, r8, t, hd).transpose(1, 3, 0, 2, 4).reshape(nb * t, NSA_QD)
    kv6 = kv.reshape(nb, t, 3, 2, g4, hd)
    win_new = jnp.concatenate([cache_win, kv6[:, :, 2]], axis=1)[:, t:]
    return o, kv6[:, :, 0], kv6[:, :, 1], win_new


GLA_QK = GLA_HEADS * GLA_DK
GLA_VD = GLA_HEADS * GLA_DV
GLA_SUPER = 256
GLA_HEAD_PAIR = 2


def _chunk_scan(la, c):
    n = la.shape[0]
    ri = lax.broadcasted_iota(jnp.int32, la.shape, 0) % c
    b = la
    s = 1
    while s < c:
        b = b + jnp.where(ri >= s, pltpu.roll(b, s, 0), 0.0)
        s *= 2
    tot = jnp.where(ri == c - 1, b, 0.0)
    s = 1
    while s < c:
        tot = tot + jnp.where(ri + s < c, pltpu.roll(tot, n - s, 0), 0.0)
        s *= 2
    return b, tot


def _gla_gates(a_rows, wa_ref, ba_ref):
    x = jnp.dot(a_rows.astype(BF16), wa_ref[...], preferred_element_type=F32) + ba_ref[...]
    return jax.nn.log_sigmoid(x) / GLA_TAU


def _gla_finish(o, r, g_ref):
    return (_norm_rows(o, g_ref[...]) * (r * jax.nn.sigmoid(r))).astype(BF16)


def _gla_prompt_kernel(q_ref, k_ref, v_ref, r_ref, a_ref, wa_ref, ba_ref, g_ref, o_ref, s_ref, st_ref, oc_ref):
    c, sc, hp, dk, dv = GLA_CHUNK, GLA_SUPER, GLA_HEAD_PAIR, GLA_DK, GLA_DV
    step = pl.program_id(2)

    @pl.when(step == 0)
    def _():
        st_ref[...] = jnp.zeros(st_ref.shape, F32)

    ti = lax.broadcasted_iota(jnp.int32, (sc, sc), 0)
    si = lax.broadcasted_iota(jnp.int32, (sc, sc), 1)
    causal = (ti // c == si // c) & (si <= ti)
    la = _gla_gates(a_ref[...], wa_ref, ba_ref)
    b, b_last = _chunk_scan(la, c)
    q = q_ref[...] * (dk ** -0.5)
    k = k_ref[...]
    qe = (q * jnp.exp(b)).astype(BF16)
    ke = (k * jnp.exp(-b)).astype(BF16)
    kd = (k * jnp.exp(b_last - b)).astype(BF16)
    vb = v_ref[...].astype(BF16)
    kc = lambda hh: slice(hh * dk, (hh + 1) * dk)
    vc = lambda hh: slice(hh * dv, (hh + 1) * dv)
    o_intra = []
    for hh in range(hp):
        att = jnp.where(causal, lax.dot_general(qe[:, kc(hh)], ke[:, kc(hh)], _NT, preferred_element_type=F32), 0.0)
        o_intra.append(jnp.dot(att.astype(BF16), vb[:, vc(hh)], preferred_element_type=F32))
    for j in range(sc // c):
        rows = slice(j * c, (j + 1) * c)
        for hh in range(hp):
            st = st_ref[hh]
            o_inter = lax.dot_general(qe[rows, kc(hh)], st.astype(BF16), _NT, preferred_element_type=F32)
            oc_ref[rows, vc(hh)] = o_intra[hh][rows] + o_inter
            dec = jnp.exp(b_last[j * c:j * c + 1, kc(hh)])
            st_ref[hh] = dec * st + lax.dot_general(vb[rows, vc(hh)], kd[rows, kc(hh)], _TN,
                                                    preferred_element_type=F32)
    for hh in range(hp):
        o_ref[:, vc(hh)] = _gla_finish(oc_ref[:, vc(hh)], r_ref[:, vc(hh)], g_ref)

    @pl.when(step == pl.num_programs(2) - 1)
    def _():
        for hh in range(hp):
            s_ref[hh] = st_ref[hh].T


def gla_prompt(z, b, t, w_alpha, b_alpha, norm_g):
    dk, dv, h, hp, sc = GLA_DK, GLA_DV, GLA_HEADS, GLA_HEAD_PAIR, GLA_SUPER
    nt = t // sc
    wa = jnp.pad(w_alpha, ((0, LANE - GLA_RANK), (0, 0))).astype(BF16)
    rows = lambda bi, hi, ti: bi * nt + ti
    return pl.pallas_call(
        _gla_prompt_kernel,
        grid=(b, h // hp, nt),
        in_specs=[
            pl.BlockSpec((sc, hp * dk), lambda bi, hi, ti: (rows(bi, hi, ti), hi)),
            pl.BlockSpec((sc, hp * dk), lambda bi, hi, ti: (rows(bi, hi, ti), h // hp + hi)),
            pl.BlockSpec((sc, hp * dv), lambda bi, hi, ti: (rows(bi, hi, ti), 2 * GLA_QK // (hp * dv) + hi)),
            pl.BlockSpec((sc, hp * dv), lambda bi, hi, ti: (rows(bi, hi, ti), (2 * GLA_QK + GLA_VD) // (hp * dv) + hi)),
            pl.BlockSpec((sc, LANE), lambda bi, hi, ti: (rows(bi, hi, ti), (2 * GLA_QK + 2 * GLA_VD) // LANE)),
            pl.BlockSpec((LANE, hp * dk), lambda bi, hi, ti: (0, hi)),
            pl.BlockSpec((1, hp * dk), lambda bi, hi, ti: (0, hi)),
            pl.BlockSpec((1, dv), lambda bi, hi, ti: (0, 0)),
        ],
        out_specs=[
            pl.BlockSpec((sc, hp * dv), lambda bi, hi, ti: (rows(bi, hi, ti), hi)),
            pl.BlockSpec((None, hp, dk, dv), lambda bi, hi, ti: (bi, hi, 0, 0)),
        ],
        out_shape=[jax.ShapeDtypeStruct((b * t, GLA_VD), BF16), jax.ShapeDtypeStruct((b, h, dk, dv), F32)],
        scratch_shapes=[pltpu.VMEM((hp, dv, dk), F32), pltpu.VMEM((sc, hp * dv), F32)],
        compiler_params=_params("parallel", "parallel", "arbitrary"),
        name="gla_prompt",
    )(z, z, z, z, z, wa, b_alpha.reshape(1, -1), norm_g.reshape(1, -1))


def _gla_sample_kernel(nb, t, q_ref, k_ref, v_ref, r_ref, a_ref, wa_ref, ba_ref, g_ref, s0_ref, o_ref, s_ref):
    n = nb * t
    la = _gla_gates(a_ref[...], wa_ref, ba_ref)
    b, b_last = _chunk_scan(la, t)
    q = q_ref[...] * (GLA_DK ** -0.5)
    k = k_ref[...]
    qe = q * jnp.exp(b)
    ke = (k * jnp.exp(-b)).astype(BF16)
    kd = k * jnp.exp(b_last - b)
    vb = v_ref[...].astype(BF16)
    ti = lax.broadcasted_iota(jnp.int32, (n, n), 0)
    si = lax.broadcasted_iota(jnp.int32, (n, n), 1)
    causal = (ti // t == si // t) & (si <= ti)
    att = jnp.where(causal, lax.dot_general(qe.astype(BF16), ke, _NT, preferred_element_type=F32), 0.0)
    o = jnp.dot(att.astype(BF16), vb, preferred_element_type=F32)
    row = lax.broadcasted_iota(jnp.int32, (n, 1), 0) // t
    for i in range(nb):
        mine = row == i
        st = s0_ref[i].T
        qe_i = jnp.where(mine, qe, 0.0).astype(BF16)
        kd_i = jnp.where(mine, kd, 0.0).astype(BF16)
        o = o + lax.dot_general(qe_i, st.astype(BF16), _NT, preferred_element_type=F32)
        dec = jnp.exp(b_last[i * t:i * t + 1, :])
        s_ref[i] = (dec * st + lax.dot_general(vb, kd_i, _TN, preferred_element_type=F32)).T
    o_ref[...] = _gla_finish(o, r_ref[...], g_ref)


def gla_sample(z, nb, t, s0, w_alpha, b_alpha, norm_g):
    dk, dv, h = GLA_DK, GLA_DV, GLA_HEADS
    n = nb * t
    wa = jnp.pad(w_alpha, ((0, LANE - GLA_RANK), (0, 0))).astype(BF16)
    return pl.pallas_call(
        functools.partial(_gla_sample_kernel, nb, t),
        grid=(h,),
        in_specs=[
            pl.BlockSpec((n, dk), lambda hi: (0, hi)),
            pl.BlockSpec((n, dk), lambda hi: (0, h + hi)),
            pl.BlockSpec((n, dv), lambda hi: (0, 2 * GLA_QK // dv + hi)),
            pl.BlockSpec((n, dv), lambda hi: (0, (2 * GLA_QK + GLA_VD) // dv + hi)),
            pl.BlockSpec((n, LANE), lambda hi: (0, (2 * GLA_QK + 2 * GLA_VD) // LANE)),
            pl.BlockSpec((LANE, dk), lambda hi: (0, hi)),
            pl.BlockSpec((1, dk), lambda hi: (0, hi)),
            pl.BlockSpec((1, dv), lambda hi: (0, 0)),
            pl.BlockSpec((nb, None, dk, dv), lambda hi: (0, hi, 0, 0)),
        ],
        out_specs=[
            pl.BlockSpec((n, dv), lambda hi: (0, hi)),
            pl.BlockSpec((nb, None, dk, dv), lambda hi: (0, hi, 0, 0)),
        ],
        out_shape=[jax.ShapeDtypeStruct((n, GLA_VD), BF16), jax.ShapeDtypeStruct((nb, h, dk, dv), F32)],
        compiler_params=_params("parallel"),
        name="gla_sample",
    )(z, z, z, z, z, wa, b_alpha.reshape(1, -1), norm_g.reshape(1, -1), s0)


def nsa_prompt_mix(z, kv, b, t, gate_bias, w_cmp, pe_cmp):
    ckv = compress(z, b, t // CMP_STRIDE, NSA_QD // NSA_HEAD_DIM, w_cmp, pe_cmp)
    o = nsa_prompt_attention(z, ckv, gate_bias, b, t)
    kv = kv.reshape(3, b, t, 2, NSA_KV_HEADS, NSA_HEAD_DIM)
    return o, kv[0], kv[1], kv[2, :, t - min(WINDOW, t):]


def _pad_cols(w, mult):
    n = w.shape[-1]
    return jnp.pad(w, ((0, 0), (0, -(-n // mult) * mult - n)))


def kernel(x_prompt, x_sample, p_prompt, p_sample, cache_cmp_kv, cache_sel_kv, cache_win_kv, state_gla, page_table, ffn1_norm, ffn1_w_gu, ffn1_w_down, mix_norm, nsa_w_in, nsa_gate_bias, nsa_w_cmp, nsa_pe_cmp, nsa_w_out, gla_w_in, gla_w_alpha, gla_b_alpha, gla_norm, gla_w_out, ffn2_norm, ffn2_w_gu, ffn2_w_down, ple_norm, ple_w_gate, ple_w_proj, final_norm):
    bp, tp, d = x_prompt.shape
    bs, ts, _ = x_sample.shape
    mp, ms = bp * tp, bs * ts
    xp = x_prompt.reshape(mp, d)
    xs = x_sample.reshape(ms, d)
    tm_p = _row_tile(mp, 512)

    cmp_p, sel_p, win_p, gla_p = [], [], [], []
    cmp_s, sel_s, win_s, gla_s = [], [], [], []
    for i in range(DEPTH):
        xs, wa, wu, wd = ffn_cast(xs, ffn1_norm[i], ffn1_w_gu, ffn1_w_down, i)
        xp = ffn(xp, ffn1_norm[i], wa, wu, wd, tm_p)
        j = i // 2
        if i % 2 == 0:
            w_out = nsa_w_out[j]
            w_in = _pad_cols(nsa_w_in[j], 512).astype(BF16)
            zs = norm_matmul(xs, mix_norm[i], w_in, ms)
            zp, kvp = nsa_in_proj(xp, mix_norm[i], w_in, tm_p)
            op, kc, kl, kw = nsa_prompt_mix(zp, kvp, bp, tp, nsa_gate_bias[j], nsa_w_cmp[j], nsa_pe_cmp[j])
            os_, kc2, kl2, kw2 = nsa_sample_mix(zs, bs, ts, cache_cmp_kv[j], cache_sel_kv[j], cache_win_kv[j],
                                                page_table, nsa_gate_bias[j], nsa_w_cmp[j], nsa_pe_cmp[j])
            cmp_p.append(kc)
            sel_p.append(kl)
            win_p.append(kw)
            cmp_s.append(kc2)
            sel_s.append(kl2)
            win_s.append(kw2)
        else:
            w_out = gla_w_out[j]
            w_in = _pad_cols(gla_w_in[j], 1024).astype(BF16)
            zs = norm_matmul(xs, mix_norm[i], w_in, ms, tn=1024)
            zp = norm_matmul(xp, mix_norm[i], w_in, tm_p, tn=1024)
            op, sp = gla_prompt(zp, bp, tp, gla_w_alpha[j], gla_b_alpha[j], gla_norm[j])
            os_, ss = gla_sample(zs, bs, ts, state_gla[j], gla_w_alpha[j], gla_b_alpha[j], gla_norm[j])
            gla_p.append(sp)
            gla_s.append(ss)
        xs, w_out = matmul_residual_cast(os_, w_out, xs)
        xp = matmul_residual(op, w_out, xp, tm_p, tn=1024)
        xs, wa, wu, wd = ffn_cast(xs, ffn2_norm[i], ffn2_w_gu, ffn2_w_down, i)
        xp = ffn(xp, ffn2_norm[i], wa, wu, wd, tm_p)
        xs, w_pg, w_pp = ple_cast(xs, p_sample[i].reshape(ms, -1), ple_norm[i], ple_w_gate, ple_w_proj, i)
        xp = ple(xp, p_prompt[i].reshape(mp, -1), ple_norm[i], w_pg, w_pp, tm_p, tn=1024)
    y_prompt = rmsnorm(xp, final_norm, tm_p).reshape(bp, tp, d)
    y_sample = rmsnorm(xs, final_norm, ms).reshape(bs, ts, d)
    return (y_prompt, y_sample, jnp.stack(cmp_p), jnp.stack(sel_p), jnp.stack(win_p), jnp.stack(gla_p),
            jnp.stack(cmp_s), jnp.stack(sel_s), jnp.stack(win_s), jnp.stack(gla_s))
```
